```python
import math
import jax
import jax.numpy as jnp
from jax import lax
import numpy as np

D_MODEL = 2048
BATCH = 4
SEQ = 2048
DEPTH = 4
DEC_BATCH = 128
DEC_SEQ = 1
PAST_LEN = 16384
PAGE_SIZE = 128

D_MIX = 2 * D_MODEL
BRANCH = D_MIX // 4
CONV_W = 4
EPS = 1e-6
TINY = 1e-30

HG_HEADS = 8
HG_DK = 128
HG_DV = BRANCH // HG_HEADS
HG_FDIM = HG_HEADS * HG_DK
HG_CHUNK = 16

RG_BLOCKS = 8
RG_BW = BRANCH // RG_BLOCKS
RG_C = 8.0

GLA_HEADS = 4
GLA_KDIM = BRANCH // 2
GLA_DK = GLA_KDIM // GLA_HEADS
GLA_DV = BRANCH // GLA_HEADS
GLA_RANK = 16
GLA_TAU = 16.0
GLA_CHUNK = 16

SSD_HEADDIM = 64
SSD_HEADS = BRANCH // SSD_HEADDIM
SSD_GROUPS = 2
SSD_DSTATE = 128
SSD_CONV_DIM = BRANCH + 2 * SSD_GROUPS * SSD_DSTATE
SSD_CHUNK = 64

IN_SIZES = (HG_FDIM, HG_FDIM, BRANCH, BRANCH,
            BRANCH, BRANCH,
            GLA_KDIM, GLA_KDIM, BRANCH, BRANCH, GLA_RANK,
            BRANCH, SSD_CONV_DIM, SSD_HEADS)
N_IN = sum(IN_SIZES)
IN_SPLIT_POINTS = tuple(int(v) for v in np.cumsum(IN_SIZES)[:-1])

kernel_name = 'hybrid_hgrn2_rglru_gla_ssd_step'


def rmsnorm(x, w):
    xf = x.astype(jnp.float32)
    y = xf * lax.rsqrt(jnp.mean(xf * xf, axis=-1, keepdims=True) + EPS)
    return (y * w.astype(jnp.float32)).astype(x.dtype)


def group_rmsnorm(x, w, groups):
    shp = x.shape
    xg = x.astype(jnp.float32).reshape(shp[:-1] + (groups, shp[-1] // groups))
    xg = xg * lax.rsqrt(jnp.mean(xg * xg, axis=-1, keepdims=True) + EPS)
    return xg.reshape(shp) * w.astype(jnp.float32)


def pad_time(a, t_pad):
    extra = t_pad - a.shape[1]
    if extra == 0:
        return a
    widths = [(0, 0)] * a.ndim
    widths[1] = (0, extra)
    return jnp.pad(a, widths)


def causal_conv(x, buf, w, b):
    seq = x.shape[1]
    xc = jnp.concatenate([buf.astype(x.dtype), x], axis=1)
    y = xc[:, 0:seq] * w[0]
    for j in range(1, CONV_W):
        y = y + xc[:, j:j + seq] * w[j]
    return y + b, xc[:, seq:]


def chunked_gated_linear_attention(q, k, v, log_f, s0, chunk):
    bsz, seq, nh, dk = q.shape
    dv = v.shape[-1]
    c = min(chunk, seq)
    n = -(-seq // c)
    t_pad = n * c
    q, k, v, log_f = (pad_time(a, t_pad) for a in (q, k, v, log_f))
    rs = lambda a: a.reshape(bsz, n, c, nh, a.shape[-1])
    q, k, v, log_f = rs(q), rs(k), rs(v), rs(log_f)
    b = jnp.cumsum(log_f, axis=2)
    b_last = b[:, :, -1:]
    b_ref = b[:, :, c // 2:c // 2 + 1]
    q_rel = q * jnp.exp(b - b_ref)
    k_rel = k * jnp.exp(b_ref - b)
    q_in = q * jnp.exp(b)
    k_end = k * jnp.exp(b_last - b)
    causal = jnp.tril(jnp.ones((c, c), dtype=bool))
    scores = jnp.einsum('bnthk,bnshk->bnhts', q_rel, k_rel)
    scores = jnp.where(causal, scores, 0.0)
    o_intra = jnp.einsum('bnhts,bnshv->bnthv', scores, v)
    decay = jnp.exp(b_last[:, :, 0])

    def step(s, inp):
        qc, kc, vc, dc = inp
        o = jnp.einsum('bthk,bhkv->bthv', qc, s)
        s = dc[..., None] * s + jnp.einsum('bshk,bshv->bhkv', kc, vc)
        return s, o

    xs = tuple(jnp.moveaxis(a, 1, 0) for a in (q_in, k_end, v, decay))
    s_fin, o_inter = lax.scan(step, s0.astype(jnp.float32), xs)
    o = o_intra + jnp.moveaxis(o_inter, 0, 1)
    return o.reshape(bsz, t_pad, nh, dv)[:, :seq], s_fin


def ssd_chunked(x, dt, a_head, b_in, c_in, d_skip, s0, chunk):
    bsz, seq, nh, hp = x.shape
    ng, ds = b_in.shape[2], b_in.shape[3]
    r = nh // ng
    c = min(chunk, seq)
    n = -(-seq // c)
    t_pad = n * c
    x, dt, b_in, c_in = (pad_time(a, t_pad) for a in (x, dt, b_in, c_in))
    xr = x.reshape(bsz, n, c, ng, r, hp)
    dtr = dt.reshape(bsz, n, c, ng, r)
    br = b_in.reshape(bsz, n, c, ng, ds)
    cr = c_in.reshape(bsz, n, c, ng, ds)
    cum = jnp.cumsum(dtr * a_head.reshape(ng, r), axis=2)
    xdt = xr * dtr[..., None]
    causal = jnp.tril(jnp.ones((c, c), dtype=bool))[None, None, :, :, None, None]
    seg = cum[:, :, :, None] - cum[:, :, None, :]
    lmat = jnp.where(causal, jnp.exp(jnp.where(causal, seg, 0.0)), 0.0)
    cb = jnp.einsum('bctgd,bcsgd->bcgts', cr, br)
    y_intra = jnp.einsum('bcgts,bctsgr,bcsgrp->bctgrp', cb, lmat, xdt)
    dec_in = jnp.exp(cum)
    w_end = jnp.exp(cum[:, :, -1:] - cum)
    dec_chunk = jnp.exp(cum[:, :, -1])

    def step(s, inp):
        cc, di, bc, wc, xc, dc = inp
        y = jnp.einsum('btgd,btgr,bgrpd->btgrp', cc, di, s)
        s = dc[..., None, None] * s + jnp.einsum('bsgd,bsgr,bsgrp->bgrpd', bc, wc, xc)
        return s, y

    s_init = s0.astype(jnp.float32).reshape(bsz, ng, r, hp, ds)
    xs = tuple(jnp.moveaxis(a, 1, 0) for a in (cr, dec_in, br, w_end, xdt, dec_chunk))
    s_fin, y_inter = lax.scan(step, s_init, xs)
    y = y_intra + jnp.moveaxis(y_inter, 0, 1) + xr * d_skip.reshape(ng, r)[..., None]
    return y.reshape(bsz, t_pad, nh, hp)[:, :seq], s_fin.reshape(bsz, nh, hp, ds)


def _linear_combine(e1, e2):
    a1, b1 = e1
    a2, b2 = e2
    return a1 * a2, a2 * b1 + b2


def rg_lru(x, h0, w_r, b_r, w_i, b_i, lam):
    bsz, seq, width = x.shape
    xb = x.reshape(bsz, seq, RG_BLOCKS, RG_BW)
    r = jax.nn.sigmoid(jnp.einsum('btni,nij->btnj', xb, w_r) + b_r).reshape(bsz, seq, width)
    i = jax.nn.sigmoid(jnp.einsum('btni,nij->btnj', xb, w_i) + b_i).reshape(bsz, seq, width)
    log_a = -RG_C * r * jax.nn.softplus(-lam.astype(jnp.float32))
    a = jnp.exp(log_a)
    u = jnp.sqrt(jnp.maximum(-jnp.expm1(2.0 * log_a), 0.0)) * (i * x)
    u = u.at[:, 0].add(a[:, 0] * h0.astype(jnp.float32))
    _, h = lax.associative_scan(_linear_combine, (a, u), axis=1)
    return h, h[:, -1]


def hybrid_layer(x, s_hg, s_rg, s_rgc, s_gla, s_ssd, s_ssdc, lb,
                 rms_w, w_in, hg_norm, rg_conv_w, rg_conv_b, rg_w_r, rg_b_r,
                 rg_w_i, rg_b_i, rg_lambda, gla_w_up, gla_b_up, gla_norm,
                 ssd_conv_w, ssd_conv_b, ssd_dt_bias, ssd_a_log, ssd_d, ssd_norm, w_out):
    bsz, seq, _ = x.shape
    silu = jax.nn.silu
    heads = lambda a, nh: a.reshape(bsz, seq, nh, a.shape[-1] // nh)
    h = rmsnorm(x, rms_w)
    proj = jnp.einsum('btd,de->bte', h, w_in).astype(jnp.float32)
    (hg_q, hg_f, hg_i, hg_g, rg_x, rg_g, gl_q, gl_k, gl_v, gl_g, gl_a,
     ss_z, ss_xbc, ss_dt) = jnp.split(proj, IN_SPLIT_POINTS, axis=-1)

    lb = lb.astype(jnp.float32)
    f_gate = lb + (1.0 - lb) * jax.nn.sigmoid(hg_f)
    log_f = jnp.log(jnp.maximum(f_gate, TINY))
    k_hg = (1.0 - lb) * jax.nn.sigmoid(-hg_f)
    o_hg, s_hg_new = chunked_gated_linear_attention(
        heads(silu(hg_q), HG_HEADS), heads(k_hg, HG_HEADS), heads(hg_i, HG_HEADS),
        heads(log_f, HG_HEADS), s_hg, HG_CHUNK)
    y_hg = group_rmsnorm(o_hg.reshape(bsz, seq, BRANCH), hg_norm, HG_HEADS) * silu(hg_g)

    rg_xc, s_rgc_new = causal_conv(rg_x, s_rgc, rg_conv_w, rg_conv_b)
    h_rg, s_rg_new = rg_lru(rg_xc, s_rg, rg_w_r, rg_b_r, rg_w_i, rg_b_i, rg_lambda)
    y_rg = h_rg * silu(rg_g)

    log_a = jax.nn.log_sigmoid(jnp.einsum('btr,rk->btk', gl_a, gla_w_up) + gla_b_up) / GLA_TAU
    o_gl, s_gla_new = chunked_gated_linear_attention(
        heads(gl_q * GLA_DK ** -0.5, GLA_HEADS), heads(gl_k, GLA_HEADS), heads(gl_v, GLA_HEADS),
        heads(log_a, GLA_HEADS), s_gla, GLA_CHUNK)
    y_gl = group_rmsnorm(o_gl.reshape(bsz, seq, BRANCH), gla_norm, GLA_HEADS) * silu(gl_g)

    xbc, s_ssdc_new = causal_conv(ss_xbc, s_ssdc, ssd_conv_w, ssd_conv_b)
    xbc = silu(xbc)
    ss_x, ss_b, ss_c = jnp.split(xbc, (BRANCH, BRANCH + SSD_GROUPS * SSD_DSTATE), axis=-1)
    dt = jax.nn.softplus(ss_dt + ssd_dt_bias)
    a_head = -jnp.exp(ssd_a_log.astype(jnp.float32))
    y_ss, s_ssd_new = ssd_chunked(heads(ss_x, SSD_HEADS), dt, a_head, heads(ss_b, SSD_GROUPS),
                                  heads(ss_c, SSD_GROUPS), ssd_d, s_ssd, SSD_CHUNK)
    y_ss = group_rmsnorm(y_ss.reshape(bsz, seq, BRANCH) * silu(ss_z), ssd_norm, SSD_GROUPS)

    mix = jnp.concatenate([y_hg, y_rg, y_gl, y_ss], axis=-1).astype(w_out.dtype)
    x = x + jnp.einsum('bte,ed->btd', mix, w_out).astype(x.dtype)
    return x, (s_hg_new, s_rg_new, s_rgc_new, s_gla_new, s_ssd_new, s_ssdc_new)


def run_trunk(x, st_hg, st_rg, st_rgc, st_gla, st_ssd, st_ssdc, lb_all, layer_weights, rms_final):
    new = ([], [], [], [], [], [])
    for l in range(DEPTH):
        x, states = hybrid_layer(x, st_hg[l], st_rg[l], st_rgc[l], st_gla[l], st_ssd[l], st_ssdc[l],
                                 lb_all[l], *(w[l] for w in layer_weights))
        for lst, s in zip(new, states):
            lst.append(s)
    y = rmsnorm(x, rms_final)
    return y, tuple(jnp.stack(lst) for lst in new)


def setup_inputs(seed: int = 0) -> dict:
    key = jax.random.key(seed)
    ks = list(jax.random.split(key, 40))
    f32 = jnp.float32
    nrm = lambda shape, scale: scale * jax.random.normal(ks.pop(), shape, f32)
    x_prompt = nrm((BATCH, SEQ, D_MODEL), 1.0)
    x_sample = nrm((DEC_BATCH, DEC_SEQ, D_MODEL), 1.0)
    state_hgrn = nrm((DEPTH, DEC_BATCH, HG_HEADS, HG_DK, HG_DV), 0.5)
    state_rglru = nrm((DEPTH, DEC_BATCH, BRANCH), 0.5)
    state_rglru_conv = nrm((DEPTH, DEC_BATCH, CONV_W - 1, BRANCH), 1.0)
    state_gla = nrm((DEPTH, DEC_BATCH, GLA_HEADS, GLA_DK, GLA_DV), 0.5)
    state_ssd = nrm((DEPTH, DEC_BATCH, SSD_HEADS, SSD_HEADDIM, SSD_DSTATE), 0.1)
    state_ssd_conv = nrm((DEPTH, DEC_BATCH, CONV_W - 1, SSD_CONV_DIM), 1.0)
    rms_in = 1.0 + nrm((DEPTH, D_MODEL), 0.02)
    w_in = nrm((DEPTH, D_MODEL, N_IN), D_MODEL ** -0.5)
    hgrn_lower_bounds = 1.0 + nrm((DEPTH, HG_FDIM), 0.1)
    hgrn_norm = 1.0 + nrm((DEPTH, BRANCH), 0.02)
    rglru_conv_w = nrm((DEPTH, CONV_W, BRANCH), CONV_W ** -0.5)
    rglru_conv_b = nrm((DEPTH, BRANCH), 0.01)
    rglru_w_r = nrm((DEPTH, RG_BLOCKS, RG_BW, RG_BW), RG_BW ** -0.5)
    rglru_b_r = nrm((DEPTH, RG_BLOCKS, RG_BW), 0.01)
    rglru_w_i = nrm((DEPTH, RG_BLOCKS, RG_BW, RG_BW), RG_BW ** -0.5)
    rglru_b_i = nrm((DEPTH, RG_BLOCKS, RG_BW), 0.01)
    u = jax.random.uniform(ks.pop(), (DEPTH, BRANCH), f32, 0.9, 0.999)
    a_base = u ** (1.0 / RG_C)
    rglru_lambda = jnp.log(a_base) - jnp.log1p(-a_base)
    gla_w_up = nrm((DEPTH, GLA_RANK, GLA_KDIM), GLA_RANK ** -0.5)
    gla_b_up = nrm((DEPTH, GLA_KDIM), 0.01)
    gla_norm = 1.0 + nrm((DEPTH, BRANCH), 0.02)
    ssd_conv_w = nrm((DEPTH, CONV_W, SSD_CONV_DIM), CONV_W ** -0.5)
    ssd_conv_b = nrm((DEPTH, SSD_CONV_DIM), 0.01)
    dt0 = jnp.exp(jax.random.uniform(ks.pop(), (DEPTH, SSD_HEADS), f32, math.log(1e-3), math.log(1e-1)))
    ssd_dt_bias = dt0 + jnp.log(-jnp.expm1(-dt0))
    ssd_a_log = jnp.log(jax.random.uniform(ks.pop(), (DEPTH, SSD_HEADS), f32, 1.0, 16.0))
    ssd_d = 1.0 + nrm((DEPTH, SSD_HEADS), 0.01)
    ssd_norm = 1.0 + nrm((DEPTH, BRANCH), 0.02)
    w_out = nrm((DEPTH, D_MIX, D_MODEL), D_MIX ** -0.5)
    rms_final = 1.0 + nrm((D_MODEL,), 0.02)
    return {'x_prompt': x_prompt, 'x_sample': x_sample,
            'state_hgrn': state_hgrn, 'state_rglru': state_rglru, 'state_rglru_conv': state_rglru_conv,
            'state_gla': state_gla, 'state_ssd': state_ssd, 'state_ssd_conv': state_ssd_conv,
            'rms_in': rms_in, 'w_in': w_in, 'hgrn_lower_bounds': hgrn_lower_bounds, 'hgrn_norm': hgrn_norm,
            'rglru_conv_w': rglru_conv_w, 'rglru_conv_b': rglru_conv_b,
            'rglru_w_r': rglru_w_r, 'rglru_b_r': rglru_b_r, 'rglru_w_i': rglru_w_i, 'rglru_b_i': rglru_b_i,
            'rglru_lambda': rglru_lambda, 'gla_w_up': gla_w_up, 'gla_b_up': gla_b_up, 'gla_norm': gla_norm,
            'ssd_conv_w': ssd_conv_w, 'ssd_conv_b': ssd_conv_b, 'ssd_dt_bias': ssd_dt_bias,
            'ssd_a_log': ssd_a_log, 'ssd_d': ssd_d, 'ssd_norm': ssd_norm, 'w_out': w_out,
            'rms_final': rms_final}


def reference(x_prompt, x_sample, state_hgrn, state_rglru, state_rglru_conv, state_gla,
              state_ssd, state_ssd_conv, rms_in, w_in, hgrn_lower_bounds, hgrn_norm,
              rglru_conv_w, rglru_conv_b, rglru_w_r, rglru_b_r, rglru_w_i, rglru_b_i,
              rglru_lambda, gla_w_up, gla_b_up, gla_norm, ssd_conv_w, ssd_conv_b,
              ssd_dt_bias, ssd_a_log, ssd_d, ssd_norm, w_out, rms_final):
    p = jax.nn.softmax(hgrn_lower_bounds.astype(jnp.float32), axis=0)
    lb_all = jnp.cumsum(p, axis=0) - p[0:1]
    layer_weights = (rms_in, w_in, hgrn_norm, rglru_conv_w, rglru_conv_b, rglru_w_r, rglru_b_r,
                     rglru_w_i, rglru_b_i, rglru_lambda, gla_w_up, gla_b_up, gla_norm,
                     ssd_conv_w, ssd_conv_b, ssd_dt_bias, ssd_a_log, ssd_d, ssd_norm, w_out)
    zeros = lambda s: jnp.zeros((DEPTH, BATCH) + s.shape[2:], jnp.float32)
    y_prompt, (p_hg, p_rg, p_rgc, p_gla, p_ssd, p_ssdc) = run_trunk(
        x_prompt, zeros(state_hgrn), zeros(state_rglru), zeros(state_rglru_conv), zeros(state_gla),
        zeros(state_ssd), zeros(state_ssd_conv), lb_all, layer_weights, rms_final)
    y_sample, (s_hg, s_rg, s_rgc, s_gla, s_ssd, s_ssdc) = run_trunk(
        x_sample, state_hgrn, state_rglru, state_rglru_conv, state_gla, state_ssd, state_ssd_conv,
        lb_all, layer_weights, rms_final)
    return (y_prompt, y_sample, p_hg, p_rg, p_rgc, p_gla, p_ssd, p_ssdc,
            s_hg, s_rg, s_rgc, s_gla, s_ssd, s_ssdc)
```

```python
import functools
import math

import numpy as np
import jax
import jax.numpy as jnp
from jax import lax
from jax.experimental import pallas as pl
from jax.experimental.pallas import tpu as pltpu

F32 = jnp.float32
BF16 = jnp.bfloat16

D_MODEL = 2048
DEPTH = 4
BRANCH = 1024
D_MIX = 4 * BRANCH
CONV_W = 4
EPS = 1e-6
TINY = 1e-30

HG_HEADS, HG_DK, HG_DV = 8, 128, 128
RG_BLOCKS, RG_BW, RG_C = 8, 128, 8.0
GLA_HEADS, GLA_DK, GLA_DV, GLA_RANK, GLA_TAU = 4, 128, 256, 16, 16.0
GLA_KDIM = GLA_HEADS * GLA_DK
SSD_HEADS, SSD_P, SSD_G, SSD_N = 16, 64, 2, 128
SSD_BC = 2 * SSD_G * SSD_N
SSD_CONV_DIM = BRANCH + SSD_BC

ORIG_GLA_A = 9216
ORIG_SSD_Z = 9232
ORIG_SSD_DT = 11792
N_IN = 11808
N_PROJ = 12288
COL_SMALL = 11776
DT_LANE = 16
BLK_HG_Q, BLK_HG_F, BLK_HG_I, BLK_HG_G = 0, 1, 2, 3
BLK_RG_X, BLK_RG_G = 4, 5
BLK_GL_Q, BLK_GL_K = 12, 13
BLK_GL_V, BLK_GL_G = 7, 8
BLK_SS_Z, BLK_SS_X = 9, 10
BLK_SS_BC = 22
BLK_SMALL = COL_SMALL // 128

CHUNK = 128
SUB = 16
VMEM_LIMIT = 52 * 1024 * 1024


def _cparams(*sem):
    return pltpu.CompilerParams(dimension_semantics=sem, vmem_limit_bytes=VMEM_LIMIT)


def _sigmoid(x):
    return 1.0 / (1.0 + jnp.exp(-x))


def _silu(x):
    return x * _sigmoid(x)


def _softplus(x):
    return jnp.maximum(x, 0.0) + jnp.log(1.0 + jnp.exp(-jnp.abs(x)))


def _dot(a, b):
    return jnp.dot(a, b, preferred_element_type=F32)


def _dot_nt(a, b):
    return lax.dot_general(a, b, (((1,), (1,)), ((), ())), preferred_element_type=F32)


def _split3(a):
    a0 = a.astype(BF16)
    r1 = a - a0.astype(F32)
    a1 = r1.astype(BF16)
    a2 = (r1 - a1.astype(F32)).astype(BF16)
    return a0, a1, a2


def _sel_left(sel, x):
    x0, x1, x2 = _split3(x)
    return _dot(sel, x0) + _dot(sel, x1) + _dot(sel, x2)


def _sel_right(x, sel):
    x0, x1, x2 = _split3(x)
    return _dot(x0, sel) + _dot(x1, sel) + _dot(x2, sel)


def _group_norm(y, w, width):
    parts = []
    for g in range(y.shape[1] // width):
        yg = y[:, g * width:(g + 1) * width]
        ms = jnp.mean(yg * yg, axis=-1, keepdims=True)
        parts.append(yg * lax.rsqrt(ms + EPS))
    out = parts[0] if len(parts) == 1 else jnp.concatenate(parts, axis=1)
    return out * w


def _tri_const(c):
    return jnp.asarray(np.tril(np.ones((c, c), np.float32)), dtype=BF16)


def _level_const(c):
    t = np.arange(c)[:, None]
    s = np.arange(c)[None, :]
    lvl = np.zeros((c, c), np.int32)
    lvl[(t // SUB == s // SUB) & (s <= t)] = 1
    h, code = SUB, 2
    while h < c:
        m = (t // (2 * h) == s // (2 * h)) & (t % (2 * h) >= h) & (s % (2 * h) < h)
        lvl[m] = code
        h *= 2
        code += 1
    return jnp.asarray(lvl)


def _lb_kernel(p_ref, o_ref):
    x = p_ref[...]
    m = jnp.max(x, axis=0, keepdims=True)
    e = jnp.exp(x - m)
    p = e / jnp.sum(e, axis=0, keepdims=True)
    acc = jnp.zeros_like(p[0:1])
    rows = [acc]
    for l in range(1, DEPTH):
        acc = acc + p[l:l + 1]
        rows.append(acc)
    o_ref[...] = jnp.concatenate(rows, axis=0)


def _lower_bounds(param):
    return pl.pallas_call(
        _lb_kernel, out_shape=jax.ShapeDtypeStruct(param.shape, F32), name="hgrn_lb")(param)


def _inproj_kernel(x_ref, rw_ref, w_ref, o_ref, h_scr, *, rb):
    @pl.when(pl.program_id(1) == 0)
    def _():
        def body(i, carry):
            r = pl.multiple_of(i * rb, rb)
            x = x_ref[pl.ds(r, rb), :]
            ms = jnp.mean(x * x, axis=-1, keepdims=True)
            h_scr[pl.ds(r, rb), :] = (x * lax.rsqrt(ms + EPS) * rw_ref[...]).astype(BF16)
            return carry
        lax.fori_loop(0, x_ref.shape[0] // rb, body, 0)

    o_ref[...] = _dot(h_scr[...], w_ref[...])


def _inproj(x, rms_w, w, *, tm, tn):
    m, d = x.shape
    n = w.shape[1]
    return pl.pallas_call(
        functools.partial(_inproj_kernel, rb=min(tm, 64)),
        grid=(m // tm, n // tn),
        in_specs=[pl.BlockSpec((tm, d), lambda i, j: (i, 0)),
                  pl.BlockSpec((1, d), lambda i, j: (0, 0)),
                  pl.BlockSpec((d, tn), lambda i, j: (0, j))],
        out_specs=pl.BlockSpec((tm, tn), lambda i, j: (i, j)),
        out_shape=jax.ShapeDtypeStruct((m, n), F32),
        scratch_shapes=[pltpu.VMEM((tm, d), BF16)],
        compiler_params=_cparams("parallel", "arbitrary"),
        name="inproj",
    )(x, rms_w, w)


def _outproj_kernel(y0_ref, y1_ref, y2_ref, y3_ref, w_ref, x_ref, o_ref):
    acc = x_ref[...]
    for g, y_ref in enumerate((y0_ref, y1_ref, y2_ref, y3_ref)):
        acc = acc + _dot(y_ref[...], w_ref[g * BRANCH:(g + 1) * BRANCH, :])
    o_ref[...] = acc


def _outproj(ys, w, x, *, tm, tn):
    m, d = x.shape
    yspec = pl.BlockSpec((tm, BRANCH), lambda i, j: (i, 0))
    return pl.pallas_call(
        _outproj_kernel,
        grid=(m // tm, d // tn),
        in_specs=[yspec, yspec, yspec, yspec,
                  pl.BlockSpec((D_MIX, tn), lambda i, j: (0, j)),
                  pl.BlockSpec((tm, tn), lambda i, j: (i, j))],
        out_specs=pl.BlockSpec((tm, tn), lambda i, j: (i, j)),
        out_shape=jax.ShapeDtypeStruct((m, d), F32),
        compiler_params=_cparams("parallel", "arbitrary"),
        name="outproj",
    )(*ys, w, x)


def _rmsnorm_kernel(x_ref, w_ref, o_ref):
    x = x_ref[...]
    ms = jnp.mean(x * x, axis=-1, keepdims=True)
    o_ref[...] = x * lax.rsqrt(ms + EPS) * w_ref[...]


def _rmsnorm(x, w, *, tm):
    m, d = x.shape
    return pl.pallas_call(
        _rmsnorm_kernel,
        grid=(m // tm,),
        in_specs=[pl.BlockSpec((tm, d), lambda i: (i, 0)),
                  pl.BlockSpec((1, d), lambda i: (0, 0))],
        out_specs=pl.BlockSpec((tm, d), lambda i: (i, 0)),
        out_shape=jax.ShapeDtypeStruct((m, d), F32),
        compiler_params=_cparams("parallel"),
        name="final_rmsnorm",
    )(x, w)


def _gla_chunk(q, k, v, logf, st, tri, lvl):
    c = q.shape[0]
    b = _sel_left(tri, logf)

    def ref_rows(rows, span):
        return jnp.concatenate(
            [jnp.broadcast_to(b[r:r + 1, :], (span, b.shape[1])) for r in rows], axis=0)

    ed = b - ref_rows([SUB * m + SUB // 2 for m in range(c // SUB)], SUB)
    s = _dot_nt((q * jnp.exp(ed)).astype(BF16), (k * jnp.exp(-ed)).astype(BF16))
    scores = jnp.where(lvl == 1, s, 0.0)
    half, code = SUB, 2
    while half < c:
        ref = ref_rows([2 * half * m + half - 1 for m in range(c // (2 * half))], 2 * half)
        e = jnp.exp(-jnp.abs(b - ref))
        s = _dot_nt((q * e).astype(BF16), (k * e).astype(BF16))
        scores = jnp.where(lvl == code, s, scores)
        half *= 2
        code += 1
    b_last = b[c - 1:c, :]
    q_in = (q * jnp.exp(b)).astype(BF16)
    k_end = (k * jnp.exp(b_last - b)).astype(BF16)
    o = _dot(scores.astype(BF16), v.astype(BF16)) + _dot_nt(q_in, st.astype(BF16))
    st_new = st * jnp.exp(b_last) + _dot(v.T.astype(BF16), k_end)
    return o, st_new


def _hgrn_prompt_kernel(q_ref, f_ref, i_ref, g_ref, lb_ref, nw_ref, tri_ref, lvl_ref,
                        y_ref, s_ref, st_scr):
    c = pl.program_id(1)

    @pl.when(c == 0)
    def _():
        st_scr[...] = jnp.zeros_like(st_scr)

    tri = tri_ref[...]
    lvl = lvl_ref[...]
    for h in range(HG_HEADS):
        sl = slice(h * HG_DK, (h + 1) * HG_DK)
        f = f_ref[:, sl]
        lb = lb_ref[:, sl]
        fg = lb + (1.0 - lb) * _sigmoid(f)
        logf = jnp.log(jnp.maximum(fg, TINY))
        k = (1.0 - lb) * _sigmoid(-f)
        q = _silu(q_ref[:, sl])
        o, st_new = _gla_chunk(q, k, i_ref[:, sl], logf, st_scr[h], tri, lvl)
        st_scr[h] = st_new
        y = _group_norm(o, nw_ref[:, sl], HG_DV) * _silu(g_ref[:, sl])
        y_ref[:, sl] = y.astype(BF16)

    @pl.when(c == pl.num_programs(1) - 1)
    def _():
        for h in range(HG_HEADS):
            s_ref[0, h] = st_scr[h].T


def _hgrn_prompt(proj, lb, nw, bsz, seq):
    nc = seq // CHUNK
    blk = lambda j: pl.BlockSpec((CHUNK, BRANCH), lambda b, c, j=j: (b * nc + c, j))
    row = pl.BlockSpec((1, BRANCH), lambda b, c: (0, 0))
    cc = pl.BlockSpec((CHUNK, CHUNK), lambda b, c: (0, 0))
    return pl.pallas_call(
        _hgrn_prompt_kernel,
        grid=(bsz, nc),
        in_specs=[blk(BLK_HG_Q), blk(BLK_HG_F), blk(BLK_HG_I), blk(BLK_HG_G), row, row, cc, cc],
        out_specs=[pl.BlockSpec((CHUNK, BRANCH), lambda b, c: (b * nc + c, 0)),
                   pl.BlockSpec((1, HG_HEADS, HG_DK, HG_DV), lambda b, c: (b, 0, 0, 0))],
        out_shape=[jax.ShapeDtypeStruct((bsz * seq, BRANCH), BF16),
                   jax.ShapeDtypeStruct((bsz, HG_HEADS, HG_DK, HG_DV), F32)],
        scratch_shapes=[pltpu.VMEM((HG_HEADS, HG_DV, HG_DK), F32)],
        compiler_params=_cparams("parallel", "arbitrary"),
        name="hgrn_prompt",
    )(proj, proj, proj, proj, lb, nw, _tri_const(CHUNK), _level_const(CHUNK))


def _gla_prompt_kernel(q_ref, k_ref, v_ref, g_ref, sm_ref, wup_ref, bup_ref, nw_ref,
                       tri_ref, lvl_ref, y_ref, s_ref, st_scr):
    c = pl.program_id(1)

    @pl.when(c == 0)
    def _():
        st_scr[...] = jnp.zeros_like(st_scr)

    tri = tri_ref[...]
    lvl = lvl_ref[...]
    up = _dot(sm_ref[...].astype(BF16), wup_ref[...]) + bup_ref[...]
    log_a = -_softplus(-up) * (1.0 / GLA_TAU)
    for h in range(GLA_HEADS):
        ks = slice(h * GLA_DK, (h + 1) * GLA_DK)
        vs = slice(h * GLA_DV, (h + 1) * GLA_DV)
        q = q_ref[:, ks] * (GLA_DK ** -0.5)
        o, st_new = _gla_chunk(q, k_ref[:, ks], v_ref[:, vs], log_a[:, ks], st_scr[h], tri, lvl)
        st_scr[h] = st_new
        y = _group_norm(o, nw_ref[:, vs], GLA_DV) * _silu(g_ref[:, vs])
        y_ref[:, vs] = y.astype(BF16)

    @pl.when(c == pl.num_programs(1) - 1)
    def _():
        for h in range(GLA_HEADS):
            s_ref[0, h] = st_scr[h].T


def _gla_prompt(proj, wup, bup, nw, bsz, seq):
    nc = seq // CHUNK
    blk = lambda w, j: pl.BlockSpec((CHUNK, w), lambda b, c, j=j: (b * nc + c, j))
    const = lambda shape: pl.BlockSpec(shape, lambda b, c: (0,) * len(shape))
    return pl.pallas_call(
        _gla_prompt_kernel,
        grid=(bsz, nc),
        in_specs=[blk(GLA_KDIM, BLK_GL_Q), blk(GLA_KDIM, BLK_GL_K), blk(BRANCH, BLK_GL_V),
                  blk(BRANCH, BLK_GL_G), blk(128, BLK_SMALL),
                  const((128, GLA_KDIM)), const((1, GLA_KDIM)), const((1, BRANCH)),
                  const((CHUNK, CHUNK)), const((CHUNK, CHUNK))],
        out_specs=[pl.BlockSpec((CHUNK, BRANCH), lambda b, c: (b * nc + c, 0)),
                   pl.BlockSpec((1, GLA_HEADS, GLA_DK, GLA_DV), lambda b, c: (b, 0, 0, 0))],
        out_shape=[jax.ShapeDtypeStruct((bsz * seq, BRANCH), BF16),
                   jax.ShapeDtypeStruct((bsz, GLA_HEADS, GLA_DK, GLA_DV), F32)],
        scratch_shapes=[pltpu.VMEM((GLA_HEADS, GLA_DV, GLA_DK), F32)],
        compiler_params=_cparams("parallel", "arbitrary"),
        name="gla_prompt",
    )(proj, proj, proj, proj, proj, wup, bup, nw, _tri_const(CHUNK), _level_const(CHUNK))


def _chunk_conv(x_ref, buf, w_ref, b_ref, first):
    c = x_ref.shape[0]

    @pl.when(first)
    def _():
        buf[0:8, :] = jnp.zeros((8, buf.shape[1]), F32)

    buf[8:8 + c, :] = x_ref[...]
    y = b_ref[...] + buf[5:5 + c, :] * w_ref[0:1, :]
    for j in range(1, CONV_W):
        y = y + buf[5 + j:5 + j + c, :] * w_ref[j:j + 1, :]
    tail = buf[c:c + 8, :]
    buf[0:8, :] = tail
    return y, tail[5:8, :]


def _rglru_gates(xc, wr_ref, br_ref, wi_ref, bi_ref, lam_ref):
    a_parts, u_parts = [], []
    for n in range(RG_BLOCKS):
        sl = slice(n * RG_BW, (n + 1) * RG_BW)
        xb = xc[:, sl]
        xb16 = xb.astype(BF16)
        r = _sigmoid(_dot(xb16, wr_ref[n]) + br_ref[:, sl])
        i = _sigmoid(_dot(xb16, wi_ref[n]) + bi_ref[:, sl])
        log_a = -RG_C * r * _softplus(-lam_ref[:, sl])
        a = jnp.exp(log_a)
        one_m_a2 = -jnp.tanh(log_a) * (a * a + 1.0)
        a_parts.append(a)
        u_parts.append(jnp.sqrt(jnp.maximum(one_m_a2, 0.0)) * (i * xb))
    return a_parts, u_parts


def _rglru_prompt_kernel(x_ref, g_ref, cw_ref, cb_ref, wr_ref, br_ref, wi_ref, bi_ref, lam_ref,
                         y_ref, h_ref, cs_ref, buf, h_scr):
    c = pl.program_id(1)
    first = c == 0

    @pl.when(first)
    def _():
        h_scr[...] = jnp.zeros_like(h_scr)

    xc, tail = _chunk_conv(x_ref, buf, cw_ref, cb_ref, first)
    cs_ref[0] = tail
    a_parts, u_parts = _rglru_gates(xc, wr_ref, br_ref, wi_ref, bi_ref, lam_ref)
    n_rows = xc.shape[0]
    row = lax.broadcasted_iota(jnp.int32, (n_rows, RG_BW), 0)
    for n in range(RG_BLOCKS):
        sl = slice(n * RG_BW, (n + 1) * RG_BW)
        a, u = a_parts[n], u_parts[n]
        s = 1
        while s < n_rows:
            keep = row >= s
            a_sh = jnp.where(keep, pltpu.roll(a, s, 0), 1.0)
            u_sh = jnp.where(keep, pltpu.roll(u, s, 0), 0.0)
            u = a * u_sh + u
            a = a * a_sh
            s *= 2
        h = a * h_scr[:, sl] + u
        h_scr[:, sl] = h[n_rows - 1:n_rows, :]
        y_ref[:, sl] = (h * _silu(g_ref[:, sl])).astype(BF16)
    h_ref[0] = h_scr[...]


def _rglru_prompt(proj, cw, cb, wr, br, wi, bi, lam, bsz, seq):
    nc = seq // CHUNK
    blk = lambda j: pl.BlockSpec((CHUNK, BRANCH), lambda b, c, j=j: (b * nc + c, j))
    const = lambda shape: pl.BlockSpec(shape, lambda b, c: (0,) * len(shape))
    return pl.pallas_call(
        _rglru_prompt_kernel,
        grid=(bsz, nc),
        in_specs=[blk(BLK_RG_X), blk(BLK_RG_G), const((CONV_W, BRANCH)), const((1, BRANCH)),
                  const((RG_BLOCKS, RG_BW, RG_BW)), const((1, BRANCH)),
                  const((RG_BLOCKS, RG_BW, RG_BW)), const((1, BRANCH)), const((1, BRANCH))],
        out_specs=[pl.BlockSpec((CHUNK, BRANCH), lambda b, c: (b * nc + c, 0)),
                   pl.BlockSpec((1, 1, BRANCH), lambda b, c: (b, 0, 0)),
                   pl.BlockSpec((1, CONV_W - 1, BRANCH), lambda b, c: (b, 0, 0))],
        out_shape=[jax.ShapeDtypeStruct((bsz * seq, BRANCH), BF16),
                   jax.ShapeDtypeStruct((bsz, 1, BRANCH), F32),
                   jax.ShapeDtypeStruct((bsz, CONV_W - 1, BRANCH), F32)],
        scratch_shapes=[pltpu.VMEM((8 + CHUNK, BRANCH), F32), pltpu.VMEM((1, BRANCH), F32)],
        compiler_params=_cparams("parallel", "arbitrary"),
        name="rglru_prompt",
    )(proj, proj, cw, cb, wr, br, wi, bi, lam)


def _ssd_prompt_kernel(z_ref, x_ref, bc_ref, sm_ref, cwx_ref, cbx_ref, cwb_ref, cbb_ref,
                       dtb_ref, a_ref, d_ref, nw_ref, tri_ref, exp_ref,
                       y_ref, s_ref, cs_ref, st_scr, xbuf, bcbuf):
    c = pl.program_id(1)
    first = c == 0
    n_rows = x_ref.shape[0]
    gw = BRANCH // SSD_G

    @pl.when(first)
    def _():
        st_scr[...] = jnp.zeros_like(st_scr)

    xc, xtail = _chunk_conv(x_ref, xbuf, cwx_ref, cbx_ref, first)
    bcc, bctail = _chunk_conv(bc_ref, bcbuf, cwb_ref, cbb_ref, first)
    cs_ref[0, :, 0:BRANCH] = xtail
    cs_ref[0, :, BRANCH:SSD_CONV_DIM] = bctail
    xs = _silu(xc)
    bcs = _silu(bcc)

    tri = tri_ref[...]
    expand = exp_ref[...]
    dt = _softplus(sm_ref[...] + dtb_ref[...])
    cum = _sel_left(tri, dt * a_ref[...])
    cum_t = cum.T
    dt_x = _sel_right(dt, expand)
    cum_x = _sel_right(cum, expand)
    cum_last = cum_x[n_rows - 1:n_rows, :]
    dec_in = jnp.exp(cum_x)
    xdt = xs * dt_x
    xw = (xdt * jnp.exp(cum_last - cum_x)).astype(BF16)
    xdt16 = xdt.astype(BF16)

    t_idx = lax.broadcasted_iota(jnp.int32, (n_rows, n_rows), 0)
    s_idx = lax.broadcasted_iota(jnp.int32, (n_rows, n_rows), 1)
    causal = s_idx <= t_idx
    lane = lax.broadcasted_iota(jnp.int32, (n_rows, 2 * SSD_P), 1)

    y_parts = []
    for g in range(SSD_G):
        b_g = bcs[:, g * SSD_N:(g + 1) * SSD_N].astype(BF16)
        c_g = bcs[:, (SSD_G + g) * SSD_N:(SSD_G + g + 1) * SSD_N].astype(BF16)
        cb = _dot_nt(c_g, b_g)
        st_g = st_scr[:, g * gw:(g + 1) * gw]
        y_inter = _dot(c_g, st_g.astype(BF16)) * dec_in[:, g * gw:(g + 1) * gw]
        heads_per_g = SSD_HEADS // SSD_G
        for pair in range(heads_per_g // 2):
            h0 = g * heads_per_g + 2 * pair
            xp = xdt16[:, h0 * SSD_P:(h0 + 2) * SSD_P]
            outs = []
            for h in (h0, h0 + 1):
                col = DT_LANE + h
                seg = cum[:, col:col + 1] - cum_t[col:col + 1, :]
                lmat = jnp.where(causal, jnp.exp(jnp.where(causal, seg, 0.0)), 0.0)
                outs.append(_dot((cb * lmat).astype(BF16), xp))
            y_parts.append(jnp.where(lane < SSD_P, outs[0], outs[1]))
        y_parts.append(y_inter)
        st_scr[:, g * gw:(g + 1) * gw] = (
            st_g * jnp.exp(cum_last[:, g * gw:(g + 1) * gw])
            + _dot(bcs[:, g * SSD_N:(g + 1) * SSD_N].T.astype(BF16), xw[:, g * gw:(g + 1) * gw]))
    npair = SSD_HEADS // SSD_G // 2
    y = jnp.concatenate(
        [jnp.concatenate(y_parts[g * (npair + 1):g * (npair + 1) + npair], axis=1)
         + y_parts[g * (npair + 1) + npair] for g in range(SSD_G)], axis=1)
    y = (y + xs * d_ref[...]) * _silu(z_ref[...])
    y_ref[...] = _group_norm(y, nw_ref[...], gw).astype(BF16)

    @pl.when(c == pl.num_programs(1) - 1)
    def _():
        s_ref[0] = st_scr[...].T.reshape(SSD_HEADS, SSD_P, SSD_N)


def _ssd_expand_const():
    e = np.zeros((128, BRANCH), np.float32)
    for h in range(SSD_HEADS):
        e[DT_LANE + h, h * SSD_P:(h + 1) * SSD_P] = 1.0
    return jnp.asarray(e, dtype=BF16)


def _ssd_prompt(proj, cwx, cbx, cwb, cbb, dtb, a_pad, d_x, nw, bsz, seq):
    nc = seq // CHUNK
    blk = lambda w, j: pl.BlockSpec((CHUNK, w), lambda b, c, j=j: (b * nc + c, j))
    const = lambda shape: pl.BlockSpec(shape, lambda b, c: (0,) * len(shape))
    return pl.pallas_call(
        _ssd_prompt_kernel,
        grid=(bsz, nc),
        in_specs=[blk(BRANCH, BLK_SS_Z), blk(BRANCH, BLK_SS_X), blk(SSD_BC, BLK_SS_BC),
                  blk(128, BLK_SMALL),
                  const((CONV_W, BRANCH)), const((1, BRANCH)), const((CONV_W, SSD_BC)),
                  const((1, SSD_BC)), const((1, 128)), const((1, 128)), const((1, BRANCH)),
                  const((1, BRANCH)), const((CHUNK, CHUNK)), const((128, BRANCH))],
        out_specs=[pl.BlockSpec((CHUNK, BRANCH), lambda b, c: (b * nc + c, 0)),
                   pl.BlockSpec((1, SSD_HEADS, SSD_P, SSD_N), lambda b, c: (b, 0, 0, 0)),
                   pl.BlockSpec((1, CONV_W - 1, SSD_CONV_DIM), lambda b, c: (b, 0, 0))],
        out_shape=[jax.ShapeDtypeStruct((bsz * seq, BRANCH), BF16),
                   jax.ShapeDtypeStruct((bsz, SSD_HEADS, SSD_P, SSD_N), F32),
                   jax.ShapeDtypeStruct((bsz, CONV_W - 1, SSD_CONV_DIM), F32)],
        scratch_shapes=[pltpu.VMEM((SSD_N, BRANCH), F32),
                        pltpu.VMEM((8 + CHUNK, BRANCH), F32),
                        pltpu.VMEM((8 + CHUNK, SSD_BC), F32)],
        compiler_params=_cparams("parallel", "arbitrary"),
        name="ssd_prompt",
    )(proj, proj, proj, proj, cwx, cbx, cwb, cbb, dtb, a_pad, d_x, nw,
      _tri_const(CHUNK), _ssd_expand_const())


def _step_conv(x, cs_ref, w_ref, b_ref, ncs_ref):
    y = b_ref[...] + x * w_ref[CONV_W - 1:CONV_W, :]
    for j in range(CONV_W - 1):
        y = y + cs_ref[j] * w_ref[j:j + 1, :]
    for j in range(CONV_W - 2):
        ncs_ref[j] = cs_ref[j + 1]
    ncs_ref[CONV_W - 2] = x
    return y


def _sample_pre_kernel(p_ref, lb_ref, rcs_ref, rh_ref, rcw_ref, rcb_ref, wr_ref, br_ref, wi_ref,
                       bi_ref, lam_ref, wup_ref, bup_ref, scs_ref, scw_ref, scb_ref, dtb_ref,
                       a_ref, exp_ref,
                       hq_ref, hk_ref, hd_ref, gq_ref, gd_ref, yrg_ref, nrh_ref, nrcs_ref,
                       sx_ref, sbc_ref, sdx_ref, sda_ref, nscs_ref):
    col = lambda blk, w: slice(blk * w, (blk + 1) * w)
    f = p_ref[:, col(BLK_HG_F, BRANCH)]
    lb = lb_ref[...]
    hq_ref[...] = _silu(p_ref[:, col(BLK_HG_Q, BRANCH)])
    hk_ref[...] = (1.0 - lb) * _sigmoid(-f)
    hd_ref[...] = jnp.maximum(lb + (1.0 - lb) * _sigmoid(f), TINY)
    sm = p_ref[:, col(BLK_SMALL, 128)]
    up = _dot(sm.astype(BF16), wup_ref[...]) + bup_ref[...]
    gq_ref[...] = p_ref[:, col(BLK_GL_Q, GLA_KDIM)] * (GLA_DK ** -0.5)
    gd_ref[...] = jnp.exp(-_softplus(-up) * (1.0 / GLA_TAU))
    xc = _step_conv(p_ref[:, col(BLK_RG_X, BRANCH)], rcs_ref, rcw_ref, rcb_ref, nrcs_ref)
    a_parts, u_parts = _rglru_gates(xc, wr_ref, br_ref, wi_ref, bi_ref, lam_ref)
    h = jnp.concatenate(a_parts, axis=1) * rh_ref[...] + jnp.concatenate(u_parts, axis=1)
    nrh_ref[...] = h
    yrg_ref[...] = (h * _silu(p_ref[:, col(BLK_RG_G, BRANCH)])).astype(BF16)
    xbc = jnp.concatenate([p_ref[:, col(BLK_SS_X, BRANCH)], p_ref[:, col(BLK_SS_BC, SSD_BC)]], axis=1)
    xbc = _silu(_step_conv(xbc, scs_ref, scw_ref, scb_ref, nscs_ref))
    xs = xbc[:, 0:BRANCH]
    sx_ref[...] = xs
    sbc_ref[...] = xbc[:, BRANCH:SSD_CONV_DIM]
    dt = _softplus(sm + dtb_ref[...])
    expand = exp_ref[...]
    sdx_ref[...] = xs * _sel_right(dt, expand)
    sda_ref[...] = jnp.exp(_sel_right(dt * a_ref[...], expand))


def _sample_pre(proj, lb, rcs, rh, rcw, rcb, wr, br, wi, bi, lam, wup, bup, scs, scw, scb,
                dtb, a_pad):
    nb = proj.shape[0]
    sd = lambda *shape, dt=F32: jax.ShapeDtypeStruct(shape, dt)
    return pl.pallas_call(
        _sample_pre_kernel,
        out_shape=[sd(nb, BRANCH), sd(nb, BRANCH), sd(nb, BRANCH),
                   sd(nb, GLA_KDIM), sd(nb, GLA_KDIM),
                   sd(nb, BRANCH, dt=BF16), sd(nb, BRANCH), sd(CONV_W - 1, nb, BRANCH),
                   sd(nb, BRANCH), sd(nb, SSD_BC), sd(nb, BRANCH), sd(nb, BRANCH),
                   sd(CONV_W - 1, nb, SSD_CONV_DIM)],
        compiler_params=pltpu.CompilerParams(vmem_limit_bytes=VMEM_LIMIT),
        name="sample_pre",
    )(proj, lb, rcs, rh, rcw, rcb, wr, br, wi, bi, lam, wup, bup, scs, scw, scb, dtb, a_pad,
      _ssd_expand_const())


def _pad_t(x):
    r = x.shape[0]
    return jnp.concatenate([x, jnp.zeros((128 - r, 128), F32)], axis=0).T


def _gla_state_kernel(d_ref, k_ref, v_ref, q_ref, s_ref, so_ref, o_ref, *, heads):
    def body(b, carry):
        dt_ = _pad_t(d_ref[b])
        kt_ = _pad_t(k_ref[b])
        vr = v_ref[b]
        qr = q_ref[b]
        for h in range(heads):
            s_new = dt_[:, h:h + 1] * s_ref[b, h] + kt_[:, h:h + 1] * vr[h:h + 1, :]
            so_ref[b, h] = s_new
            q8 = jnp.broadcast_to(qr[h:h + 1, :], (8, qr.shape[1])).astype(BF16)
            o_ref[b, h:h + 1, :] = _dot(q8, s_new.astype(BF16))[0:1, :]
        return carry
    lax.fori_loop(0, s_ref.shape[0], body, 0)


def _gla_state(d, k, v, q, s, *, bb):
    nb, heads, dk, dv = s.shape
    vec = lambda w: pl.BlockSpec((bb, heads, w), lambda i: (i, 0, 0))
    st = pl.BlockSpec((bb, heads, dk, dv), lambda i: (i, 0, 0, 0))
    return pl.pallas_call(
        functools.partial(_gla_state_kernel, heads=heads),
        grid=(nb // bb,),
        in_specs=[vec(dk), vec(dk), vec(dv), vec(dk), st],
        out_specs=[st, vec(dv)],
        out_shape=[jax.ShapeDtypeStruct(s.shape, F32),
                   jax.ShapeDtypeStruct((nb, heads, dv), F32)],
        compiler_params=_cparams("parallel"),
        name="gla_state",
    )(d, k, v, q, s)


def _ssd_state_kernel(da_ref, dx_ref, b_ref, c_ref, s_ref, so_ref, y_ref):
    hpg = SSD_HEADS // SSD_G
    def body(b, carry):
        at_ = _pad_t(da_ref[b])
        xt_ = _pad_t(dx_ref[b])
        br = b_ref[b]
        cr = c_ref[b]
        for h in range(SSD_HEADS):
            g = h // hpg
            rows = slice((h % 2) * SSD_P, (h % 2 + 1) * SSD_P)
            j = h // 2
            s_new = at_[rows, j:j + 1] * s_ref[b, h] + xt_[rows, j:j + 1] * br[g:g + 1, :]
            so_ref[b, h] = s_new
            c8 = jnp.broadcast_to(cr[g:g + 1, :], (8, SSD_N)).astype(BF16)
            y_ref[b, h:h + 1, :] = _dot_nt(c8, s_new.astype(BF16))[0:1, :]
        return carry
    lax.fori_loop(0, s_ref.shape[0], body, 0)


def _ssd_state(da, dx, bv, cv, s, *, bb):
    nb = s.shape[0]
    vec = lambda r, w: pl.BlockSpec((bb, r, w), lambda i: (i, 0, 0))
    st = pl.BlockSpec((bb, SSD_HEADS, SSD_P, SSD_N), lambda i: (i, 0, 0, 0))
    return pl.pallas_call(
        _ssd_state_kernel,
        grid=(nb // bb,),
        in_specs=[vec(8, 128), vec(8, 128), vec(SSD_G, SSD_N), vec(SSD_G, SSD_N), st],
        out_specs=[st, vec(SSD_HEADS, SSD_P)],
        out_shape=[jax.ShapeDtypeStruct(s.shape, F32),
                   jax.ShapeDtypeStruct((nb, SSD_HEADS, SSD_P), F32)],
        compiler_params=_cparams("parallel"),
        name="ssd_state",
    )(da, dx, bv, cv, s)


def _sample_post_kernel(p_ref, ohg_ref, ogl_ref, yss_ref, sx_ref, d_ref, hnw_ref, gnw_ref,
                        snw_ref, yhg_ref, ygl_ref, yso_ref):
    col = lambda blk: slice(blk * BRANCH, (blk + 1) * BRANCH)
    yhg_ref[...] = (_group_norm(ohg_ref[...], hnw_ref[...], HG_DV)
                    * _silu(p_ref[:, col(BLK_HG_G)])).astype(BF16)
    ygl_ref[...] = (_group_norm(ogl_ref[...], gnw_ref[...], GLA_DV)
                    * _silu(p_ref[:, col(BLK_GL_G)])).astype(BF16)
    y = (yss_ref[...] + sx_ref[...] * d_ref[...]) * _silu(p_ref[:, col(BLK_SS_Z)])
    yso_ref[...] = _group_norm(y, snw_ref[...], BRANCH // SSD_G).astype(BF16)


def _sample_post(proj, ohg, ogl, yss, sx, d_x, hnw, gnw, snw):
    nb = proj.shape[0]
    out = jax.ShapeDtypeStruct((nb, BRANCH), BF16)
    return pl.pallas_call(
        _sample_post_kernel, out_shape=[out, out, out],
        compiler_params=pltpu.CompilerParams(vmem_limit_bytes=VMEM_LIMIT),
        name="sample_post",
    )(proj, ohg, ogl, yss, sx, d_x, hnw, gnw, snw)


def _prep_w_in(w_in):
    pad = jnp.zeros(w_in.shape[:2] + (N_PROJ - N_IN,), w_in.dtype)
    w = jnp.concatenate([w_in[..., :ORIG_GLA_A], w_in[..., ORIG_SSD_Z:ORIG_SSD_DT],
                         w_in[..., ORIG_GLA_A:ORIG_SSD_Z], w_in[..., ORIG_SSD_DT:N_IN], pad], axis=-1)
    return w.astype(BF16)


def _pad_lanes(v, start, width=128):
    out = jnp.zeros((v.shape[0], 1, width), F32)
    return out.at[:, 0, start:start + v.shape[1]].set(v.astype(F32))


def kernel(x_prompt, x_sample, state_hgrn, state_rglru, state_rglru_conv, state_gla, state_ssd, state_ssd_conv, rms_in, w_in, hgrn_lower_bounds, hgrn_norm, rglru_conv_w, rglru_conv_b, rglru_w_r, rglru_b_r, rglru_w_i, rglru_b_i, rglru_lambda, gla_w_up, gla_b_up, gla_norm, ssd_conv_w, ssd_conv_b, ssd_dt_bias, ssd_a_log, ssd_d, ssd_norm, w_out, rms_final):
    bsz, seq, _ = x_prompt.shape
    nb = x_sample.shape[0]
    row = lambda v: v.reshape(DEPTH, 1, -1).astype(F32)

    lb_all = _lower_bounds(hgrn_lower_bounds.astype(F32)).reshape(DEPTH, 1, BRANCH)
    w_in16 = _prep_w_in(w_in)
    w_out16 = w_out.astype(BF16)
    wr16 = rglru_w_r.astype(BF16)
    wi16 = rglru_w_i.astype(BF16)
    wup16 = jnp.concatenate(
        [gla_w_up, jnp.zeros((DEPTH, 128 - GLA_RANK, GLA_KDIM), gla_w_up.dtype)], axis=1).astype(BF16)
    dtb = _pad_lanes(ssd_dt_bias, DT_LANE)
    a_pad = _pad_lanes(-jnp.exp(ssd_a_log.astype(F32)), DT_LANE)
    d_x = jnp.repeat(ssd_d.astype(F32), SSD_P, axis=-1).reshape(DEPTH, 1, BRANCH)
    rms_in_r, hnw, gnw, snw = row(rms_in), row(hgrn_norm), row(gla_norm), row(ssd_norm)
    rcb, br, bi, lam, bup = (row(rglru_conv_b), row(rglru_b_r), row(rglru_b_i),
                             row(rglru_lambda), row(gla_b_up))
    scb = row(ssd_conv_b)

    xp = x_prompt.reshape(bsz * seq, D_MODEL)
    xs = x_sample.reshape(nb, D_MODEL)
    tm_p = 512 if (bsz * seq) % 512 == 0 else CHUNK
    bb = 4 if nb % 4 == 0 else 1

    outs_p = [[] for _ in range(6)]
    outs_s = [[] for _ in range(6)]
    for l in range(DEPTH):
        scw = ssd_conv_w[l].astype(F32)
        proj = _inproj(xp, rms_in_r[l], w_in16[l], tm=tm_p, tn=512)
        y_hg, s_hg = _hgrn_prompt(proj, lb_all[l], hnw[l], bsz, seq)
        y_rg, s_rg, s_rgc = _rglru_prompt(proj, rglru_conv_w[l].astype(F32), rcb[l], wr16[l], br[l],
                                          wi16[l], bi[l], lam[l], bsz, seq)
        y_gl, s_gl = _gla_prompt(proj, wup16[l], bup[l], gnw[l], bsz, seq)
        y_ss, s_ss, s_ssc = _ssd_prompt(proj, scw[:, :BRANCH], scb[l][:, :BRANCH], scw[:, BRANCH:],
                                        scb[l][:, BRANCH:], dtb[l], a_pad[l], d_x[l], snw[l], bsz, seq)
        xp = _outproj((y_hg, y_rg, y_gl, y_ss), w_out16[l], xp, tm=tm_p, tn=512)
        for lst, s in zip(outs_p, (s_hg, s_rg.reshape(bsz, BRANCH), s_rgc, s_gl, s_ss, s_ssc)):
            lst.append(s)

        proj_s = _inproj(xs, rms_in_r[l], w_in16[l], tm=nb, tn=512)
        (hq, hk, hd, gq, gd, yrg_s, nrh, nrcs, sx, sbc, sdx, sda, nscs) = _sample_pre(
            proj_s, lb_all[l], jnp.swapaxes(state_rglru_conv[l], 0, 1), state_rglru[l],
            rglru_conv_w[l].astype(F32), rcb[l], wr16[l], br[l], wi16[l], bi[l], lam[l],
            wup16[l], bup[l], jnp.swapaxes(state_ssd_conv[l], 0, 1), scw, scb[l], dtb[l], a_pad[l])
        hv = proj_s[:, BLK_HG_I * BRANCH:(BLK_HG_I + 1) * BRANCH]
        gk = proj_s[:, BLK_GL_K * GLA_KDIM:(BLK_GL_K + 1) * GLA_KDIM]
        gv = proj_s[:, BLK_GL_V * BRANCH:(BLK_GL_V + 1) * BRANCH]
        hsh = lambda a: a.reshape(nb, HG_HEADS, -1)
        gsh = lambda a: a.reshape(nb, GLA_HEADS, -1)
        ns_hg, o_hg = _gla_state(hsh(hd), hsh(hk), hsh(hv), hsh(hq), state_hgrn[l], bb=bb)
        ns_gl, o_gl = _gla_state(gsh(gd), gsh(gk), gsh(gv), gsh(gq), state_gla[l], bb=bb)
        ns_ss, y_ssr = _ssd_state(sda.reshape(nb, 8, 128), sdx.reshape(nb, 8, 128),
                                  sbc[:, :SSD_G * SSD_N].reshape(nb, SSD_G, SSD_N),
                                  sbc[:, SSD_G * SSD_N:].reshape(nb, SSD_G, SSD_N),
                                  state_ssd[l], bb=bb)
        yhg_s, ygl_s, yss_s = _sample_post(proj_s, o_hg.reshape(nb, BRANCH), o_gl.reshape(nb, BRANCH),
                                           y_ssr.reshape(nb, BRANCH), sx, d_x[l], hnw[l], gnw[l], snw[l])
        xs = _outproj((yhg_s, yrg_s, ygl_s, yss_s), w_out16[l], xs, tm=nb, tn=512)
        for lst, s in zip(outs_s, (ns_hg, nrh, jnp.swapaxes(nrcs, 0, 1), ns_gl, ns_ss,
                                   jnp.swapaxes(nscs, 0, 1))):
            lst.append(s)

    rf = rms_final.reshape(1, D_MODEL).astype(F32)
    y_prompt = _rmsnorm(xp, rf, tm=tm_p).reshape(bsz, seq, D_MODEL)
    y_sample = _rmsnorm(xs, rf, tm=nb).reshape(nb, 1, D_MODEL)
    return (y_prompt, y_sample) + tuple(jnp.stack(l) for l in outs_p) + tuple(jnp.stack(l) for l in outs_s)
```

```python
import functools
import math

import numpy as np
import jax
import jax.numpy as jnp
from jax import lax
from jax.experimental import pallas as pl
from jax.experimental.pallas import tpu as pltpu

F32 = jnp.float32
BF16 = jnp.bfloat16

D_MODEL = 2048
DEPTH = 4
BRANCH = 1024
D_MIX = 4 * BRANCH
CONV_W = 4
EPS = 1e-6
TINY = 1e-30

HG_HEADS, HG_DK, HG_DV = 8, 128, 128
RG_BLOCKS, RG_BW, RG_C = 8, 128, 8.0
GLA_HEADS, GLA_DK, GLA_DV, GLA_RANK, GLA_TAU = 4, 128, 256, 16, 16.0
GLA_KDIM = GLA_HEADS * GLA_DK
SSD_HEADS, SSD_P, SSD_G, SSD_N = 16, 64, 2, 128
SSD_BC = 2 * SSD_G * SSD_N
SSD_CONV_DIM = BRANCH + SSD_BC

ORIG_GLA_A = 9216
ORIG_SSD_Z = 9232
ORIG_SSD_DT = 11792
N_IN = 11808
N_PROJ = 12288
COL_SMALL = 11776
DT_LANE = 16
BLK_HG_Q, BLK_HG_F, BLK_HG_I, BLK_HG_G = 0, 1, 2, 3
BLK_RG_X, BLK_RG_G = 4, 5
BLK_GL_Q, BLK_GL_K = 12, 13
BLK_GL_V, BLK_GL_G = 7, 8
BLK_SS_Z, BLK_SS_X = 9, 10
BLK_SS_BC = 22
BLK_SMALL = COL_SMALL // 128

CHUNK = 128
SUB = 16
VMEM_LIMIT = 52 * 1024 * 1024


def _cparams(*sem):
    return pltpu.CompilerParams(dimension_semantics=sem, vmem_limit_bytes=VMEM_LIMIT)


def _sigmoid(x):
    return 0.5 * jnp.tanh(0.5 * x) + 0.5


def _silu(x):
    return x * _sigmoid(x)


def _softplus(x):
    return jnp.maximum(x, 0.0) + jnp.log(1.0 + jnp.exp(-jnp.abs(x)))


def _dot(a, b):
    return jnp.dot(a, b, preferred_element_type=F32)


def _dot_nt(a, b):
    return lax.dot_general(a, b, (((1,), (1,)), ((), ())), preferred_element_type=F32)


def _split3(a):
    a0 = a.astype(BF16)
    r1 = a - a0.astype(F32)
    a1 = r1.astype(BF16)
    a2 = (r1 - a1.astype(F32)).astype(BF16)
    return a0, a1, a2


def _sel_left(sel, x):
    x0, x1, x2 = _split3(x)
    return _dot(sel, x0) + _dot(sel, x1) + _dot(sel, x2)


def _sel_right(x, sel):
    x0, x1, x2 = _split3(x)
    return _dot(x0, sel) + _dot(x1, sel) + _dot(x2, sel)


def _group_norm(y, w, width):
    parts = []
    for g in range(y.shape[1] // width):
        yg = y[:, g * width:(g + 1) * width]
        ms = jnp.mean(yg * yg, axis=-1, keepdims=True)
        parts.append(yg * lax.rsqrt(ms + EPS))
    out = parts[0] if len(parts) == 1 else jnp.concatenate(parts, axis=1)
    return out * w


def _tri_const(c):
    return jnp.asarray(np.tril(np.ones((c, c), np.float32)), dtype=BF16)


def _level_const(c):
    t = np.arange(c)[:, None]
    s = np.arange(c)[None, :]
    lvl = np.zeros((c, c), np.int32)
    lvl[(t // SUB == s // SUB) & (s <= t)] = 1
    h, code = SUB, 2
    while h < c:
        m = (t // (2 * h) == s // (2 * h)) & (t % (2 * h) >= h) & (s % (2 * h) < h)
        lvl[m] = code
        h *= 2
        code += 1
    return jnp.asarray(lvl)


def _lb_kernel(p_ref, o_ref):
    x = p_ref[...]
    m = jnp.max(x, axis=0, keepdims=True)
    e = jnp.exp(x - m)
    p = e / jnp.sum(e, axis=0, keepdims=True)
    acc = jnp.zeros_like(p[0:1])
    rows = [acc]
    for l in range(1, DEPTH):
        acc = acc + p[l:l + 1]
        rows.append(acc)
    o_ref[...] = jnp.concatenate(rows, axis=0)


def _lower_bounds(param):
    return pl.pallas_call(
        _lb_kernel, out_shape=jax.ShapeDtypeStruct(param.shape, F32), name="hgrn_lb")(param)


N_MAIN = ORIG_GLA_A
N_TAIL = N_PROJ - N_MAIN


def _inproj_kernel(h_ref, w_ref, wt_ref, o_ref, wb_scr, *, n_main, rb):
    j = pl.program_id(0)

    @pl.when(pl.program_id(1) == 0)
    def _():
        @pl.when(j < n_main)
        def _():
            def body(i, carry):
                r = pl.multiple_of(i * rb, rb)
                wb_scr[pl.ds(r, rb), :] = w_ref[pl.ds(r, rb), :].astype(BF16)
                return carry
            lax.fori_loop(0, wb_scr.shape[0] // rb, body, 0)

        @pl.when(j >= n_main)
        def _():
            wb_scr[...] = wt_ref[...]

    o_ref[...] = _dot(h_ref[...], wb_scr[...])


def _inproj(h, w_in, w_tail, layer, *, tm, tn):
    m, d = h.shape
    n_main = N_MAIN // tn
    return pl.pallas_call(
        functools.partial(_inproj_kernel, n_main=n_main, rb=256),
        grid=(N_PROJ // tn, m // tm),
        in_specs=[pl.BlockSpec((tm, d), lambda j, i: (i, 0)),
                  pl.BlockSpec((None, d, tn), lambda j, i: (layer, 0, jnp.minimum(j, n_main - 1))),
                  pl.BlockSpec((None, d, tn), lambda j, i: (layer, 0, jnp.maximum(j - n_main, 0)))],
        out_specs=pl.BlockSpec((tm, tn), lambda j, i: (i, j)),
        out_shape=jax.ShapeDtypeStruct((m, N_PROJ), F32),
        scratch_shapes=[pltpu.VMEM((d, tn), BF16)],
        compiler_params=_cparams("arbitrary", "arbitrary"),
        name="inproj",
    )(h, w_in, w_tail)


def _outproj_kernel(y0_ref, y1_ref, y2_ref, y3_ref, w_ref, x_ref, nw_ref, xo_ref, ho_ref, *, rb):
    acc = x_ref[...]
    for g, y_ref in enumerate((y0_ref, y1_ref, y2_ref, y3_ref)):
        acc = acc + _dot(y_ref[...], w_ref[g * BRANCH:(g + 1) * BRANCH, :])
    xo_ref[...] = acc

    def body(i, carry):
        r = pl.multiple_of(i * rb, rb)
        x = xo_ref[pl.ds(r, rb), :]
        ms = jnp.mean(x * x, axis=-1, keepdims=True)
        ho_ref[pl.ds(r, rb), :] = (x * lax.rsqrt(ms + EPS) * nw_ref[...]).astype(ho_ref.dtype)
        return carry
    lax.fori_loop(0, xo_ref.shape[0] // rb, body, 0)


def _outproj(ys, w, x, norm_w, norm_dtype, *, tm):
    m, d = x.shape
    yspec = pl.BlockSpec((tm, BRANCH), lambda i: (i, 0))
    xspec = pl.BlockSpec((tm, d), lambda i: (i, 0))
    return pl.pallas_call(
        functools.partial(_outproj_kernel, rb=min(tm, 64)),
        grid=(m // tm,),
        in_specs=[yspec, yspec, yspec, yspec,
                  pl.BlockSpec((D_MIX, d), lambda i: (0, 0), pipeline_mode=pl.Buffered(1)),
                  xspec, pl.BlockSpec((1, d), lambda i: (0, 0))],
        out_specs=[xspec, xspec],
        out_shape=[jax.ShapeDtypeStruct((m, d), F32), jax.ShapeDtypeStruct((m, d), norm_dtype)],
        compiler_params=_cparams("parallel"),
        name="outproj",
    )(*ys, w, x, norm_w)


def _rmsnorm_kernel(x_ref, w_ref, o_ref):
    x = x_ref[...]
    ms = jnp.mean(x * x, axis=-1, keepdims=True)
    o_ref[...] = (x * lax.rsqrt(ms + EPS) * w_ref[...]).astype(o_ref.dtype)


def _rmsnorm(x, w, out_dtype, *, tm):
    m, d = x.shape
    return pl.pallas_call(
        _rmsnorm_kernel,
        grid=(m // tm,),
        in_specs=[pl.BlockSpec((tm, d), lambda i: (i, 0)),
                  pl.BlockSpec((1, d), lambda i: (0, 0))],
        out_specs=pl.BlockSpec((tm, d), lambda i: (i, 0)),
        out_shape=jax.ShapeDtypeStruct((m, d), out_dtype),
        compiler_params=_cparams("parallel"),
        name="rmsnorm",
    )(x, w)


def _gla_chunk(q, k, v, logf, st, tri, lvl):
    c = q.shape[0]
    b = _sel_left(tri, logf)

    def ref_rows(rows, span):
        return jnp.concatenate(
            [jnp.broadcast_to(b[r:r + 1, :], (span, b.shape[1])) for r in rows], axis=0)

    ed = b - ref_rows([SUB * m + SUB // 2 for m in range(c // SUB)], SUB)
    s = _dot_nt((q * jnp.exp(ed)).astype(BF16), (k * jnp.exp(-ed)).astype(BF16))
    scores = jnp.where(lvl == 1, s, 0.0)
    half, code = SUB, 2
    while half < c:
        ref = ref_rows([2 * half * m + half - 1 for m in range(c // (2 * half))], 2 * half)
        e = jnp.exp(-jnp.abs(b - ref))
        s = _dot_nt((q * e).astype(BF16), (k * e).astype(BF16))
        scores = jnp.where(lvl == code, s, scores)
        half *= 2
        code += 1
    b_last = b[c - 1:c, :]
    q_in = (q * jnp.exp(b)).astype(BF16)
    k_end = (k * jnp.exp(b_last - b)).astype(BF16)
    o = _dot(scores.astype(BF16), v.astype(BF16)) + _dot_nt(q_in, st.astype(BF16))
    st_new = st * jnp.exp(b_last) + _dot(v.T.astype(BF16), k_end)
    return o, st_new


def _hgrn_prompt_kernel(q_ref, f_ref, i_ref, g_ref, lb_ref, nw_ref, tri_ref, lvl_ref,
                        y_ref, s_ref, st_scr):
    c = pl.program_id(1)

    @pl.when(c == 0)
    def _():
        st_scr[...] = jnp.zeros_like(st_scr)

    tri = tri_ref[...]
    lvl = lvl_ref[...]
    for h in range(HG_HEADS):
        sl = slice(h * HG_DK, (h + 1) * HG_DK)
        f = f_ref[:, sl]
        lb = lb_ref[:, sl]
        sg = _sigmoid(f)
        fg = lb + (1.0 - lb) * sg
        logf = jnp.log(jnp.maximum(fg, TINY))
        k = (1.0 - lb) * (1.0 - sg)
        q = _silu(q_ref[:, sl])
        o, st_new = _gla_chunk(q, k, i_ref[:, sl], logf, st_scr[h], tri, lvl)
        st_scr[h] = st_new
        y = _group_norm(o, nw_ref[:, sl], HG_DV) * _silu(g_ref[:, sl])
        y_ref[:, sl] = y.astype(BF16)

    @pl.when(c == pl.num_programs(1) - 1)
    def _():
        for h in range(HG_HEADS):
            s_ref[0, h] = st_scr[h].T


def _hgrn_prompt(proj, lb, nw, bsz, seq):
    nc = seq // CHUNK
    blk = lambda j: pl.BlockSpec((CHUNK, BRANCH), lambda b, c, j=j: (b * nc + c, j))
    row = pl.BlockSpec((1, BRANCH), lambda b, c: (0, 0))
    cc = pl.BlockSpec((CHUNK, CHUNK), lambda b, c: (0, 0))
    return pl.pallas_call(
        _hgrn_prompt_kernel,
        grid=(bsz, nc),
        in_specs=[blk(BLK_HG_Q), blk(BLK_HG_F), blk(BLK_HG_I), blk(BLK_HG_G), row, row, cc, cc],
        out_specs=[pl.BlockSpec((CHUNK, BRANCH), lambda b, c: (b * nc + c, 0)),
                   pl.BlockSpec((1, HG_HEADS, HG_DK, HG_DV), lambda b, c: (b, 0, 0, 0))],
        out_shape=[jax.ShapeDtypeStruct((bsz * seq, BRANCH), BF16),
                   jax.ShapeDtypeStruct((bsz, HG_HEADS, HG_DK, HG_DV), F32)],
        scratch_shapes=[pltpu.VMEM((HG_HEADS, HG_DV, HG_DK), F32)],
        compiler_params=_cparams("parallel", "arbitrary"),
        name="hgrn_prompt",
    )(proj, proj, proj, proj, lb, nw, _tri_const(CHUNK), _level_const(CHUNK))


def _gla_prompt_kernel(q_ref, k_ref, v_ref, g_ref, sm_ref, wup_ref, bup_ref, nw_ref,
                       tri_ref, lvl_ref, y_ref, s_ref, st_scr):
    c = pl.program_id(1)

    @pl.when(c == 0)
    def _():
        st_scr[...] = jnp.zeros_like(st_scr)

    tri = tri_ref[...]
    lvl = lvl_ref[...]
    up = _dot(sm_ref[...].astype(BF16), wup_ref[...]) + bup_ref[...]
    log_a = -_softplus(-up) * (1.0 / GLA_TAU)
    for h in range(GLA_HEADS):
        ks = slice(h * GLA_DK, (h + 1) * GLA_DK)
        vs = slice(h * GLA_DV, (h + 1) * GLA_DV)
        q = q_ref[:, ks] * (GLA_DK ** -0.5)
        o, st_new = _gla_chunk(q, k_ref[:, ks], v_ref[:, vs], log_a[:, ks], st_scr[h], tri, lvl)
        st_scr[h] = st_new
        y = _group_norm(o, nw_ref[:, vs], GLA_DV) * _silu(g_ref[:, vs])
        y_ref[:, vs] = y.astype(BF16)

    @pl.when(c == pl.num_programs(1) - 1)
    def _():
        for h in range(GLA_HEADS):
            s_ref[0, h] = st_scr[h].T


def _gla_prompt(proj, wup, bup, nw, bsz, seq):
    nc = seq // CHUNK
    blk = lambda w, j: pl.BlockSpec((CHUNK, w), lambda b, c, j=j: (b * nc + c, j))
    const = lambda shape: pl.BlockSpec(shape, lambda b, c: (0,) * len(shape))
    return pl.pallas_call(
        _gla_prompt_kernel,
        grid=(bsz, nc),
        in_specs=[blk(GLA_KDIM, BLK_GL_Q), blk(GLA_KDIM, BLK_GL_K), blk(BRANCH, BLK_GL_V),
                  blk(BRANCH, BLK_GL_G), blk(128, BLK_SMALL),
                  const((128, GLA_KDIM)), const((1, GLA_KDIM)), const((1, BRANCH)),
                  const((CHUNK, CHUNK)), const((CHUNK, CHUNK))],
        out_specs=[pl.BlockSpec((CHUNK, BRANCH), lambda b, c: (b * nc + c, 0)),
                   pl.BlockSpec((1, GLA_HEADS, GLA_DK, GLA_DV), lambda b, c: (b, 0, 0, 0))],
        out_shape=[jax.ShapeDtypeStruct((bsz * seq, BRANCH), BF16),
                   jax.ShapeDtypeStruct((bsz, GLA_HEADS, GLA_DK, GLA_DV), F32)],
        scratch_shapes=[pltpu.VMEM((GLA_HEADS, GLA_DV, GLA_DK), F32)],
        compiler_params=_cparams("parallel", "arbitrary"),
        name="gla_prompt",
    )(proj, proj, proj, proj, proj, wup, bup, nw, _tri_const(CHUNK), _level_const(CHUNK))


def _chunk_conv(x_ref, buf, w_ref, b_ref, first):
    c = x_ref.shape[0]

    @pl.when(first)
    def _():
        buf[0:8, :] = jnp.zeros((8, buf.shape[1]), F32)

    buf[8:8 + c, :] = x_ref[...]
    y = b_ref[...] + buf[5:5 + c, :] * w_ref[0:1, :]
    for j in range(1, CONV_W):
        y = y + buf[5 + j:5 + j + c, :] * w_ref[j:j + 1, :]
    tail = buf[c:c + 8, :]
    buf[0:8, :] = tail
    return y, tail[5:8, :]


def _rglru_gates(xc, wr_ref, br_ref, wi_ref, bi_ref, lam_ref):
    a_parts, u_parts = [], []
    for n in range(RG_BLOCKS):
        sl = slice(n * RG_BW, (n + 1) * RG_BW)
        xb = xc[:, sl]
        xb16 = xb.astype(BF16)
        r = _sigmoid(_dot(xb16, wr_ref[n]) + br_ref[:, sl])
        i = _sigmoid(_dot(xb16, wi_ref[n]) + bi_ref[:, sl])
        log_a = -RG_C * r * _softplus(-lam_ref[:, sl])
        a = jnp.exp(log_a)
        one_m_a2 = -jnp.tanh(log_a) * (a * a + 1.0)
        a_parts.append(a)
        u_parts.append(jnp.sqrt(jnp.maximum(one_m_a2, 0.0)) * (i * xb))
    return a_parts, u_parts


def _rglru_prompt_kernel(x_ref, g_ref, cw_ref, cb_ref, wr_ref, br_ref, wi_ref, bi_ref, lam_ref,
                         y_ref, h_ref, cs_ref, buf, h_scr):
    c = pl.program_id(1)
    first = c == 0

    @pl.when(first)
    def _():
        h_scr[...] = jnp.zeros_like(h_scr)

    xc, tail = _chunk_conv(x_ref, buf, cw_ref, cb_ref, first)
    cs_ref[0] = tail
    a_parts, u_parts = _rglru_gates(xc, wr_ref, br_ref, wi_ref, bi_ref, lam_ref)
    n_rows = xc.shape[0]
    row = lax.broadcasted_iota(jnp.int32, (n_rows, RG_BW), 0)
    for n in range(RG_BLOCKS):
        sl = slice(n * RG_BW, (n + 1) * RG_BW)
        a, u = a_parts[n], u_parts[n]
        s = 1
        while s < n_rows:
            keep = row >= s
            a_sh = jnp.where(keep, pltpu.roll(a, s, 0), 1.0)
            u_sh = jnp.where(keep, pltpu.roll(u, s, 0), 0.0)
            u = a * u_sh + u
            a = a * a_sh
            s *= 2
        h = a * h_scr[:, sl] + u
        h_scr[:, sl] = h[n_rows - 1:n_rows, :]
        y_ref[:, sl] = (h * _silu(g_ref[:, sl])).astype(BF16)
    h_ref[0] = h_scr[...]


def _rglru_prompt(proj, cw, cb, wr, br, wi, bi, lam, bsz, seq):
    nc = seq // CHUNK
    blk = lambda j: pl.BlockSpec((CHUNK, BRANCH), lambda b, c, j=j: (b * nc + c, j))
    const = lambda shape: pl.BlockSpec(shape, lambda b, c: (0,) * len(shape))
    return pl.pallas_call(
        _rglru_prompt_kernel,
        grid=(bsz, nc),
        in_specs=[blk(BLK_RG_X), blk(BLK_RG_G), const((CONV_W, BRANCH)), const((1, BRANCH)),
                  const((RG_BLOCKS, RG_BW, RG_BW)), const((1, BRANCH)),
                  const((RG_BLOCKS, RG_BW, RG_BW)), const((1, BRANCH)), const((1, BRANCH))],
        out_specs=[pl.BlockSpec((CHUNK, BRANCH), lambda b, c: (b * nc + c, 0)),
                   pl.BlockSpec((1, 1, BRANCH), lambda b, c: (b, 0, 0)),
                   pl.BlockSpec((1, CONV_W - 1, BRANCH), lambda b, c: (b, 0, 0))],
        out_shape=[jax.ShapeDtypeStruct((bsz * seq, BRANCH), BF16),
                   jax.ShapeDtypeStruct((bsz, 1, BRANCH), F32),
                   jax.ShapeDtypeStruct((bsz, CONV_W - 1, BRANCH), F32)],
        scratch_shapes=[pltpu.VMEM((8 + CHUNK, BRANCH), F32), pltpu.VMEM((1, BRANCH), F32)],
        compiler_params=_cparams("parallel", "arbitrary"),
        name="rglru_prompt",
    )(proj, proj, cw, cb, wr, br, wi, bi, lam)


def _ssd_prompt_kernel(z_ref, x_ref, bc_ref, sm_ref, cwx_ref, cbx_ref, cwb_ref, cbb_ref,
                       dtb_ref, a_ref, d_ref, nw_ref, tri_ref, exp_ref,
                       y_ref, s_ref, cs_ref, st_scr, xbuf, bcbuf):
    c = pl.program_id(1)
    first = c == 0
    n_rows = x_ref.shape[0]
    gw = BRANCH // SSD_G

    @pl.when(first)
    def _():
        st_scr[...] = jnp.zeros_like(st_scr)

    xc, xtail = _chunk_conv(x_ref, xbuf, cwx_ref, cbx_ref, first)
    bcc, bctail = _chunk_conv(bc_ref, bcbuf, cwb_ref, cbb_ref, first)
    cs_ref[0, :, 0:BRANCH] = xtail
    cs_ref[0, :, BRANCH:SSD_CONV_DIM] = bctail
    xs = _silu(xc)
    bcs = _silu(bcc)

    tri = tri_ref[...]
    expand = exp_ref[...]
    dt = _softplus(sm_ref[...] + dtb_ref[...])
    cum = _sel_left(tri, dt * a_ref[...])
    cum_t = cum.T
    dt_x = _sel_right(dt, expand)
    cum_x = _sel_right(cum, expand)
    cum_last = cum_x[n_rows - 1:n_rows, :]
    dec_in = jnp.exp(cum_x)
    xdt = xs * dt_x
    xw = (xdt * jnp.exp(cum_last - cum_x)).astype(BF16)
    xdt16 = xdt.astype(BF16)

    t_idx = lax.broadcasted_iota(jnp.int32, (n_rows, n_rows), 0)
    s_idx = lax.broadcasted_iota(jnp.int32, (n_rows, n_rows), 1)
    causal = s_idx <= t_idx
    lane = lax.broadcasted_iota(jnp.int32, (n_rows, 2 * SSD_P), 1)

    y_parts = []
    for g in range(SSD_G):
        b_g = bcs[:, g * SSD_N:(g + 1) * SSD_N].astype(BF16)
        c_g = bcs[:, (SSD_G + g) * SSD_N:(SSD_G + g + 1) * SSD_N].astype(BF16)
        cb = _dot_nt(c_g, b_g)
        st_g = st_scr[:, g * gw:(g + 1) * gw]
        y_inter = _dot(c_g, st_g.astype(BF16)) * dec_in[:, g * gw:(g + 1) * gw]
        heads_per_g = SSD_HEADS // SSD_G
        for pair in range(heads_per_g // 2):
            h0 = g * heads_per_g + 2 * pair
            xp = xdt16[:, h0 * SSD_P:(h0 + 2) * SSD_P]
            outs = []
            for h in (h0, h0 + 1):
                col = DT_LANE + h
                seg = cum[:, col:col + 1] - cum_t[col:col + 1, :]
                lmat = jnp.where(causal, jnp.exp(jnp.where(causal, seg, 0.0)), 0.0)
                outs.append(_dot((cb * lmat).astype(BF16), xp))
            y_parts.append(jnp.where(lane < SSD_P, outs[0], outs[1]))
        y_parts.append(y_inter)
        st_scr[:, g * gw:(g + 1) * gw] = (
            st_g * jnp.exp(cum_last[:, g * gw:(g + 1) * gw])
            + _dot(bcs[:, g * SSD_N:(g + 1) * SSD_N].T.astype(BF16), xw[:, g * gw:(g + 1) * gw]))
    npair = SSD_HEADS // SSD_G // 2
    y = jnp.concatenate(
        [jnp.concatenate(y_parts[g * (npair + 1):g * (npair + 1) + npair], axis=1)
         + y_parts[g * (npair + 1) + npair] for g in range(SSD_G)], axis=1)
    y = (y + xs * d_ref[...]) * _silu(z_ref[...])
    y_ref[...] = _group_norm(y, nw_ref[...], gw).astype(BF16)

    @pl.when(c == pl.num_programs(1) - 1)
    def _():
        s_ref[0] = st_scr[...].T.reshape(SSD_HEADS, SSD_P, SSD_N)


def _ssd_expand_const():
    e = np.zeros((128, BRANCH), np.float32)
    for h in range(SSD_HEADS):
        e[DT_LANE + h, h * SSD_P:(h + 1) * SSD_P] = 1.0
    return jnp.asarray(e, dtype=BF16)


def _ssd_prompt(proj, cwx, cbx, cwb, cbb, dtb, a_pad, d_x, nw, bsz, seq):
    nc = seq // CHUNK
    blk = lambda w, j: pl.BlockSpec((CHUNK, w), lambda b, c, j=j: (b * nc + c, j))
    const = lambda shape: pl.BlockSpec(shape, lambda b, c: (0,) * len(shape))
    return pl.pallas_call(
        _ssd_prompt_kernel,
        grid=(bsz, nc),
        in_specs=[blk(BRANCH, BLK_SS_Z), blk(BRANCH, BLK_SS_X), blk(SSD_BC, BLK_SS_BC),
                  blk(128, BLK_SMALL),
                  const((CONV_W, BRANCH)), const((1, BRANCH)), const((CONV_W, SSD_BC)),
                  const((1, SSD_BC)), const((1, 128)), const((1, 128)), const((1, BRANCH)),
                  const((1, BRANCH)), const((CHUNK, CHUNK)), const((128, BRANCH))],
        out_specs=[pl.BlockSpec((CHUNK, BRANCH), lambda b, c: (b * nc + c, 0)),
                   pl.BlockSpec((1, SSD_HEADS, SSD_P, SSD_N), lambda b, c: (b, 0, 0, 0)),
                   pl.BlockSpec((1, CONV_W - 1, SSD_CONV_DIM), lambda b, c: (b, 0, 0))],
        out_shape=[jax.ShapeDtypeStruct((bsz * seq, BRANCH), BF16),
                   jax.ShapeDtypeStruct((bsz, SSD_HEADS, SSD_P, SSD_N), F32),
                   jax.ShapeDtypeStruct((bsz, CONV_W - 1, SSD_CONV_DIM), F32)],
        scratch_shapes=[pltpu.VMEM((SSD_N, BRANCH), F32),
                        pltpu.VMEM((8 + CHUNK, BRANCH), F32),
                        pltpu.VMEM((8 + CHUNK, SSD_BC), F32)],
        compiler_params=_cparams("parallel", "arbitrary"),
        name="ssd_prompt",
    )(proj, proj, proj, proj, cwx, cbx, cwb, cbb, dtb, a_pad, d_x, nw,
      _tri_const(CHUNK), _ssd_expand_const())


def _step_conv(x, cs_ref, w_ref, b_ref, ncs_ref):
    y = b_ref[...] + x * w_ref[CONV_W - 1:CONV_W, :]
    for j in range(CONV_W - 1):
        y = y + cs_ref[j] * w_ref[j:j + 1, :]
    for j in range(CONV_W - 2):
        ncs_ref[j] = cs_ref[j + 1]
    ncs_ref[CONV_W - 2] = x
    return y


def _sample_pre_kernel(p_ref, lb_ref, rcs_ref, rh_ref, rcw_ref, rcb_ref, wr_ref, br_ref, wi_ref,
                       bi_ref, lam_ref, wup_ref, bup_ref, scs_ref, scw_ref, scb_ref, dtb_ref,
                       a_ref, exp_ref,
                       hq_ref, hk_ref, hd_ref, gq_ref, gd_ref, yrg_ref, nrh_ref, nrcs_ref,
                       sx_ref, sbc_ref, sdx_ref, sda_ref, nscs_ref):
    col = lambda blk, w: slice(blk * w, (blk + 1) * w)
    f = p_ref[:, col(BLK_HG_F, BRANCH)]
    lb = lb_ref[...]
    hq_ref[...] = _silu(p_ref[:, col(BLK_HG_Q, BRANCH)])
    sg = _sigmoid(f)
    hk_ref[...] = (1.0 - lb) * (1.0 - sg)
    hd_ref[...] = jnp.maximum(lb + (1.0 - lb) * sg, TINY)
    sm = p_ref[:, col(BLK_SMALL, 128)]
    up = _dot(sm.astype(BF16), wup_ref[...]) + bup_ref[...]
    gq_ref[...] = p_ref[:, col(BLK_GL_Q, GLA_KDIM)] * (GLA_DK ** -0.5)
    gd_ref[...] = jnp.exp(-_softplus(-up) * (1.0 / GLA_TAU))
    xc = _step_conv(p_ref[:, col(BLK_RG_X, BRANCH)], rcs_ref, rcw_ref, rcb_ref, nrcs_ref)
    a_parts, u_parts = _rglru_gates(xc, wr_ref, br_ref, wi_ref, bi_ref, lam_ref)
    h = jnp.concatenate(a_parts, axis=1) * rh_ref[...] + jnp.concatenate(u_parts, axis=1)
    nrh_ref[...] = h
    yrg_ref[...] = (h * _silu(p_ref[:, col(BLK_RG_G, BRANCH)])).astype(BF16)
    xbc = jnp.concatenate([p_ref[:, col(BLK_SS_X, BRANCH)], p_ref[:, col(BLK_SS_BC, SSD_BC)]], axis=1)
    xbc = _silu(_step_conv(xbc, scs_ref, scw_ref, scb_ref, nscs_ref))
    xs = xbc[:, 0:BRANCH]
    sx_ref[...] = xs
    sbc_ref[...] = xbc[:, BRANCH:SSD_CONV_DIM]
    dt = _softplus(sm + dtb_ref[...])
    expand = exp_ref[...]
    sdx_ref[...] = xs * _sel_right(dt, expand)
    sda_ref[...] = jnp.exp(_sel_right(dt * a_ref[...], expand))


def _sample_pre(proj, lb, rcs, rh, rcw, rcb, wr, br, wi, bi, lam, wup, bup, scs, scw, scb,
                dtb, a_pad):
    nb = proj.shape[0]
    sd = lambda *shape, dt=F32: jax.ShapeDtypeStruct(shape, dt)
    return pl.pallas_call(
        _sample_pre_kernel,
        out_shape=[sd(nb, BRANCH), sd(nb, BRANCH), sd(nb, BRANCH),
                   sd(nb, GLA_KDIM), sd(nb, GLA_KDIM),
                   sd(nb, BRANCH, dt=BF16), sd(nb, BRANCH), sd(CONV_W - 1, nb, BRANCH),
                   sd(nb, BRANCH), sd(nb, SSD_BC), sd(nb, BRANCH), sd(nb, BRANCH),
                   sd(CONV_W - 1, nb, SSD_CONV_DIM)],
        compiler_params=pltpu.CompilerParams(vmem_limit_bytes=VMEM_LIMIT),
        name="sample_pre",
    )(proj, lb, rcs, rh, rcw, rcb, wr, br, wi, bi, lam, wup, bup, scs, scw, scb, dtb, a_pad,
      _ssd_expand_const())


def _pad_t(x):
    r = x.shape[0]
    return jnp.concatenate([x, jnp.zeros((128 - r, 128), F32)], axis=0).T


def _gla_state_kernel(d_ref, k_ref, v_ref, q_ref, s_ref, *rest, heads):
    so_ref, o_ref = rest[-2:]

    def body(b, carry):
        dt_ = _pad_t(d_ref[b])
        kt_ = _pad_t(k_ref[b])
        vr = v_ref[b]
        qr = q_ref[b]
        for h in range(heads):
            s_new = dt_[:, h:h + 1] * s_ref[b, h] + kt_[:, h:h + 1] * vr[h:h + 1, :]
            so_ref[b, h] = s_new
            q8 = jnp.broadcast_to(qr[h:h + 1, :], (8, qr.shape[1])).astype(BF16)
            o_ref[b, h:h + 1, :] = _dot(q8, s_new.astype(BF16))[0:1, :]
        return carry
    lax.fori_loop(0, s_ref.shape[0], body, 0, unroll=True)


def _state_call(kern, name, vec_args, vec_specs, s_all, so_prev, layer, o_shape, o_spec, bb):
    nb = s_all.shape[1]
    st = pl.BlockSpec((None, bb) + s_all.shape[2:], lambda i: (layer, i, 0, 0, 0))
    in_specs = list(vec_specs) + [st]
    args = list(vec_args) + [s_all]
    aliases = {}
    if so_prev is not None:
        in_specs.append(pl.BlockSpec(memory_space=pl.ANY))
        args.append(so_prev)
        aliases = {len(args) - 1: 0}
    return pl.pallas_call(
        kern,
        grid=(nb // bb,),
        in_specs=in_specs,
        out_specs=[st, o_spec],
        out_shape=[jax.ShapeDtypeStruct(s_all.shape, F32), o_shape],
        input_output_aliases=aliases,
        compiler_params=_cparams("parallel"),
        name=name,
    )(*args)


def _gla_state(d, k, v, q, s_all, so_prev, layer, *, bb):
    _, nb, heads, dk, dv = s_all.shape
    vec = lambda w: pl.BlockSpec((bb, heads, w), lambda i: (i, 0, 0))
    return _state_call(functools.partial(_gla_state_kernel, heads=heads), "gla_state",
                       (d, k, v, q), (vec(dk), vec(dk), vec(dv), vec(dk)), s_all, so_prev, layer,
                       jax.ShapeDtypeStruct((nb, heads, dv), F32), vec(dv), bb)


def _ssd_state_kernel(da_ref, dx_ref, b_ref, c_ref, s_ref, *rest):
    so_ref, y_ref = rest[-2:]
    hpg = SSD_HEADS // SSD_G

    def body(b, carry):
        at_ = _pad_t(da_ref[b])
        xt_ = _pad_t(dx_ref[b])
        br = b_ref[b]
        cr = c_ref[b]
        for h in range(SSD_HEADS):
            g = h // hpg
            rows = slice((h % 2) * SSD_P, (h % 2 + 1) * SSD_P)
            j = h // 2
            s_new = at_[rows, j:j + 1] * s_ref[b, h] + xt_[rows, j:j + 1] * br[g:g + 1, :]
            so_ref[b, h] = s_new
            c8 = jnp.broadcast_to(cr[g:g + 1, :], (8, SSD_N)).astype(BF16)
            y_ref[b, h:h + 1, :] = _dot_nt(c8, s_new.astype(BF16))[0:1, :]
        return carry
    lax.fori_loop(0, s_ref.shape[0], body, 0, unroll=True)


def _ssd_state(da, dx, bv, cv, s_all, so_prev, layer, *, bb):
    nb = s_all.shape[1]
    vec = lambda r, w: pl.BlockSpec((bb, r, w), lambda i: (i, 0, 0))
    return _state_call(_ssd_state_kernel, "ssd_state", (da, dx, bv, cv),
                       (vec(8, 128), vec(8, 128), vec(SSD_G, SSD_N), vec(SSD_G, SSD_N)),
                       s_all, so_prev, layer,
                       jax.ShapeDtypeStruct((nb, SSD_HEADS, SSD_P), F32), vec(SSD_HEADS, SSD_P), bb)


def _sample_post_kernel(p_ref, ohg_ref, ogl_ref, yss_ref, sx_ref, d_ref, hnw_ref, gnw_ref,
                        snw_ref, yhg_ref, ygl_ref, yso_ref):
    col = lambda blk: slice(blk * BRANCH, (blk + 1) * BRANCH)
    yhg_ref[...] = (_group_norm(ohg_ref[...], hnw_ref[...], HG_DV)
                    * _silu(p_ref[:, col(BLK_HG_G)])).astype(BF16)
    ygl_ref[...] = (_group_norm(ogl_ref[...], gnw_ref[...], GLA_DV)
                    * _silu(p_ref[:, col(BLK_GL_G)])).astype(BF16)
    y = (yss_ref[...] + sx_ref[...] * d_ref[...]) * _silu(p_ref[:, col(BLK_SS_Z)])
    yso_ref[...] = _group_norm(y, snw_ref[...], BRANCH // SSD_G).astype(BF16)


def _sample_post(proj, ohg, ogl, yss, sx, d_x, hnw, gnw, snw):
    nb = proj.shape[0]
    out = jax.ShapeDtypeStruct((nb, BRANCH), BF16)
    return pl.pallas_call(
        _sample_post_kernel, out_shape=[out, out, out],
        compiler_params=pltpu.CompilerParams(vmem_limit_bytes=VMEM_LIMIT),
        name="sample_post",
    )(proj, ohg, ogl, yss, sx, d_x, hnw, gnw, snw)


def _prep_w_tail(w_in):
    pad = jnp.zeros(w_in.shape[:2] + (N_PROJ - N_IN,), w_in.dtype)
    w = jnp.concatenate([w_in[..., ORIG_SSD_Z:ORIG_SSD_DT], w_in[..., ORIG_GLA_A:ORIG_SSD_Z],
                         w_in[..., ORIG_SSD_DT:N_IN], pad], axis=-1)
    return w.astype(BF16)


def _pad_lanes(v, start, width=128):
    out = jnp.zeros((v.shape[0], 1, width), F32)
    return out.at[:, 0, start:start + v.shape[1]].set(v.astype(F32))


def kernel(x_prompt, x_sample, state_hgrn, state_rglru, state_rglru_conv, state_gla, state_ssd, state_ssd_conv, rms_in, w_in, hgrn_lower_bounds, hgrn_norm, rglru_conv_w, rglru_conv_b, rglru_w_r, rglru_b_r, rglru_w_i, rglru_b_i, rglru_lambda, gla_w_up, gla_b_up, gla_norm, ssd_conv_w, ssd_conv_b, ssd_dt_bias, ssd_a_log, ssd_d, ssd_norm, w_out, rms_final):
    bsz, seq, _ = x_prompt.shape
    nb = x_sample.shape[0]
    row = lambda v: v.reshape(DEPTH, 1, -1).astype(F32)

    lb_all = _lower_bounds(hgrn_lower_bounds.astype(F32)).reshape(DEPTH, 1, BRANCH)
    w_in = w_in.astype(F32)
    w_tail = _prep_w_tail(w_in)
    w_out16 = w_out.astype(BF16)
    wr16 = rglru_w_r.astype(BF16)
    wi16 = rglru_w_i.astype(BF16)
    wup16 = jnp.concatenate(
        [gla_w_up, jnp.zeros((DEPTH, 128 - GLA_RANK, GLA_KDIM), gla_w_up.dtype)], axis=1).astype(BF16)
    dtb = _pad_lanes(ssd_dt_bias, DT_LANE)
    a_pad = _pad_lanes(-jnp.exp(ssd_a_log.astype(F32)), DT_LANE)
    d_x = jnp.repeat(ssd_d.astype(F32), SSD_P, axis=-1).reshape(DEPTH, 1, BRANCH)
    rms_in_r, hnw, gnw, snw = row(rms_in), row(hgrn_norm), row(gla_norm), row(ssd_norm)
    rcb, br, bi, lam, bup = (row(rglru_conv_b), row(rglru_b_r), row(rglru_b_i),
                             row(rglru_lambda), row(gla_b_up))
    scb = row(ssd_conv_b)

    xp = x_prompt.reshape(bsz * seq, D_MODEL)
    xs = x_sample.reshape(nb, D_MODEL)
    tm_in = 1024 if (bsz * seq) % 1024 == 0 else CHUNK
    tm_out = 256 if (bsz * seq) % 256 == 0 else CHUNK
    bb = 4 if nb % 4 == 0 else 1
    rf = rms_final.reshape(1, D_MODEL).astype(F32)

    hp = _rmsnorm(xp, rms_in_r[0], BF16, tm=tm_out)
    hs = _rmsnorm(xs, rms_in_r[0], BF16, tm=nb)
    outs_p = [[] for _ in range(6)]
    outs_s = [[] for _ in range(3)]
    ns_hg = ns_gl = ns_ss = None
    for l in range(DEPTH):
        scw = ssd_conv_w[l].astype(F32)
        next_w, next_dt = (rms_in_r[l + 1], BF16) if l + 1 < DEPTH else (rf, F32)
        proj = _inproj(hp, w_in, w_tail, l, tm=tm_in, tn=1024)
        y_hg, s_hg = _hgrn_prompt(proj, lb_all[l], hnw[l], bsz, seq)
        y_rg, s_rg, s_rgc = _rglru_prompt(proj, rglru_conv_w[l].astype(F32), rcb[l], wr16[l], br[l],
                                          wi16[l], bi[l], lam[l], bsz, seq)
        y_gl, s_gl = _gla_prompt(proj, wup16[l], bup[l], gnw[l], bsz, seq)
        y_ss, s_ss, s_ssc = _ssd_prompt(proj, scw[:, :BRANCH], scb[l][:, :BRANCH], scw[:, BRANCH:],
                                        scb[l][:, BRANCH:], dtb[l], a_pad[l], d_x[l], snw[l], bsz, seq)
        xp, hp = _outproj((y_hg, y_rg, y_gl, y_ss), w_out16[l], xp, next_w, next_dt, tm=tm_out)
        for lst, s in zip(outs_p, (s_hg, s_rg.reshape(bsz, BRANCH), s_rgc, s_gl, s_ss, s_ssc)):
            lst.append(s)

        proj_s = _inproj(hs, w_in, w_tail, l, tm=nb, tn=1024)
        (hq, hk, hd, gq, gd, yrg_s, nrh, nrcs, sx, sbc, sdx, sda, nscs) = _sample_pre(
            proj_s, lb_all[l], jnp.swapaxes(state_rglru_conv[l], 0, 1), state_rglru[l],
            rglru_conv_w[l].astype(F32), rcb[l], wr16[l], br[l], wi16[l], bi[l], lam[l],
            wup16[l], bup[l], jnp.swapaxes(state_ssd_conv[l], 0, 1), scw, scb[l], dtb[l], a_pad[l])
        hv = proj_s[:, BLK_HG_I * BRANCH:(BLK_HG_I + 1) * BRANCH]
        gk = proj_s[:, BLK_GL_K * GLA_KDIM:(BLK_GL_K + 1) * GLA_KDIM]
        gv = proj_s[:, BLK_GL_V * BRANCH:(BLK_GL_V + 1) * BRANCH]
        hsh = lambda a: a.reshape(nb, HG_HEADS, -1)
        gsh = lambda a: a.reshape(nb, GLA_HEADS, -1)
        ns_hg, o_hg = _gla_state(hsh(hd), hsh(hk), hsh(hv), hsh(hq), state_hgrn, ns_hg, l, bb=bb)
        ns_gl, o_gl = _gla_state(gsh(gd), gsh(gk), gsh(gv), gsh(gq), state_gla, ns_gl, l, bb=bb)
        ns_ss, y_ssr = _ssd_state(sda.reshape(nb, 8, 128), sdx.reshape(nb, 8, 128),
                                  sbc[:, :SSD_G * SSD_N].reshape(nb, SSD_G, SSD_N),
                                  sbc[:, SSD_G * SSD_N:].reshape(nb, SSD_G, SSD_N),
                                  state_ssd, ns_ss, l, bb=bb)
        yhg_s, ygl_s, yss_s = _sample_post(proj_s, o_hg.reshape(nb, BRANCH), o_gl.reshape(nb, BRANCH),
                                           y_ssr.reshape(nb, BRANCH), sx, d_x[l], hnw[l], gnw[l], snw[l])
        xs, hs = _outproj((yhg_s, yrg_s, ygl_s, yss_s), w_out16[l], xs, next_w, next_dt, tm=nb)
        for lst, s in zip(outs_s, (nrh, jnp.swapaxes(nrcs, 0, 1), jnp.swapaxes(nscs, 0, 1))):
            lst.append(s)

    y_prompt = hp.reshape(bsz, seq, D_MODEL)
    y_sample = hs.reshape(nb, 1, D_MODEL)
    s_rg, s_rgc, s_ssc = (jnp.stack(l) for l in outs_s)
    return ((y_prompt, y_sample) + tuple(jnp.stack(l) for l in outs_p)
            + (ns_hg, s_rg, s_rgc, ns_gl, ns_ss, s_ssc))
```

```python
import functools
import math

import numpy as np
import jax
import jax.numpy as jnp
from jax import lax
from jax.experimental import pallas as pl
from jax.experimental.pallas import tpu as pltpu

F32 = jnp.float32
BF16 = jnp.bfloat16

D_MODEL = 2048
DEPTH = 4
BRANCH = 1024
D_MIX = 4 * BRANCH
CONV_W = 4
EPS = 1e-6
TINY = 1e-30

HG_HEADS, HG_DK, HG_DV = 8, 128, 128
RG_BLOCKS, RG_BW, RG_C = 8, 128, 8.0
GLA_HEADS, GLA_DK, GLA_DV, GLA_RANK, GLA_TAU = 4, 128, 256, 16, 16.0
GLA_KDIM = GLA_HEADS * GLA_DK
SSD_HEADS, SSD_P, SSD_G, SSD_N = 16, 64, 2, 128
SSD_BC = 2 * SSD_G * SSD_N
SSD_CONV_DIM = BRANCH + SSD_BC

ORIG_GLA_A = 9216
ORIG_SSD_Z = 9232
ORIG_SSD_DT = 11792
N_IN = 11808
N_PROJ = 12288
COL_SMALL = 11776
DT_LANE = 16
BLK_HG_Q, BLK_HG_F, BLK_HG_I, BLK_HG_G = 0, 1, 2, 3
BLK_RG_X, BLK_RG_G = 4, 5
BLK_GL_Q, BLK_GL_K = 12, 13
BLK_GL_V, BLK_GL_G = 7, 8
BLK_SS_Z, BLK_SS_X = 9, 10
BLK_SS_BC = 22
BLK_SMALL = COL_SMALL // 128

CHUNK = 128
SUB = 16
VMEM_LIMIT = 52 * 1024 * 1024


def _cparams(*sem):
    return pltpu.CompilerParams(dimension_semantics=sem, vmem_limit_bytes=VMEM_LIMIT)


def _sigmoid(x):
    return 0.5 * jnp.tanh(0.5 * x) + 0.5


def _silu(x):
    return x * _sigmoid(x)


def _softplus(x):
    return jnp.maximum(x, 0.0) + jnp.log(1.0 + jnp.exp(-jnp.abs(x)))


def _dot(a, b):
    return jnp.dot(a, b, preferred_element_type=F32)


def _dot_nt(a, b):
    return lax.dot_general(a, b, (((1,), (1,)), ((), ())), preferred_element_type=F32)


def _split3(a):
    a0 = a.astype(BF16)
    r1 = a - a0.astype(F32)
    a1 = r1.astype(BF16)
    a2 = (r1 - a1.astype(F32)).astype(BF16)
    return a0, a1, a2


def _sel_left(sel, x):
    x0, x1, x2 = _split3(x)
    return _dot(sel, x0) + _dot(sel, x1) + _dot(sel, x2)


def _sel_right(x, sel):
    x0, x1, x2 = _split3(x)
    return _dot(x0, sel) + _dot(x1, sel) + _dot(x2, sel)


def _group_norm(y, w, width):
    parts = []
    for g in range(y.shape[1] // width):
        yg = y[:, g * width:(g + 1) * width]
        ms = jnp.mean(yg * yg, axis=-1, keepdims=True)
        parts.append(yg * lax.rsqrt(ms + EPS))
    out = parts[0] if len(parts) == 1 else jnp.concatenate(parts, axis=1)
    return out * w


def _tri_const(c):
    return jnp.asarray(np.tril(np.ones((c, c), np.float32)), dtype=BF16)


def _level_const(c):
    t = np.arange(c)[:, None]
    s = np.arange(c)[None, :]
    lvl = np.zeros((c, c), np.int32)
    lvl[(t // SUB == s // SUB) & (s <= t)] = 1
    h, code = SUB, 2
    while h < c:
        m = (t // (2 * h) == s // (2 * h)) & (t % (2 * h) >= h) & (s % (2 * h) < h)
        lvl[m] = code
        h *= 2
        code += 1
    return jnp.asarray(lvl)


def _lb_kernel(p_ref, o_ref):
    x = p_ref[...]
    m = jnp.max(x, axis=0, keepdims=True)
    e = jnp.exp(x - m)
    p = e / jnp.sum(e, axis=0, keepdims=True)
    acc = jnp.zeros_like(p[0:1])
    rows = [acc]
    for l in range(1, DEPTH):
        acc = acc + p[l:l + 1]
        rows.append(acc)
    o_ref[...] = jnp.concatenate(rows, axis=0)


def _lower_bounds(param):
    return pl.pallas_call(
        _lb_kernel, out_shape=jax.ShapeDtypeStruct(param.shape, F32), name="hgrn_lb")(param)


N_MAIN = ORIG_GLA_A
N_TAIL = N_PROJ - N_MAIN


def _inproj_kernel(h_ref, w_ref, wt_ref, o_ref, wb_scr, *, n_main, rb):
    j = pl.program_id(0)

    @pl.when(pl.program_id(1) == 0)
    def _():
        @pl.when(j < n_main)
        def _():
            def body(i, carry):
                r = pl.multiple_of(i * rb, rb)
                wb_scr[pl.ds(r, rb), :] = w_ref[pl.ds(r, rb), :].astype(BF16)
                return carry
            lax.fori_loop(0, wb_scr.shape[0] // rb, body, 0)

        @pl.when(j >= n_main)
        def _():
            wb_scr[...] = wt_ref[...]

    o_ref[...] = _dot(h_ref[...], wb_scr[...])


def _inproj(h, w_in, w_tail, layer, *, tm, tn):
    m, d = h.shape
    n_main = N_MAIN // tn
    return pl.pallas_call(
        functools.partial(_inproj_kernel, n_main=n_main, rb=256),
        grid=(N_PROJ // tn, m // tm),
        in_specs=[pl.BlockSpec((tm, d), lambda j, i: (i, 0)),
                  pl.BlockSpec((None, d, tn), lambda j, i: (layer, 0, jnp.minimum(j, n_main - 1))),
                  pl.BlockSpec((None, d, tn), lambda j, i: (layer, 0, jnp.maximum(j - n_main, 0)))],
        out_specs=pl.BlockSpec((tm, tn), lambda j, i: (i, j)),
        out_shape=jax.ShapeDtypeStruct((m, N_PROJ), F32),
        scratch_shapes=[pltpu.VMEM((d, tn), BF16)],
        compiler_params=_cparams("arbitrary", "arbitrary"),
        name="inproj",
    )(h, w_in, w_tail)


def _outproj_kernel(y0_ref, y1_ref, y2_ref, y3_ref, w_ref, x_ref, nw_ref, xo_ref, ho_ref, *, rb):
    acc = x_ref[...]
    for g, y_ref in enumerate((y0_ref, y1_ref, y2_ref, y3_ref)):
        acc = acc + _dot(y_ref[...], w_ref[g * BRANCH:(g + 1) * BRANCH, :])
    xo_ref[...] = acc

    def body(i, carry):
        r = pl.multiple_of(i * rb, rb)
        x = xo_ref[pl.ds(r, rb), :]
        ms = jnp.mean(x * x, axis=-1, keepdims=True)
        ho_ref[pl.ds(r, rb), :] = (x * lax.rsqrt(ms + EPS) * nw_ref[...]).astype(ho_ref.dtype)
        return carry
    lax.fori_loop(0, xo_ref.shape[0] // rb, body, 0)


def _outproj(ys, w, x, norm_w, norm_dtype, *, tm):
    m, d = x.shape
    yspec = pl.BlockSpec((tm, BRANCH), lambda i: (i, 0))
    xspec = pl.BlockSpec((tm, d), lambda i: (i, 0))
    return pl.pallas_call(
        functools.partial(_outproj_kernel, rb=min(tm, 64)),
        grid=(m // tm,),
        in_specs=[yspec, yspec, yspec, yspec,
                  pl.BlockSpec((D_MIX, d), lambda i: (0, 0), pipeline_mode=pl.Buffered(1)),
                  xspec, pl.BlockSpec((1, d), lambda i: (0, 0))],
        out_specs=[xspec, xspec],
        out_shape=[jax.ShapeDtypeStruct((m, d), F32), jax.ShapeDtypeStruct((m, d), norm_dtype)],
        compiler_params=_cparams("parallel"),
        name="outproj",
    )(*ys, w, x, norm_w)


def _rmsnorm_kernel(x_ref, w_ref, o_ref):
    x = x_ref[...]
    ms = jnp.mean(x * x, axis=-1, keepdims=True)
    o_ref[...] = (x * lax.rsqrt(ms + EPS) * w_ref[...]).astype(o_ref.dtype)


def _rmsnorm(x, w, out_dtype, *, tm):
    m, d = x.shape
    return pl.pallas_call(
        _rmsnorm_kernel,
        grid=(m // tm,),
        in_specs=[pl.BlockSpec((tm, d), lambda i: (i, 0)),
                  pl.BlockSpec((1, d), lambda i: (0, 0))],
        out_specs=pl.BlockSpec((tm, d), lambda i: (i, 0)),
        out_shape=jax.ShapeDtypeStruct((m, d), out_dtype),
        compiler_params=_cparams("parallel"),
        name="rmsnorm",
    )(x, w)


def _gla_chunk_heads(qs, ks, get_v, logfs, get_st, tri, lvl):
    n = len(qs)
    c, width = qs[0].shape
    bs = [_sel_left(tri, lf) for lf in logfs]

    def ref_rows(b, rows, span):
        return jnp.concatenate(
            [jnp.broadcast_to(b[r:r + 1, :], (span, width)) for r in rows], axis=0)

    s_diag = []
    for h in range(n):
        ed = bs[h] - ref_rows(bs[h], [SUB * m + SUB // 2 for m in range(c // SUB)], SUB)
        s_diag.append(_dot_nt((qs[h] * jnp.exp(ed)).astype(BF16), (ks[h] * jnp.exp(-ed)).astype(BF16)))
    s_levels = []
    half = SUB
    while half < c:
        cur = []
        for h in range(n):
            ref = ref_rows(bs[h], [2 * half * m + half - 1 for m in range(c // (2 * half))], 2 * half)
            e = jnp.exp(-jnp.abs(bs[h] - ref))
            cur.append(_dot_nt((qs[h] * e).astype(BF16), (ks[h] * e).astype(BF16)))
        s_levels.append(cur)
        half *= 2

    outs, new_sts = [], []
    for h in range(n):
        scores = jnp.where(lvl == 1, s_diag[h], 0.0)
        for i, cur in enumerate(s_levels):
            scores = jnp.where(lvl == i + 2, cur[h], scores)
        b = bs[h]
        b_last = b[c - 1:c, :]
        q_in = (qs[h] * jnp.exp(b)).astype(BF16)
        k_end = (ks[h] * jnp.exp(b_last - b)).astype(BF16)
        v = get_v(h)
        st = get_st(h)
        outs.append(_dot(scores.astype(BF16), v.astype(BF16)) + _dot_nt(q_in, st.astype(BF16)))
        new_sts.append(st * jnp.exp(b_last) + _dot(v.T.astype(BF16), k_end))
    return outs, new_sts


def _hgrn_prompt_kernel(q_ref, f_ref, i_ref, g_ref, lb_ref, nw_ref, tri_ref, lvl_ref,
                        y_ref, s_ref, st_scr):
    c = pl.program_id(1)

    @pl.when(c == 0)
    def _():
        st_scr[...] = jnp.zeros_like(st_scr)

    sls = [slice(h * HG_DK, (h + 1) * HG_DK) for h in range(HG_HEADS)]
    qs, ks, logfs = [], [], []
    for sl in sls:
        lb = lb_ref[:, sl]
        sg = _sigmoid(f_ref[:, sl])
        logfs.append(jnp.log(jnp.maximum(lb + (1.0 - lb) * sg, TINY)))
        ks.append((1.0 - lb) * (1.0 - sg))
        qs.append(_silu(q_ref[:, sl]))
    outs, new_sts = _gla_chunk_heads(qs, ks, lambda h: i_ref[:, sls[h]], logfs,
                                     lambda h: st_scr[h], tri_ref[...], lvl_ref[...])
    for h, sl in enumerate(sls):
        st_scr[h] = new_sts[h]
        y = _group_norm(outs[h], nw_ref[:, sl], HG_DV) * _silu(g_ref[:, sl])
        y_ref[:, sl] = y.astype(BF16)

    @pl.when(c == pl.num_programs(1) - 1)
    def _():
        for h in range(HG_HEADS):
            s_ref[0, h] = st_scr[h].T


def _hgrn_prompt(proj, lb, nw, bsz, seq):
    nc = seq // CHUNK
    blk = lambda j: pl.BlockSpec((CHUNK, BRANCH), lambda b, c, j=j: (b * nc + c, j))
    row = pl.BlockSpec((1, BRANCH), lambda b, c: (0, 0))
    cc = pl.BlockSpec((CHUNK, CHUNK), lambda b, c: (0, 0))
    return pl.pallas_call(
        _hgrn_prompt_kernel,
        grid=(bsz, nc),
        in_specs=[blk(BLK_HG_Q), blk(BLK_HG_F), blk(BLK_HG_I), blk(BLK_HG_G), row, row, cc, cc],
        out_specs=[pl.BlockSpec((CHUNK, BRANCH), lambda b, c: (b * nc + c, 0)),
                   pl.BlockSpec((1, HG_HEADS, HG_DK, HG_DV), lambda b, c: (b, 0, 0, 0))],
        out_shape=[jax.ShapeDtypeStruct((bsz * seq, BRANCH), BF16),
                   jax.ShapeDtypeStruct((bsz, HG_HEADS, HG_DK, HG_DV), F32)],
        scratch_shapes=[pltpu.VMEM((HG_HEADS, HG_DV, HG_DK), F32)],
        compiler_params=_cparams("parallel", "arbitrary"),
        name="hgrn_prompt",
    )(proj, proj, proj, proj, lb, nw, _tri_const(CHUNK), _level_const(CHUNK))


def _gla_prompt_kernel(q_ref, k_ref, v_ref, g_ref, sm_ref, wup_ref, bup_ref, nw_ref,
                       tri_ref, lvl_ref, y_ref, s_ref, st_scr):
    c = pl.program_id(1)

    @pl.when(c == 0)
    def _():
        st_scr[...] = jnp.zeros_like(st_scr)

    up = _dot(sm_ref[...].astype(BF16), wup_ref[...]) + bup_ref[...]
    log_a = -_softplus(-up) * (1.0 / GLA_TAU)
    ksl = [slice(h * GLA_DK, (h + 1) * GLA_DK) for h in range(GLA_HEADS)]
    vsl = [slice(h * GLA_DV, (h + 1) * GLA_DV) for h in range(GLA_HEADS)]
    outs, new_sts = _gla_chunk_heads(
        [q_ref[:, sl] * (GLA_DK ** -0.5) for sl in ksl], [k_ref[:, sl] for sl in ksl],
        lambda h: v_ref[:, vsl[h]], [log_a[:, sl] for sl in ksl], lambda h: st_scr[h],
        tri_ref[...], lvl_ref[...])
    for h, sl in enumerate(vsl):
        st_scr[h] = new_sts[h]
        y = _group_norm(outs[h], nw_ref[:, sl], GLA_DV) * _silu(g_ref[:, sl])
        y_ref[:, sl] = y.astype(BF16)

    @pl.when(c == pl.num_programs(1) - 1)
    def _():
        for h in range(GLA_HEADS):
            s_ref[0, h] = st_scr[h].T


def _gla_prompt(proj, wup, bup, nw, bsz, seq):
    nc = seq // CHUNK
    blk = lambda w, j: pl.BlockSpec((CHUNK, w), lambda b, c, j=j: (b * nc + c, j))
    const = lambda shape: pl.BlockSpec(shape, lambda b, c: (0,) * len(shape))
    return pl.pallas_call(
        _gla_prompt_kernel,
        grid=(bsz, nc),
        in_specs=[blk(GLA_KDIM, BLK_GL_Q), blk(GLA_KDIM, BLK_GL_K), blk(BRANCH, BLK_GL_V),
                  blk(BRANCH, BLK_GL_G), blk(128, BLK_SMALL),
                  const((128, GLA_KDIM)), const((1, GLA_KDIM)), const((1, BRANCH)),
                  const((CHUNK, CHUNK)), const((CHUNK, CHUNK))],
        out_specs=[pl.BlockSpec((CHUNK, BRANCH), lambda b, c: (b * nc + c, 0)),
                   pl.BlockSpec((1, GLA_HEADS, GLA_DK, GLA_DV), lambda b, c: (b, 0, 0, 0))],
        out_shape=[jax.ShapeDtypeStruct((bsz * seq, BRANCH), BF16),
                   jax.ShapeDtypeStruct((bsz, GLA_HEADS, GLA_DK, GLA_DV), F32)],
        scratch_shapes=[pltpu.VMEM((GLA_HEADS, GLA_DV, GLA_DK), F32)],
        compiler_params=_cparams("parallel", "arbitrary"),
        name="gla_prompt",
    )(proj, proj, proj, proj, proj, wup, bup, nw, _tri_const(CHUNK), _level_const(CHUNK))


def _chunk_conv(x_ref, buf, w_ref, b_ref, first):
    c = x_ref.shape[0]

    @pl.when(first)
    def _():
        buf[0:8, :] = jnp.zeros((8, buf.shape[1]), F32)

    buf[8:8 + c, :] = x_ref[...]
    y = b_ref[...] + buf[5:5 + c, :] * w_ref[0:1, :]
    for j in range(1, CONV_W):
        y = y + buf[5 + j:5 + j + c, :] * w_ref[j:j + 1, :]
    tail = buf[c:c + 8, :]
    buf[0:8, :] = tail
    return y, tail[5:8, :]


def _rglru_gates(xc, wr_ref, br_ref, wi_ref, bi_ref, lam_ref):
    a_parts, u_parts = [], []
    for n in range(RG_BLOCKS):
        sl = slice(n * RG_BW, (n + 1) * RG_BW)
        xb = xc[:, sl]
        xb16 = xb.astype(BF16)
        r = _sigmoid(_dot(xb16, wr_ref[n]) + br_ref[:, sl])
        i = _sigmoid(_dot(xb16, wi_ref[n]) + bi_ref[:, sl])
        log_a = -RG_C * r * _softplus(-lam_ref[:, sl])
        a = jnp.exp(log_a)
        one_m_a2 = -jnp.tanh(log_a) * (a * a + 1.0)
        a_parts.append(a)
        u_parts.append(jnp.sqrt(jnp.maximum(one_m_a2, 0.0)) * (i * xb))
    return a_parts, u_parts


def _rglru_prompt_kernel(x_ref, g_ref, cw_ref, cb_ref, wr_ref, br_ref, wi_ref, bi_ref, lam_ref,
                         y_ref, h_ref, cs_ref, buf, h_scr):
    c = pl.program_id(1)
    first = c == 0

    @pl.when(first)
    def _():
        h_scr[...] = jnp.zeros_like(h_scr)

    xc, tail = _chunk_conv(x_ref, buf, cw_ref, cb_ref, first)
    cs_ref[0] = tail
    a_parts, u_parts = _rglru_gates(xc, wr_ref, br_ref, wi_ref, bi_ref, lam_ref)
    n_rows = xc.shape[0]
    row = lax.broadcasted_iota(jnp.int32, (n_rows, RG_BW), 0)
    for n in range(RG_BLOCKS):
        sl = slice(n * RG_BW, (n + 1) * RG_BW)
        a, u = a_parts[n], u_parts[n]
        s = 1
        while s < n_rows:
            keep = row >= s
            a_sh = jnp.where(keep, pltpu.roll(a, s, 0), 1.0)
            u_sh = jnp.where(keep, pltpu.roll(u, s, 0), 0.0)
            u = a * u_sh + u
            a = a * a_sh
            s *= 2
        h = a * h_scr[:, sl] + u
        h_scr[:, sl] = h[n_rows - 1:n_rows, :]
        y_ref[:, sl] = (h * _silu(g_ref[:, sl])).astype(BF16)
    h_ref[0] = h_scr[...]


def _rglru_prompt(proj, cw, cb, wr, br, wi, bi, lam, bsz, seq):
    nc = seq // CHUNK
    blk = lambda j: pl.BlockSpec((CHUNK, BRANCH), lambda b, c, j=j: (b * nc + c, j))
    const = lambda shape: pl.BlockSpec(shape, lambda b, c: (0,) * len(shape))
    return pl.pallas_call(
        _rglru_prompt_kernel,
        grid=(bsz, nc),
        in_specs=[blk(BLK_RG_X), blk(BLK_RG_G), const((CONV_W, BRANCH)), const((1, BRANCH)),
                  const((RG_BLOCKS, RG_BW, RG_BW)), const((1, BRANCH)),
                  const((RG_BLOCKS, RG_BW, RG_BW)), const((1, BRANCH)), const((1, BRANCH))],
        out_specs=[pl.BlockSpec((CHUNK, BRANCH), lambda b, c: (b * nc + c, 0)),
                   pl.BlockSpec((1, 1, BRANCH), lambda b, c: (b, 0, 0)),
                   pl.BlockSpec((1, CONV_W - 1, BRANCH), lambda b, c: (b, 0, 0))],
        out_shape=[jax.ShapeDtypeStruct((bsz * seq, BRANCH), BF16),
                   jax.ShapeDtypeStruct((bsz, 1, BRANCH), F32),
                   jax.ShapeDtypeStruct((bsz, CONV_W - 1, BRANCH), F32)],
        scratch_shapes=[pltpu.VMEM((8 + CHUNK, BRANCH), F32), pltpu.VMEM((1, BRANCH), F32)],
        compiler_params=_cparams("parallel", "arbitrary"),
        name="rglru_prompt",
    )(proj, proj, cw, cb, wr, br, wi, bi, lam)


def _ssd_prompt_kernel(z_ref, x_ref, bc_ref, sm_ref, cwx_ref, cbx_ref, cwb_ref, cbb_ref,
                       dtb_ref, a_ref, d_ref, nw_ref, tri_ref, exp_ref,
                       y_ref, s_ref, cs_ref, st_scr, xbuf, bcbuf):
    c = pl.program_id(1)
    first = c == 0
    n_rows = x_ref.shape[0]
    gw = BRANCH // SSD_G

    @pl.when(first)
    def _():
        st_scr[...] = jnp.zeros_like(st_scr)

    xc, xtail = _chunk_conv(x_ref, xbuf, cwx_ref, cbx_ref, first)
    bcc, bctail = _chunk_conv(bc_ref, bcbuf, cwb_ref, cbb_ref, first)
    cs_ref[0, :, 0:BRANCH] = xtail
    cs_ref[0, :, BRANCH:SSD_CONV_DIM] = bctail
    xs = _silu(xc)
    bcs = _silu(bcc)

    tri = tri_ref[...]
    expand = exp_ref[...]
    dt = _softplus(sm_ref[...] + dtb_ref[...])
    cum = _sel_left(tri, dt * a_ref[...])
    cum_t = cum.T
    dt_x = _sel_right(dt, expand)
    cum_x = _sel_right(cum, expand)
    cum_last = cum_x[n_rows - 1:n_rows, :]
    dec_in = jnp.exp(cum_x)
    xdt = xs * dt_x
    xw = (xdt * jnp.exp(cum_last - cum_x)).astype(BF16)
    xdt16 = xdt.astype(BF16)

    t_idx = lax.broadcasted_iota(jnp.int32, (n_rows, n_rows), 0)
    s_idx = lax.broadcasted_iota(jnp.int32, (n_rows, n_rows), 1)
    causal = s_idx <= t_idx
    lane = lax.broadcasted_iota(jnp.int32, (n_rows, 2 * SSD_P), 1)

    y_parts = []
    for g in range(SSD_G):
        b_g = bcs[:, g * SSD_N:(g + 1) * SSD_N].astype(BF16)
        c_g = bcs[:, (SSD_G + g) * SSD_N:(SSD_G + g + 1) * SSD_N].astype(BF16)
        cb = _dot_nt(c_g, b_g)
        st_g = st_scr[:, g * gw:(g + 1) * gw]
        y_inter = _dot(c_g, st_g.astype(BF16)) * dec_in[:, g * gw:(g + 1) * gw]
        heads_per_g = SSD_HEADS // SSD_G
        for pair in range(heads_per_g // 2):
            h0 = g * heads_per_g + 2 * pair
            xp = xdt16[:, h0 * SSD_P:(h0 + 2) * SSD_P]
            outs = []
            for h in (h0, h0 + 1):
                col = DT_LANE + h
                seg = cum[:, col:col + 1] - cum_t[col:col + 1, :]
                lmat = jnp.where(causal, jnp.exp(jnp.where(causal, seg, 0.0)), 0.0)
                outs.append(_dot((cb * lmat).astype(BF16), xp))
            y_parts.append(jnp.where(lane < SSD_P, outs[0], outs[1]))
        y_parts.append(y_inter)
        st_scr[:, g * gw:(g + 1) * gw] = (
            st_g * jnp.exp(cum_last[:, g * gw:(g + 1) * gw])
            + _dot(bcs[:, g * SSD_N:(g + 1) * SSD_N].T.astype(BF16), xw[:, g * gw:(g + 1) * gw]))
    npair = SSD_HEADS // SSD_G // 2
    y = jnp.concatenate(
        [jnp.concatenate(y_parts[g * (npair + 1):g * (npair + 1) + npair], axis=1)
         + y_parts[g * (npair + 1) + npair] for g in range(SSD_G)], axis=1)
    y = (y + xs * d_ref[...]) * _silu(z_ref[...])
    y_ref[...] = _group_norm(y, nw_ref[...], gw).astype(BF16)

    @pl.when(c == pl.num_programs(1) - 1)
    def _():
        s_ref[0] = st_scr[...].T.reshape(SSD_HEADS, SSD_P, SSD_N)


def _ssd_expand_const(width):
    e = np.zeros((128, SSD_HEADS * width), np.float32)
    for h in range(SSD_HEADS):
        e[DT_LANE + h, h * width:(h + 1) * width] = 1.0
    return jnp.asarray(e, dtype=BF16)


def _ssd_prompt(proj, cwx, cbx, cwb, cbb, dtb, a_pad, d_x, nw, bsz, seq):
    nc = seq // CHUNK
    blk = lambda w, j: pl.BlockSpec((CHUNK, w), lambda b, c, j=j: (b * nc + c, j))
    const = lambda shape: pl.BlockSpec(shape, lambda b, c: (0,) * len(shape))
    return pl.pallas_call(
        _ssd_prompt_kernel,
        grid=(bsz, nc),
        in_specs=[blk(BRANCH, BLK_SS_Z), blk(BRANCH, BLK_SS_X), blk(SSD_BC, BLK_SS_BC),
                  blk(128, BLK_SMALL),
                  const((CONV_W, BRANCH)), const((1, BRANCH)), const((CONV_W, SSD_BC)),
                  const((1, SSD_BC)), const((1, 128)), const((1, 128)), const((1, BRANCH)),
                  const((1, BRANCH)), const((CHUNK, CHUNK)), const((128, BRANCH))],
        out_specs=[pl.BlockSpec((CHUNK, BRANCH), lambda b, c: (b * nc + c, 0)),
                   pl.BlockSpec((1, SSD_HEADS, SSD_P, SSD_N), lambda b, c: (b, 0, 0, 0)),
                   pl.BlockSpec((1, CONV_W - 1, SSD_CONV_DIM), lambda b, c: (b, 0, 0))],
        out_shape=[jax.ShapeDtypeStruct((bsz * seq, BRANCH), BF16),
                   jax.ShapeDtypeStruct((bsz, SSD_HEADS, SSD_P, SSD_N), F32),
                   jax.ShapeDtypeStruct((bsz, CONV_W - 1, SSD_CONV_DIM), F32)],
        scratch_shapes=[pltpu.VMEM((SSD_N, BRANCH), F32),
                        pltpu.VMEM((8 + CHUNK, BRANCH), F32),
                        pltpu.VMEM((8 + CHUNK, SSD_BC), F32)],
        compiler_params=_cparams("parallel", "arbitrary"),
        name="ssd_prompt",
    )(proj, proj, proj, proj, cwx, cbx, cwb, cbb, dtb, a_pad, d_x, nw,
      _tri_const(CHUNK), _ssd_expand_const(SSD_P))


def _step_conv(x, cs_ref, w_ref, b_ref, ncs_ref):
    y = b_ref[...] + x * w_ref[CONV_W - 1:CONV_W, :]
    for j in range(CONV_W - 1):
        y = y + cs_ref[j] * w_ref[j:j + 1, :]
    for j in range(CONV_W - 2):
        ncs_ref[j] = cs_ref[j + 1]
    ncs_ref[CONV_W - 2] = x
    return y


def _sample_pre_kernel(p_ref, lb_ref, rcs_ref, rh_ref, rcw_ref, rcb_ref, wr_ref, br_ref, wi_ref,
                       bi_ref, lam_ref, wup_ref, bup_ref, scs_ref, scw_ref, scb_ref, dtb_ref,
                       a_ref, exp_ref, expw_ref,
                       hq_ref, hk_ref, gq_ref, gd_ref, yrg_ref, nrh_ref, nrcs_ref,
                       sx_ref, sbc_ref, sdx_ref, sda_ref, nscs_ref):
    col = lambda blk, w: slice(blk * w, (blk + 1) * w)
    f = p_ref[:, col(BLK_HG_F, BRANCH)]
    lb = lb_ref[...]
    hq_ref[...] = _silu(p_ref[:, col(BLK_HG_Q, BRANCH)])
    hk_ref[...] = (1.0 - lb) * (1.0 - _sigmoid(f))
    sm = p_ref[:, col(BLK_SMALL, 128)]
    up = _dot(sm.astype(BF16), wup_ref[...]) + bup_ref[...]
    gq_ref[...] = p_ref[:, col(BLK_GL_Q, GLA_KDIM)] * (GLA_DK ** -0.5)
    gd_ref[...] = jnp.exp(-_softplus(-up) * (1.0 / GLA_TAU))
    xc = _step_conv(p_ref[:, col(BLK_RG_X, BRANCH)], rcs_ref, rcw_ref, rcb_ref, nrcs_ref)
    a_parts, u_parts = _rglru_gates(xc, wr_ref, br_ref, wi_ref, bi_ref, lam_ref)
    h = jnp.concatenate(a_parts, axis=1) * rh_ref[...] + jnp.concatenate(u_parts, axis=1)
    nrh_ref[...] = h
    yrg_ref[...] = (h * _silu(p_ref[:, col(BLK_RG_G, BRANCH)])).astype(BF16)
    xbc = jnp.concatenate([p_ref[:, col(BLK_SS_X, BRANCH)], p_ref[:, col(BLK_SS_BC, SSD_BC)]], axis=1)
    xbc = _silu(_step_conv(xbc, scs_ref, scw_ref, scb_ref, nscs_ref))
    xs = xbc[:, 0:BRANCH]
    sx_ref[...] = xs
    sbc_ref[...] = xbc[:, BRANCH:SSD_CONV_DIM]
    dt = _softplus(sm + dtb_ref[...])
    expand = exp_ref[...]
    sdx_ref[...] = xs * _sel_right(dt, expand)
    sda_ref[...] = jnp.exp(_sel_right(dt * a_ref[...], expw_ref[...]))


def _sample_pre(proj, lb, rcs, rh, rcw, rcb, wr, br, wi, bi, lam, wup, bup, scs, scw, scb,
                dtb, a_pad):
    nb = proj.shape[0]
    sd = lambda *shape, dt=F32: jax.ShapeDtypeStruct(shape, dt)
    return pl.pallas_call(
        _sample_pre_kernel,
        out_shape=[sd(nb, BRANCH), sd(nb, BRANCH),
                   sd(nb, GLA_KDIM), sd(nb, GLA_KDIM),
                   sd(nb, BRANCH, dt=BF16), sd(nb, BRANCH), sd(CONV_W - 1, nb, BRANCH),
                   sd(nb, BRANCH), sd(nb, SSD_BC), sd(nb, BRANCH), sd(nb, SSD_HEADS * SSD_N),
                   sd(CONV_W - 1, nb, SSD_CONV_DIM)],
        compiler_params=pltpu.CompilerParams(vmem_limit_bytes=VMEM_LIMIT),
        name="sample_pre",
    )(proj, lb, rcs, rh, rcw, rcb, wr, br, wi, bi, lam, wup, bup, scs, scw, scb, dtb, a_pad,
      _ssd_expand_const(SSD_P), _ssd_expand_const(SSD_N))


def _pad_t(x):
    r = x.shape[0]
    return jnp.concatenate([x, jnp.zeros((128 - r, 128), F32)], axis=0).T


STATE_UNROLL = 4


def _gla_state_kernel(*refs, heads, tied):
    so_ref, o_ref = refs[-2:]
    if tied:
        k_ref, v_ref, q_ref, s_ref = refs[:4]
    else:
        d_ref, k_ref, v_ref, q_ref, s_ref = refs[:5]
    dv = s_ref.shape[-1]

    def body(b, carry):
        kt_ = _pad_t(k_ref[b])
        dt_ = None if tied else _pad_t(d_ref[b])
        vr = v_ref[b]
        qr = q_ref[b]
        for h in range(heads):
            kb = jnp.broadcast_to(kt_[:, h:h + 1], (kt_.shape[0], dv))
            d = jnp.maximum(1.0 - kb, TINY) if tied else dt_[:, h:h + 1]
            s_new = d * s_ref[b, h] + kb * vr[h:h + 1, :]
            so_ref[b, h] = s_new
            q8 = jnp.broadcast_to(qr[h:h + 1, :], (8, qr.shape[1])).astype(BF16)
            o_ref[b, h:h + 1, :] = _dot(q8, s_new.astype(BF16))[0:1, :]
        return carry
    lax.fori_loop(0, s_ref.shape[0], body, 0, unroll=STATE_UNROLL)


def _state_call(kern, name, vec_args, vec_specs, s_all, so_prev, layer, o_shape, o_spec, bb):
    nb = s_all.shape[1]
    st = pl.BlockSpec((None, bb) + s_all.shape[2:], lambda i: (layer, i, 0, 0, 0))
    in_specs = list(vec_specs) + [st]
    args = list(vec_args) + [s_all]
    aliases = {}
    if so_prev is not None:
        in_specs.append(pl.BlockSpec(memory_space=pl.ANY))
        args.append(so_prev)
        aliases = {len(args) - 1: 0}
    return pl.pallas_call(
        kern,
        grid=(nb // bb,),
        in_specs=in_specs,
        out_specs=[st, o_spec],
        out_shape=[jax.ShapeDtypeStruct(s_all.shape, F32), o_shape],
        input_output_aliases=aliases,
        compiler_params=_cparams("parallel"),
        name=name,
    )(*args)


def _gla_state(d, k, v, q, s_all, so_prev, layer, *, bb):
    _, nb, heads, dk, dv = s_all.shape
    vec = lambda w: pl.BlockSpec((bb, heads, w), lambda i: (i, 0, 0))
    tied = d is None
    args = (k, v, q) if tied else (d, k, v, q)
    specs = (vec(dk), vec(dv), vec(dk)) if tied else (vec(dk), vec(dk), vec(dv), vec(dk))
    return _state_call(functools.partial(_gla_state_kernel, heads=heads, tied=tied),
                       "hgrn_state" if tied else "gla_state", args, specs, s_all, so_prev, layer,
                       jax.ShapeDtypeStruct((nb, heads, dv), F32), vec(dv), bb)


def _ssd_state_kernel(da_ref, dx_ref, b_ref, c_ref, s_ref, *rest):
    so_ref, y_ref = rest[-2:]
    hpg = SSD_HEADS // SSD_G

    def body(b, carry):
        ar = da_ref[b]
        xt_ = _pad_t(dx_ref[b])
        br = b_ref[b]
        cr = c_ref[b]
        for h in range(SSD_HEADS):
            g = h // hpg
            rows = slice((h % 2) * SSD_P, (h % 2 + 1) * SSD_P)
            j = h // 2
            s_new = ar[h:h + 1, :] * s_ref[b, h] + xt_[rows, j:j + 1] * br[g:g + 1, :]
            so_ref[b, h] = s_new
            c8 = jnp.broadcast_to(cr[g:g + 1, :], (8, SSD_N)).astype(BF16)
            y_ref[b, h:h + 1, :] = _dot_nt(c8, s_new.astype(BF16))[0:1, :]
        return carry
    lax.fori_loop(0, s_ref.shape[0], body, 0, unroll=STATE_UNROLL)


def _ssd_state(da, dx, bv, cv, s_all, so_prev, layer, *, bb):
    nb = s_all.shape[1]
    vec = lambda r, w: pl.BlockSpec((bb, r, w), lambda i: (i, 0, 0))
    return _state_call(_ssd_state_kernel, "ssd_state", (da, dx, bv, cv),
                       (vec(SSD_HEADS, SSD_N), vec(8, 128), vec(SSD_G, SSD_N), vec(SSD_G, SSD_N)),
                       s_all, so_prev, layer,
                       jax.ShapeDtypeStruct((nb, SSD_HEADS, SSD_P), F32), vec(SSD_HEADS, SSD_P), bb)


def _sample_post_kernel(p_ref, ohg_ref, ogl_ref, yss_ref, sx_ref, d_ref, hnw_ref, gnw_ref,
                        snw_ref, yhg_ref, ygl_ref, yso_ref):
    col = lambda blk: slice(blk * BRANCH, (blk + 1) * BRANCH)
    yhg_ref[...] = (_group_norm(ohg_ref[...], hnw_ref[...], HG_DV)
                    * _silu(p_ref[:, col(BLK_HG_G)])).astype(BF16)
    ygl_ref[...] = (_group_norm(ogl_ref[...], gnw_ref[...], GLA_DV)
                    * _silu(p_ref[:, col(BLK_GL_G)])).astype(BF16)
    y = (yss_ref[...] + sx_ref[...] * d_ref[...]) * _silu(p_ref[:, col(BLK_SS_Z)])
    yso_ref[...] = _group_norm(y, snw_ref[...], BRANCH // SSD_G).astype(BF16)


def _sample_post(proj, ohg, ogl, yss, sx, d_x, hnw, gnw, snw):
    nb = proj.shape[0]
    out = jax.ShapeDtypeStruct((nb, BRANCH), BF16)
    return pl.pallas_call(
        _sample_post_kernel, out_shape=[out, out, out],
        compiler_params=pltpu.CompilerParams(vmem_limit_bytes=VMEM_LIMIT),
        name="sample_post",
    )(proj, ohg, ogl, yss, sx, d_x, hnw, gnw, snw)


def _tail_kernel(w_ref, o_ref):
    rows = w_ref.shape[0]
    n_a = ORIG_SSD_Z - ORIG_GLA_A
    n_zx = ORIG_SSD_DT - ORIG_SSD_Z
    n_dt = N_IN - ORIG_SSD_DT
    x = w_ref[...]
    o_ref[:, 0:n_zx] = x[:, n_a:n_a + n_zx].astype(BF16)
    small = jnp.concatenate([x[:, 0:n_a], x[:, n_a + n_zx:n_a + n_zx + n_dt],
                             jnp.zeros((rows, 128 - n_a - n_dt), F32)], axis=1)
    o_ref[:, n_zx:n_zx + 128] = small.astype(BF16)
    o_ref[:, n_zx + 128:N_TAIL] = jnp.zeros((rows, N_TAIL - n_zx - 128), BF16)


def _prep_w_tail(w_in, *, rb=256):
    depth, d, _ = w_in.shape
    return pl.pallas_call(
        _tail_kernel,
        grid=(depth, d // rb),
        in_specs=[pl.BlockSpec((None, rb, N_TAIL), lambda l, i: (l, i, N_MAIN // N_TAIL))],
        out_specs=pl.BlockSpec((None, rb, N_TAIL), lambda l, i: (l, i, 0)),
        out_shape=jax.ShapeDtypeStruct((depth, d, N_TAIL), BF16),
        compiler_params=_cparams("parallel", "parallel"),
        name="w_tail",
    )(w_in)


def _pad_lanes(v, start, width=128):
    out = jnp.zeros((v.shape[0], 1, width), F32)
    return out.at[:, 0, start:start + v.shape[1]].set(v.astype(F32))


def kernel(x_prompt, x_sample, state_hgrn, state_rglru, state_rglru_conv, state_gla, state_ssd, state_ssd_conv, rms_in, w_in, hgrn_lower_bounds, hgrn_norm, rglru_conv_w, rglru_conv_b, rglru_w_r, rglru_b_r, rglru_w_i, rglru_b_i, rglru_lambda, gla_w_up, gla_b_up, gla_norm, ssd_conv_w, ssd_conv_b, ssd_dt_bias, ssd_a_log, ssd_d, ssd_norm, w_out, rms_final):
    bsz, seq, _ = x_prompt.shape
    nb = x_sample.shape[0]
    row = lambda v: v.reshape(DEPTH, 1, -1).astype(F32)

    lb_all = _lower_bounds(hgrn_lower_bounds.astype(F32)).reshape(DEPTH, 1, BRANCH)
    w_in = w_in.astype(F32)
    w_tail = _prep_w_tail(w_in)
    w_out16 = w_out.astype(BF16)
    wr16 = rglru_w_r.astype(BF16)
    wi16 = rglru_w_i.astype(BF16)
    wup16 = jnp.concatenate(
        [gla_w_up, jnp.zeros((DEPTH, 128 - GLA_RANK, GLA_KDIM), gla_w_up.dtype)], axis=1).astype(BF16)
    dtb = _pad_lanes(ssd_dt_bias, DT_LANE)
    a_pad = _pad_lanes(-jnp.exp(ssd_a_log.astype(F32)), DT_LANE)
    d_x = jnp.repeat(ssd_d.astype(F32), SSD_P, axis=-1).reshape(DEPTH, 1, BRANCH)
    rms_in_r, hnw, gnw, snw = row(rms_in), row(hgrn_norm), row(gla_norm), row(ssd_norm)
    rcb, br, bi, lam, bup = (row(rglru_conv_b), row(rglru_b_r), row(rglru_b_i),
                             row(rglru_lambda), row(gla_b_up))
    scb = row(ssd_conv_b)

    xp = x_prompt.reshape(bsz * seq, D_MODEL)
    xs = x_sample.reshape(nb, D_MODEL)
    tm_in = 1024 if (bsz * seq) % 1024 == 0 else CHUNK
    tm_out = 512 if (bsz * seq) % 512 == 0 else CHUNK
    bb = 8 if nb % 8 == 0 else 1
    rf = rms_final.reshape(1, D_MODEL).astype(F32)

    hp = _rmsnorm(xp, rms_in_r[0], BF16, tm=tm_out)
    hs = _rmsnorm(xs, rms_in_r[0], BF16, tm=nb)
    outs_p = [[] for _ in range(6)]
    outs_s = [[] for _ in range(3)]
    ns_hg = ns_gl = ns_ss = None
    for l in range(DEPTH):
        scw = ssd_conv_w[l].astype(F32)
        next_w, next_dt = (rms_in_r[l + 1], BF16) if l + 1 < DEPTH else (rf, F32)
        proj = _inproj(hp, w_in, w_tail, l, tm=tm_in, tn=1024)
        y_hg, s_hg = _hgrn_prompt(proj, lb_all[l], hnw[l], bsz, seq)
        y_rg, s_rg, s_rgc = _rglru_prompt(proj, rglru_conv_w[l].astype(F32), rcb[l], wr16[l], br[l],
                                          wi16[l], bi[l], lam[l], bsz, seq)
        y_gl, s_gl = _gla_prompt(proj, wup16[l], bup[l], gnw[l], bsz, seq)
        y_ss, s_ss, s_ssc = _ssd_prompt(proj, scw[:, :BRANCH], scb[l][:, :BRANCH], scw[:, BRANCH:],
                                        scb[l][:, BRANCH:], dtb[l], a_pad[l], d_x[l], snw[l], bsz, seq)
        xp, hp = _outproj((y_hg, y_rg, y_gl, y_ss), w_out16[l], xp, next_w, next_dt,
                          tm=tm_out if next_dt == BF16 else tm_out // 2)
        for lst, s in zip(outs_p, (s_hg, s_rg.reshape(bsz, BRANCH), s_rgc, s_gl, s_ss, s_ssc)):
            lst.append(s)

        proj_s = _inproj(hs, w_in, w_tail, l, tm=nb, tn=1024)
        (hq, hk, gq, gd, yrg_s, nrh, nrcs, sx, sbc, sdx, sda, nscs) = _sample_pre(
            proj_s, lb_all[l], jnp.swapaxes(state_rglru_conv[l], 0, 1), state_rglru[l],
            rglru_conv_w[l].astype(F32), rcb[l], wr16[l], br[l], wi16[l], bi[l], lam[l],
            wup16[l], bup[l], jnp.swapaxes(state_ssd_conv[l], 0, 1), scw, scb[l], dtb[l], a_pad[l])
        hv = proj_s[:, BLK_HG_I * BRANCH:(BLK_HG_I + 1) * BRANCH]
        gk = proj_s[:, BLK_GL_K * GLA_KDIM:(BLK_GL_K + 1) * GLA_KDIM]
        gv = proj_s[:, BLK_GL_V * BRANCH:(BLK_GL_V + 1) * BRANCH]
        hsh = lambda a: a.reshape(nb, HG_HEADS, -1)
        gsh = lambda a: a.reshape(nb, GLA_HEADS, -1)
        ns_hg, o_hg = _gla_state(None, hsh(hk), hsh(hv), hsh(hq), state_hgrn, ns_hg, l, bb=bb)
        ns_gl, o_gl = _gla_state(gsh(gd), gsh(gk), gsh(gv), gsh(gq), state_gla, ns_gl, l, bb=bb)
        ns_ss, y_ssr = _ssd_state(sda.reshape(nb, SSD_HEADS, SSD_N), sdx.reshape(nb, 8, 128),
                                  sbc[:, :SSD_G * SSD_N].reshape(nb, SSD_G, SSD_N),
                                  sbc[:, SSD_G * SSD_N:].reshape(nb, SSD_G, SSD_N),
                                  state_ssd, ns_ss, l, bb=bb)
        yhg_s, ygl_s, yss_s = _sample_post(proj_s, o_hg.reshape(nb, BRANCH), o_gl.reshape(nb, BRANCH),
                                           y_ssr.reshape(nb, BRANCH), sx, d_x[l], hnw[l], gnw[l], snw[l])
        xs, hs = _outproj((yhg_s, yrg_s, ygl_s, yss_s), w_out16[l], xs, next_w, next_dt, tm=nb)
        for lst, s in zip(outs_s, (nrh, jnp.swapaxes(nrcs, 0, 1), jnp.swapaxes(nscs, 0, 1))):
            lst.append(s)

    y_prompt = hp.reshape(bsz, seq, D_MODEL)
    y_sample = hs.reshape(nb, 1, D_MODEL)
    s_rg, s_rgc, s_ssc = (jnp.stack(l) for l in outs_s)
    return ((y_prompt, y_sample) + tuple(jnp.stack(l) for l in outs_p)
            + (ns_hg, s_rg, s_rgc, ns_gl, ns_ss, s_ssc))
```

```python
import functools
import math

import numpy as np
import jax
import jax.numpy as jnp
from jax import lax
from jax.experimental import pallas as pl
from jax.experimental.pallas import tpu as pltpu

F32 = jnp.float32
BF16 = jnp.bfloat16

D_MODEL = 2048
DEPTH = 4
BRANCH = 1024
D_MIX = 4 * BRANCH
CONV_W = 4
EPS = 1e-6
TINY = 1e-30

HG_HEADS, HG_DK, HG_DV = 8, 128, 128
RG_BLOCKS, RG_BW, RG_C = 8, 128, 8.0
GLA_HEADS, GLA_DK, GLA_DV, GLA_RANK, GLA_TAU = 4, 128, 256, 16, 16.0
GLA_KDIM = GLA_HEADS * GLA_DK
SSD_HEADS, SSD_P, SSD_G, SSD_N = 16, 64, 2, 128
SSD_BC = 2 * SSD_G * SSD_N
SSD_CONV_DIM = BRANCH + SSD_BC

ORIG_GLA_A = 9216
ORIG_SSD_Z = 9232
ORIG_SSD_DT = 11792
N_IN = 11808
N_PROJ = 12288
COL_SMALL = 11776
DT_LANE = 16
BLK_HG_Q, BLK_HG_F, BLK_HG_I, BLK_HG_G = 0, 1, 2, 3
BLK_RG_X, BLK_RG_G = 4, 5
BLK_GL_Q, BLK_GL_K = 12, 13
BLK_GL_V, BLK_GL_G = 7, 8
BLK_SS_Z, BLK_SS_X = 9, 10
BLK_SS_BC = 22
BLK_SMALL = COL_SMALL // 128

CHUNK = 128
SUB = 16
VMEM_LIMIT = 52 * 1024 * 1024


def _cparams(*sem):
    return pltpu.CompilerParams(dimension_semantics=sem, vmem_limit_bytes=VMEM_LIMIT)


def _sigmoid(x):
    return 0.5 * jnp.tanh(0.5 * x) + 0.5


def _silu(x):
    return x * _sigmoid(x)


def _softplus(x):
    return jnp.maximum(x, 0.0) + jnp.log(1.0 + jnp.exp(-jnp.abs(x)))


def _dot(a, b):
    return jnp.dot(a, b, preferred_element_type=F32)


def _dot_nt(a, b):
    return lax.dot_general(a, b, (((1,), (1,)), ((), ())), preferred_element_type=F32)


def _split3(a):
    a0 = a.astype(BF16)
    r1 = a - a0.astype(F32)
    a1 = r1.astype(BF16)
    a2 = (r1 - a1.astype(F32)).astype(BF16)
    return a0, a1, a2


def _sel_left(sel3, x):
    return _dot(sel3, jnp.concatenate(_split3(x), axis=0))


def _sel_right(x, sel3):
    return _dot(jnp.concatenate(_split3(x), axis=1), sel3)


def _group_norm(y, w, width):
    parts = []
    for g in range(y.shape[1] // width):
        yg = y[:, g * width:(g + 1) * width]
        ms = jnp.mean(yg * yg, axis=-1, keepdims=True)
        parts.append(yg * lax.rsqrt(ms + EPS))
    out = parts[0] if len(parts) == 1 else jnp.concatenate(parts, axis=1)
    return out * w


def _tri_const(c):
    return jnp.asarray(np.tile(np.tril(np.ones((c, c), np.float32)), (1, 3)), dtype=BF16)


def _level_const(c):
    t = np.arange(c)[:, None]
    s = np.arange(c)[None, :]
    lvl = np.zeros((c, c), np.int32)
    lvl[(t // SUB == s // SUB) & (s <= t)] = 1
    h, code = SUB, 2
    while h < c:
        m = (t // (2 * h) == s // (2 * h)) & (t % (2 * h) >= h) & (s % (2 * h) < h)
        lvl[m] = code
        h *= 2
        code += 1
    return jnp.asarray(lvl)


def _lb_kernel(p_ref, o_ref):
    x = p_ref[...]
    m = jnp.max(x, axis=0, keepdims=True)
    e = jnp.exp(x - m)
    p = e / jnp.sum(e, axis=0, keepdims=True)
    acc = jnp.zeros_like(p[0:1])
    rows = [acc]
    for l in range(1, DEPTH):
        acc = acc + p[l:l + 1]
        rows.append(acc)
    o_ref[...] = jnp.concatenate(rows, axis=0)


def _lower_bounds(param):
    return pl.pallas_call(
        _lb_kernel, out_shape=jax.ShapeDtypeStruct(param.shape, F32), name="hgrn_lb")(param)


N_MAIN = ORIG_GLA_A
N_TAIL = N_PROJ - N_MAIN


def _inproj_kernel(h_ref, w_ref, wt_ref, o_ref, wb_scr, *, n_main, rb):
    j = pl.program_id(0)

    @pl.when(pl.program_id(1) == 0)
    def _():
        @pl.when(j < n_main)
        def _():
            def body(i, carry):
                r = pl.multiple_of(i * rb, rb)
                wb_scr[pl.ds(r, rb), :] = w_ref[pl.ds(r, rb), :].astype(BF16)
                return carry
            lax.fori_loop(0, wb_scr.shape[0] // rb, body, 0)

        @pl.when(j >= n_main)
        def _():
            wb_scr[...] = wt_ref[...]

    o_ref[...] = _dot_nt(h_ref[...], wb_scr[...])


def _inproj(h, w_in_t, w_tail_t, layer, *, tm, tn):
    m, d = h.shape
    n_main = N_MAIN // tn
    return pl.pallas_call(
        functools.partial(_inproj_kernel, n_main=n_main, rb=128),
        grid=(N_PROJ // tn, m // tm),
        in_specs=[pl.BlockSpec((tm, d), lambda j, i: (i, 0)),
                  pl.BlockSpec((None, tn, d), lambda j, i: (layer, jnp.minimum(j, n_main - 1), 0)),
                  pl.BlockSpec((None, tn, d), lambda j, i: (layer, jnp.maximum(j - n_main, 0), 0))],
        out_specs=pl.BlockSpec((tm, tn), lambda j, i: (i, j)),
        out_shape=jax.ShapeDtypeStruct((m, N_PROJ), F32),
        scratch_shapes=[pltpu.VMEM((tn, d), BF16)],
        compiler_params=_cparams("arbitrary", "arbitrary"),
        name="inproj",
    )(h, w_in_t, w_tail_t)


def _outproj_kernel(y0_ref, y1_ref, y2_ref, y3_ref, w_ref, x_ref, nw_ref, xo_ref, ho_ref, *, rb):
    acc = x_ref[...]
    for g, y_ref in enumerate((y0_ref, y1_ref, y2_ref, y3_ref)):
        acc = acc + _dot(y_ref[...], w_ref[g * BRANCH:(g + 1) * BRANCH, :])
    xo_ref[...] = acc

    def body(i, carry):
        r = pl.multiple_of(i * rb, rb)
        x = xo_ref[pl.ds(r, rb), :]
        ms = jnp.mean(x * x, axis=-1, keepdims=True)
        ho_ref[pl.ds(r, rb), :] = (x * lax.rsqrt(ms + EPS) * nw_ref[...]).astype(ho_ref.dtype)
        return carry
    lax.fori_loop(0, xo_ref.shape[0] // rb, body, 0)


def _outproj(ys, w, x, norm_w, norm_dtype, *, tm):
    m, d = x.shape
    yspec = pl.BlockSpec((tm, BRANCH), lambda i: (i, 0))
    xspec = pl.BlockSpec((tm, d), lambda i: (i, 0))
    return pl.pallas_call(
        functools.partial(_outproj_kernel, rb=min(tm, 64)),
        grid=(m // tm,),
        in_specs=[yspec, yspec, yspec, yspec,
                  pl.BlockSpec((D_MIX, d), lambda i: (0, 0), pipeline_mode=pl.Buffered(1)),
                  xspec, pl.BlockSpec((1, d), lambda i: (0, 0))],
        out_specs=[xspec, xspec],
        out_shape=[jax.ShapeDtypeStruct((m, d), F32), jax.ShapeDtypeStruct((m, d), norm_dtype)],
        compiler_params=_cparams("parallel"),
        name="outproj",
    )(*ys, w, x, norm_w)


def _rmsnorm_kernel(x_ref, w_ref, o_ref):
    x = x_ref[...]
    ms = jnp.mean(x * x, axis=-1, keepdims=True)
    o_ref[...] = (x * lax.rsqrt(ms + EPS) * w_ref[...]).astype(o_ref.dtype)


def _rmsnorm(x, w, out_dtype, *, tm):
    m, d = x.shape
    return pl.pallas_call(
        _rmsnorm_kernel,
        grid=(m // tm,),
        in_specs=[pl.BlockSpec((tm, d), lambda i: (i, 0)),
                  pl.BlockSpec((1, d), lambda i: (0, 0))],
        out_specs=pl.BlockSpec((tm, d), lambda i: (i, 0)),
        out_shape=jax.ShapeDtypeStruct((m, d), out_dtype),
        compiler_params=_cparams("parallel"),
        name="rmsnorm",
    )(x, w)


def _gla_chunk_heads(qs, ks, get_v, logfs, get_st, tri, lvl):
    n = len(qs)
    c, width = qs[0].shape
    bs = [_sel_left(tri, lf) for lf in logfs]

    def ref_rows(b, rows, span):
        return jnp.concatenate(
            [jnp.broadcast_to(b[r:r + 1, :], (span, width)) for r in rows], axis=0)

    s_diag = []
    for h in range(n):
        ed = bs[h] - ref_rows(bs[h], [SUB * m + SUB // 2 for m in range(c // SUB)], SUB)
        s_diag.append(_dot_nt((qs[h] * jnp.exp(ed)).astype(BF16), (ks[h] * jnp.exp(-ed)).astype(BF16)))
    s_levels = []
    half = SUB
    while half < c:
        cur = []
        for h in range(n):
            ref = ref_rows(bs[h], [2 * half * m + half - 1 for m in range(c // (2 * half))], 2 * half)
            e = jnp.exp(-jnp.abs(bs[h] - ref))
            cur.append(_dot_nt((qs[h] * e).astype(BF16), (ks[h] * e).astype(BF16)))
        s_levels.append(cur)
        half *= 2

    outs, new_sts = [], []
    for h in range(n):
        scores = jnp.where(lvl == 1, s_diag[h], 0.0)
        for i, cur in enumerate(s_levels):
            scores = jnp.where(lvl == i + 2, cur[h], scores)
        b = bs[h]
        b_last = b[c - 1:c, :]
        q_in = (qs[h] * jnp.exp(b)).astype(BF16)
        k_end = (ks[h] * jnp.exp(b_last - b)).astype(BF16)
        v = get_v(h)
        st = get_st(h)
        outs.append(_dot(scores.astype(BF16), v.astype(BF16)) + _dot_nt(q_in, st.astype(BF16)))
        new_sts.append(st * jnp.exp(b_last) + _dot(v.T.astype(BF16), k_end))
    return outs, new_sts


def _hgrn_prompt_kernel(q_ref, f_ref, i_ref, g_ref, lb_ref, nw_ref, tri_ref, lvl_ref,
                        y_ref, s_ref, st_scr):
    c = pl.program_id(1)

    @pl.when(c == 0)
    def _():
        st_scr[...] = jnp.zeros_like(st_scr)

    sls = [slice(h * HG_DK, (h + 1) * HG_DK) for h in range(HG_HEADS)]
    qs, ks, logfs = [], [], []
    for sl in sls:
        lb = lb_ref[:, sl]
        sg = _sigmoid(f_ref[:, sl])
        logfs.append(jnp.log(jnp.maximum(lb + (1.0 - lb) * sg, TINY)))
        ks.append((1.0 - lb) * (1.0 - sg))
        qs.append(_silu(q_ref[:, sl]))
    outs, new_sts = _gla_chunk_heads(qs, ks, lambda h: i_ref[:, sls[h]], logfs,
                                     lambda h: st_scr[h], tri_ref[...], lvl_ref[...])
    for h, sl in enumerate(sls):
        st_scr[h] = new_sts[h]
        y = _group_norm(outs[h], nw_ref[:, sl], HG_DV) * _silu(g_ref[:, sl])
        y_ref[:, sl] = y.astype(BF16)

    @pl.when(c == pl.num_programs(1) - 1)
    def _():
        for h in range(HG_HEADS):
            s_ref[0, h] = st_scr[h].T


def _hgrn_prompt(proj, lb, nw, bsz, seq):
    nc = seq // CHUNK
    blk = lambda j: pl.BlockSpec((CHUNK, BRANCH), lambda b, c, j=j: (b * nc + c, j))
    row = pl.BlockSpec((1, BRANCH), lambda b, c: (0, 0))
    cc = pl.BlockSpec((CHUNK, CHUNK), lambda b, c: (0, 0))
    c3 = pl.BlockSpec((CHUNK, 3 * CHUNK), lambda b, c: (0, 0))
    return pl.pallas_call(
        _hgrn_prompt_kernel,
        grid=(bsz, nc),
        in_specs=[blk(BLK_HG_Q), blk(BLK_HG_F), blk(BLK_HG_I), blk(BLK_HG_G), row, row, c3, cc],
        out_specs=[pl.BlockSpec((CHUNK, BRANCH), lambda b, c: (b * nc + c, 0)),
                   pl.BlockSpec((1, HG_HEADS, HG_DK, HG_DV), lambda b, c: (b, 0, 0, 0))],
        out_shape=[jax.ShapeDtypeStruct((bsz * seq, BRANCH), BF16),
                   jax.ShapeDtypeStruct((bsz, HG_HEADS, HG_DK, HG_DV), F32)],
        scratch_shapes=[pltpu.VMEM((HG_HEADS, HG_DV, HG_DK), F32)],
        compiler_params=_cparams("parallel", "arbitrary"),
        name="hgrn_prompt",
    )(proj, proj, proj, proj, lb, nw, _tri_const(CHUNK), _level_const(CHUNK))


def _gla_prompt_kernel(q_ref, k_ref, v_ref, g_ref, sm_ref, wup_ref, bup_ref, nw_ref,
                       tri_ref, lvl_ref, y_ref, s_ref, st_scr):
    c = pl.program_id(1)

    @pl.when(c == 0)
    def _():
        st_scr[...] = jnp.zeros_like(st_scr)

    up = _dot(sm_ref[...].astype(BF16), wup_ref[...]) + bup_ref[...]
    log_a = -_softplus(-up) * (1.0 / GLA_TAU)
    ksl = [slice(h * GLA_DK, (h + 1) * GLA_DK) for h in range(GLA_HEADS)]
    vsl = [slice(h * GLA_DV, (h + 1) * GLA_DV) for h in range(GLA_HEADS)]
    outs, new_sts = _gla_chunk_heads(
        [q_ref[:, sl] * (GLA_DK ** -0.5) for sl in ksl], [k_ref[:, sl] for sl in ksl],
        lambda h: v_ref[:, vsl[h]], [log_a[:, sl] for sl in ksl], lambda h: st_scr[h],
        tri_ref[...], lvl_ref[...])
    for h, sl in enumerate(vsl):
        st_scr[h] = new_sts[h]
        y = _group_norm(outs[h], nw_ref[:, sl], GLA_DV) * _silu(g_ref[:, sl])
        y_ref[:, sl] = y.astype(BF16)

    @pl.when(c == pl.num_programs(1) - 1)
    def _():
        for h in range(GLA_HEADS):
            s_ref[0, h] = st_scr[h].T


def _gla_prompt(proj, wup, bup, nw, bsz, seq):
    nc = seq // CHUNK
    blk = lambda w, j: pl.BlockSpec((CHUNK, w), lambda b, c, j=j: (b * nc + c, j))
    const = lambda shape: pl.BlockSpec(shape, lambda b, c: (0,) * len(shape))
    return pl.pallas_call(
        _gla_prompt_kernel,
        grid=(bsz, nc),
        in_specs=[blk(GLA_KDIM, BLK_GL_Q), blk(GLA_KDIM, BLK_GL_K), blk(BRANCH, BLK_GL_V),
                  blk(BRANCH, BLK_GL_G), blk(128, BLK_SMALL),
                  const((128, GLA_KDIM)), const((1, GLA_KDIM)), const((1, BRANCH)),
                  const((CHUNK, 3 * CHUNK)), const((CHUNK, CHUNK))],
        out_specs=[pl.BlockSpec((CHUNK, BRANCH), lambda b, c: (b * nc + c, 0)),
                   pl.BlockSpec((1, GLA_HEADS, GLA_DK, GLA_DV), lambda b, c: (b, 0, 0, 0))],
        out_shape=[jax.ShapeDtypeStruct((bsz * seq, BRANCH), BF16),
                   jax.ShapeDtypeStruct((bsz, GLA_HEADS, GLA_DK, GLA_DV), F32)],
        scratch_shapes=[pltpu.VMEM((GLA_HEADS, GLA_DV, GLA_DK), F32)],
        compiler_params=_cparams("parallel", "arbitrary"),
        name="gla_prompt",
    )(proj, proj, proj, proj, proj, wup, bup, nw, _tri_const(CHUNK), _level_const(CHUNK))


def _chunk_conv(x_ref, carry, w_ref, b_ref, first):
    c = x_ref.shape[0]
    last = CONV_W - 1

    @pl.when(first)
    def _():
        carry[...] = jnp.zeros_like(carry)

    x = x_ref[...]
    y = b_ref[...] + x * w_ref[last:last + 1, :]
    for s in range(1, CONV_W):
        y = y + pltpu.roll(x, s, 0) * w_ref[last - s:last - s + 1, :]
    ext = jnp.concatenate([carry[...], x[0:8, :]], axis=0)
    head = b_ref[...] + ext[8:16, :] * w_ref[last:last + 1, :]
    for s in range(1, CONV_W):
        head = head + ext[8 - s:16 - s, :] * w_ref[last - s:last - s + 1, :]
    tail = x[c - 8:c, :]
    carry[...] = tail
    return jnp.concatenate([head, y[8:, :]], axis=0), tail[8 - last:8, :]


def _rglru_gates(xc, wr_ref, br_ref, wi_ref, bi_ref, lam_ref):
    a_parts, u_parts = [], []
    for n in range(RG_BLOCKS):
        sl = slice(n * RG_BW, (n + 1) * RG_BW)
        xb = xc[:, sl]
        xb16 = xb.astype(BF16)
        r = _sigmoid(_dot(xb16, wr_ref[n]) + br_ref[:, sl])
        i = _sigmoid(_dot(xb16, wi_ref[n]) + bi_ref[:, sl])
        log_a = -RG_C * r * _softplus(-lam_ref[:, sl])
        a = jnp.exp(log_a)
        one_m_a2 = -jnp.tanh(log_a) * (a * a + 1.0)
        a_parts.append(a)
        u_parts.append(jnp.sqrt(jnp.maximum(one_m_a2, 0.0)) * (i * xb))
    return a_parts, u_parts


def _rglru_prompt_kernel(x_ref, g_ref, cw_ref, cb_ref, wr_ref, br_ref, wi_ref, bi_ref, lam_ref,
                         y_ref, h_ref, cs_ref, buf, h_scr):
    c = pl.program_id(1)
    first = c == 0

    @pl.when(first)
    def _():
        h_scr[...] = jnp.zeros_like(h_scr)

    xc, tail = _chunk_conv(x_ref, buf, cw_ref, cb_ref, first)
    cs_ref[0] = tail
    a_parts, u_parts = _rglru_gates(xc, wr_ref, br_ref, wi_ref, bi_ref, lam_ref)
    n_rows = xc.shape[0]
    ng = n_rows // 8
    sub = lax.broadcasted_iota(jnp.int32, (ng, 8, RG_BW), 1)
    for n in range(RG_BLOCKS):
        sl = slice(n * RG_BW, (n + 1) * RG_BW)
        a = a_parts[n].reshape(ng, 8, RG_BW)
        u = u_parts[n].reshape(ng, 8, RG_BW)
        s = 1
        while s < 8:
            keep = sub >= s
            a_sh = jnp.where(keep, pltpu.roll(a, s, 1), 1.0)
            u_sh = jnp.where(keep, pltpu.roll(u, s, 1), 0.0)
            u = a * u_sh + u
            a = a * a_sh
            s *= 2
        h = h_scr[:, sl]
        groups = []
        for j in range(ng):
            hj = a[j] * h + u[j]
            groups.append(hj)
            h = hj[7:8, :]
        h_scr[:, sl] = h
        y_ref[:, sl] = (jnp.concatenate(groups, axis=0) * _silu(g_ref[:, sl])).astype(BF16)
    h_ref[0] = h_scr[...]


def _rglru_prompt(proj, cw, cb, wr, br, wi, bi, lam, bsz, seq):
    nc = seq // CHUNK
    blk = lambda j: pl.BlockSpec((CHUNK, BRANCH), lambda b, c, j=j: (b * nc + c, j))
    const = lambda shape: pl.BlockSpec(shape, lambda b, c: (0,) * len(shape))
    return pl.pallas_call(
        _rglru_prompt_kernel,
        grid=(bsz, nc),
        in_specs=[blk(BLK_RG_X), blk(BLK_RG_G), const((CONV_W, BRANCH)), const((1, BRANCH)),
                  const((RG_BLOCKS, RG_BW, RG_BW)), const((1, BRANCH)),
                  const((RG_BLOCKS, RG_BW, RG_BW)), const((1, BRANCH)), const((1, BRANCH))],
        out_specs=[pl.BlockSpec((CHUNK, BRANCH), lambda b, c: (b * nc + c, 0)),
                   pl.BlockSpec((1, 1, BRANCH), lambda b, c: (b, 0, 0)),
                   pl.BlockSpec((1, CONV_W - 1, BRANCH), lambda b, c: (b, 0, 0))],
        out_shape=[jax.ShapeDtypeStruct((bsz * seq, BRANCH), BF16),
                   jax.ShapeDtypeStruct((bsz, 1, BRANCH), F32),
                   jax.ShapeDtypeStruct((bsz, CONV_W - 1, BRANCH), F32)],
        scratch_shapes=[pltpu.VMEM((8, BRANCH), F32), pltpu.VMEM((1, BRANCH), F32)],
        compiler_params=_cparams("parallel", "arbitrary"),
        name="rglru_prompt",
    )(proj, proj, cw, cb, wr, br, wi, bi, lam)


def _ssd_prompt_kernel(z_ref, x_ref, bc_ref, sm_ref, cwx_ref, cbx_ref, cwb_ref, cbb_ref,
                       dtb_ref, a_ref, d_ref, nw_ref, tri_ref, exp_ref,
                       y_ref, s_ref, cs_ref, st_scr, xbuf, bcbuf):
    c = pl.program_id(1)
    first = c == 0
    n_rows = x_ref.shape[0]
    gw = BRANCH // SSD_G

    @pl.when(first)
    def _():
        st_scr[...] = jnp.zeros_like(st_scr)

    xc, xtail = _chunk_conv(x_ref, xbuf, cwx_ref, cbx_ref, first)
    bcc, bctail = _chunk_conv(bc_ref, bcbuf, cwb_ref, cbb_ref, first)
    cs_ref[0, :, 0:BRANCH] = xtail
    cs_ref[0, :, BRANCH:SSD_CONV_DIM] = bctail
    xs = _silu(xc)
    bcs = _silu(bcc)

    tri = tri_ref[...]
    expand = exp_ref[...]
    dt = _softplus(sm_ref[...] + dtb_ref[...])
    cum = _sel_left(tri, dt * a_ref[...])
    cum_t = cum.T
    dt_x = _sel_right(dt, expand)
    cum_x = _sel_right(cum, expand)
    cum_last = cum_x[n_rows - 1:n_rows, :]
    dec_in = jnp.exp(cum_x)
    xdt = xs * dt_x
    xw = (xdt * jnp.exp(cum_last - cum_x)).astype(BF16)
    xdt16 = xdt.astype(BF16)

    t_idx = lax.broadcasted_iota(jnp.int32, (n_rows, n_rows), 0)
    s_idx = lax.broadcasted_iota(jnp.int32, (n_rows, n_rows), 1)
    causal = s_idx <= t_idx
    lane = lax.broadcasted_iota(jnp.int32, (n_rows, 2 * SSD_P), 1)

    y_parts = []
    for g in range(SSD_G):
        b_g = bcs[:, g * SSD_N:(g + 1) * SSD_N].astype(BF16)
        c_g = bcs[:, (SSD_G + g) * SSD_N:(SSD_G + g + 1) * SSD_N].astype(BF16)
        cb = _dot_nt(c_g, b_g)
        st_g = st_scr[:, g * gw:(g + 1) * gw]
        y_inter = _dot(c_g, st_g.astype(BF16)) * dec_in[:, g * gw:(g + 1) * gw]
        heads_per_g = SSD_HEADS // SSD_G
        for pair in range(heads_per_g // 2):
            h0 = g * heads_per_g + 2 * pair
            xp = xdt16[:, h0 * SSD_P:(h0 + 2) * SSD_P]
            outs = []
            for h in (h0, h0 + 1):
                col = DT_LANE + h
                seg = cum[:, col:col + 1] - cum_t[col:col + 1, :]
                lmat = jnp.where(causal, jnp.exp(jnp.where(causal, seg, 0.0)), 0.0)
                outs.append(_dot((cb * lmat).astype(BF16), xp))
            y_parts.append(jnp.where(lane < SSD_P, outs[0], outs[1]))
        y_parts.append(y_inter)
        st_scr[:, g * gw:(g + 1) * gw] = (
            st_g * jnp.exp(cum_last[:, g * gw:(g + 1) * gw])
            + _dot(bcs[:, g * SSD_N:(g + 1) * SSD_N].T.astype(BF16), xw[:, g * gw:(g + 1) * gw]))
    npair = SSD_HEADS // SSD_G // 2
    y = jnp.concatenate(
        [jnp.concatenate(y_parts[g * (npair + 1):g * (npair + 1) + npair], axis=1)
         + y_parts[g * (npair + 1) + npair] for g in range(SSD_G)], axis=1)
    y = (y + xs * d_ref[...]) * _silu(z_ref[...])
    y_ref[...] = _group_norm(y, nw_ref[...], gw).astype(BF16)

    @pl.when(c == pl.num_programs(1) - 1)
    def _():
        s_ref[0] = st_scr[...].T.reshape(SSD_HEADS, SSD_P, SSD_N)


def _ssd_expand_const(width):
    e = np.zeros((128, SSD_HEADS * width), np.float32)
    for h in range(SSD_HEADS):
        e[DT_LANE + h, h * width:(h + 1) * width] = 1.0
    return jnp.asarray(np.tile(e, (3, 1)), dtype=BF16)


def _ssd_prompt(proj, cwx, cbx, cwb, cbb, dtb, a_pad, d_x, nw, bsz, seq):
    nc = seq // CHUNK
    blk = lambda w, j: pl.BlockSpec((CHUNK, w), lambda b, c, j=j: (b * nc + c, j))
    const = lambda shape: pl.BlockSpec(shape, lambda b, c: (0,) * len(shape))
    return pl.pallas_call(
        _ssd_prompt_kernel,
        grid=(bsz, nc),
        in_specs=[blk(BRANCH, BLK_SS_Z), blk(BRANCH, BLK_SS_X), blk(SSD_BC, BLK_SS_BC),
                  blk(128, BLK_SMALL),
                  const((CONV_W, BRANCH)), const((1, BRANCH)), const((CONV_W, SSD_BC)),
                  const((1, SSD_BC)), const((1, 128)), const((1, 128)), const((1, BRANCH)),
                  const((1, BRANCH)), const((CHUNK, 3 * CHUNK)), const((3 * 128, BRANCH))],
        out_specs=[pl.BlockSpec((CHUNK, BRANCH), lambda b, c: (b * nc + c, 0)),
                   pl.BlockSpec((1, SSD_HEADS, SSD_P, SSD_N), lambda b, c: (b, 0, 0, 0)),
                   pl.BlockSpec((1, CONV_W - 1, SSD_CONV_DIM), lambda b, c: (b, 0, 0))],
        out_shape=[jax.ShapeDtypeStruct((bsz * seq, BRANCH), BF16),
                   jax.ShapeDtypeStruct((bsz, SSD_HEADS, SSD_P, SSD_N), F32),
                   jax.ShapeDtypeStruct((bsz, CONV_W - 1, SSD_CONV_DIM), F32)],
        scratch_shapes=[pltpu.VMEM((SSD_N, BRANCH), F32),
                        pltpu.VMEM((8, BRANCH), F32),
                        pltpu.VMEM((8, SSD_BC), F32)],
        compiler_params=_cparams("parallel", "arbitrary"),
        name="ssd_prompt",
    )(proj, proj, proj, proj, cwx, cbx, cwb, cbb, dtb, a_pad, d_x, nw,
      _tri_const(CHUNK), _ssd_expand_const(SSD_P))


def _step_conv(x, cs_ref, w_ref, b_ref, ncs_ref):
    y = b_ref[...] + x * w_ref[CONV_W - 1:CONV_W, :]
    for j in range(CONV_W - 1):
        y = y + cs_ref[j] * w_ref[j:j + 1, :]
    for j in range(CONV_W - 2):
        ncs_ref[j] = cs_ref[j + 1]
    ncs_ref[CONV_W - 2] = x
    return y


def _sample_pre_kernel(p_ref, lb_ref, rcs_ref, rh_ref, rcw_ref, rcb_ref, wr_ref, br_ref, wi_ref,
                       bi_ref, lam_ref, wup_ref, bup_ref, scs_ref, scw_ref, scb_ref, dtb_ref,
                       a_ref, exp_ref, expw_ref,
                       hq_ref, hk_ref, gq_ref, gd_ref, yrg_ref, nrh_ref, nrcs_ref,
                       sx_ref, sbc_ref, sdx_ref, sda_ref, nscs_ref):
    col = lambda blk, w: slice(blk * w, (blk + 1) * w)
    f = p_ref[:, col(BLK_HG_F, BRANCH)]
    lb = lb_ref[...]
    hq_ref[...] = _silu(p_ref[:, col(BLK_HG_Q, BRANCH)])
    hk_ref[...] = (1.0 - lb) * (1.0 - _sigmoid(f))
    sm = p_ref[:, col(BLK_SMALL, 128)]
    up = _dot(sm.astype(BF16), wup_ref[...]) + bup_ref[...]
    gq_ref[...] = p_ref[:, col(BLK_GL_Q, GLA_KDIM)] * (GLA_DK ** -0.5)
    gd_ref[...] = jnp.exp(-_softplus(-up) * (1.0 / GLA_TAU))
    xc = _step_conv(p_ref[:, col(BLK_RG_X, BRANCH)], rcs_ref, rcw_ref, rcb_ref, nrcs_ref)
    a_parts, u_parts = _rglru_gates(xc, wr_ref, br_ref, wi_ref, bi_ref, lam_ref)
    h = jnp.concatenate(a_parts, axis=1) * rh_ref[...] + jnp.concatenate(u_parts, axis=1)
    nrh_ref[...] = h
    yrg_ref[...] = (h * _silu(p_ref[:, col(BLK_RG_G, BRANCH)])).astype(BF16)
    xbc = jnp.concatenate([p_ref[:, col(BLK_SS_X, BRANCH)], p_ref[:, col(BLK_SS_BC, SSD_BC)]], axis=1)
    xbc = _silu(_step_conv(xbc, scs_ref, scw_ref, scb_ref, nscs_ref))
    xs = xbc[:, 0:BRANCH]
    sx_ref[...] = xs
    sbc_ref[...] = xbc[:, BRANCH:SSD_CONV_DIM]
    dt = _softplus(sm + dtb_ref[...])
    expand = exp_ref[...]
    sdx_ref[...] = xs * _sel_right(dt, expand)
    sda_ref[...] = jnp.exp(_sel_right(dt * a_ref[...], expw_ref[...]))


def _sample_pre(proj, lb, rcs, rh, rcw, rcb, wr, br, wi, bi, lam, wup, bup, scs, scw, scb,
                dtb, a_pad):
    nb = proj.shape[0]
    sd = lambda *shape, dt=F32: jax.ShapeDtypeStruct(shape, dt)
    return pl.pallas_call(
        _sample_pre_kernel,
        out_shape=[sd(nb, BRANCH), sd(nb, BRANCH),
                   sd(nb, GLA_KDIM), sd(nb, GLA_KDIM),
                   sd(nb, BRANCH, dt=BF16), sd(nb, BRANCH), sd(CONV_W - 1, nb, BRANCH),
                   sd(nb, BRANCH), sd(nb, SSD_BC), sd(nb, BRANCH), sd(nb, SSD_HEADS * SSD_N),
                   sd(CONV_W - 1, nb, SSD_CONV_DIM)],
        compiler_params=pltpu.CompilerParams(vmem_limit_bytes=VMEM_LIMIT),
        name="sample_pre",
    )(proj, lb, rcs, rh, rcw, rcb, wr, br, wi, bi, lam, wup, bup, scs, scw, scb, dtb, a_pad,
      _ssd_expand_const(SSD_P), _ssd_expand_const(SSD_N))


def _pad_t(x):
    r = x.shape[0]
    return jnp.concatenate([x, jnp.zeros((128 - r, 128), F32)], axis=0).T


STATE_UNROLL = 4


def _gla_state_kernel(*refs, heads, tied):
    so_ref, o_ref = refs[-2:]
    if tied:
        k_ref, v_ref, q_ref, s_ref = refs[:4]
    else:
        d_ref, k_ref, v_ref, q_ref, s_ref = refs[:5]
    dv = s_ref.shape[-1]

    def body(b, carry):
        kt_ = _pad_t(k_ref[b])
        dt_ = None if tied else _pad_t(d_ref[b])
        vr = v_ref[b]
        qr = q_ref[b]
        for h in range(heads):
            kb = jnp.broadcast_to(kt_[:, h:h + 1], (kt_.shape[0], dv))
            d = jnp.maximum(1.0 - kb, TINY) if tied else dt_[:, h:h + 1]
            s_new = d * s_ref[b, h] + kb * vr[h:h + 1, :]
            so_ref[b, h] = s_new
            q8 = jnp.broadcast_to(qr[h:h + 1, :], (8, qr.shape[1])).astype(BF16)
            o_ref[b, h:h + 1, :] = _dot(q8, s_new.astype(BF16))[0:1, :]
        return carry
    lax.fori_loop(0, s_ref.shape[0], body, 0, unroll=STATE_UNROLL)


def _state_call(kern, name, vec_args, vec_specs, s_all, so_prev, layer, o_shape, o_spec, bb):
    nb = s_all.shape[1]
    st = pl.BlockSpec((None, bb) + s_all.shape[2:], lambda i: (layer, i, 0, 0, 0))
    in_specs = list(vec_specs) + [st]
    args = list(vec_args) + [s_all]
    aliases = {}
    if so_prev is not None:
        in_specs.append(pl.BlockSpec(memory_space=pl.ANY))
        args.append(so_prev)
        aliases = {len(args) - 1: 0}
    return pl.pallas_call(
        kern,
        grid=(nb // bb,),
        in_specs=in_specs,
        out_specs=[st, o_spec],
        out_shape=[jax.ShapeDtypeStruct(s_all.shape, F32), o_shape],
        input_output_aliases=aliases,
        compiler_params=_cparams("parallel"),
        name=name,
    )(*args)


def _gla_state(d, k, v, q, s_all, so_prev, layer, *, bb):
    _, nb, heads, dk, dv = s_all.shape
    vec = lambda w: pl.BlockSpec((bb, heads, w), lambda i: (i, 0, 0))
    tied = d is None
    args = (k, v, q) if tied else (d, k, v, q)
    specs = (vec(dk), vec(dv), vec(dk)) if tied else (vec(dk), vec(dk), vec(dv), vec(dk))
    return _state_call(functools.partial(_gla_state_kernel, heads=heads, tied=tied),
                       "hgrn_state" if tied else "gla_state", args, specs, s_all, so_prev, layer,
                       jax.ShapeDtypeStruct((nb, heads, dv), F32), vec(dv), bb)


def _ssd_state_kernel(da_ref, dx_ref, b_ref, c_ref, s_ref, *rest):
    so_ref, y_ref = rest[-2:]
    hpg = SSD_HEADS // SSD_G

    def body(b, carry):
        ar = da_ref[b]
        xt_ = _pad_t(dx_ref[b])
        br = b_ref[b]
        cr = c_ref[b]
        for h in range(SSD_HEADS):
            g = h // hpg
            rows = slice((h % 2) * SSD_P, (h % 2 + 1) * SSD_P)
            j = h // 2
            s_new = ar[h:h + 1, :] * s_ref[b, h] + xt_[rows, j:j + 1] * br[g:g + 1, :]
            so_ref[b, h] = s_new
            c8 = jnp.broadcast_to(cr[g:g + 1, :], (8, SSD_N)).astype(BF16)
            y_ref[b, h:h + 1, :] = _dot_nt(c8, s_new.astype(BF16))[0:1, :]
        return carry
    lax.fori_loop(0, s_ref.shape[0], body, 0, unroll=STATE_UNROLL)


def _ssd_state(da, dx, bv, cv, s_all, so_prev, layer, *, bb):
    nb = s_all.shape[1]
    vec = lambda r, w: pl.BlockSpec((bb, r, w), lambda i: (i, 0, 0))
    return _state_call(_ssd_state_kernel, "ssd_state", (da, dx, bv, cv),
                       (vec(SSD_HEADS, SSD_N), vec(8, 128), vec(SSD_G, SSD_N), vec(SSD_G, SSD_N)),
                       s_all, so_prev, layer,
                       jax.ShapeDtypeStruct((nb, SSD_HEADS, SSD_P), F32), vec(SSD_HEADS, SSD_P), bb)


def _sample_post_kernel(p_ref, ohg_ref, ogl_ref, yss_ref, sx_ref, d_ref, hnw_ref, gnw_ref,
                        snw_ref, yhg_ref, ygl_ref, yso_ref):
    col = lambda blk: slice(blk * BRANCH, (blk + 1) * BRANCH)
    yhg_ref[...] = (_group_norm(ohg_ref[...], hnw_ref[...], HG_DV)
                    * _silu(p_ref[:, col(BLK_HG_G)])).astype(BF16)
    ygl_ref[...] = (_group_norm(ogl_ref[...], gnw_ref[...], GLA_DV)
                    * _silu(p_ref[:, col(BLK_GL_G)])).astype(BF16)
    y = (yss_ref[...] + sx_ref[...] * d_ref[...]) * _silu(p_ref[:, col(BLK_SS_Z)])
    yso_ref[...] = _group_norm(y, snw_ref[...], BRANCH // SSD_G).astype(BF16)


def _sample_post(proj, ohg, ogl, yss, sx, d_x, hnw, gnw, snw):
    nb = proj.shape[0]
    out = jax.ShapeDtypeStruct((nb, BRANCH), BF16)
    return pl.pallas_call(
        _sample_post_kernel, out_shape=[out, out, out],
        compiler_params=pltpu.CompilerParams(vmem_limit_bytes=VMEM_LIMIT),
        name="sample_post",
    )(proj, ohg, ogl, yss, sx, d_x, hnw, gnw, snw)


def _tail_kernel(w_ref, o_ref):
    cols = w_ref.shape[1]
    n_a = ORIG_SSD_Z - ORIG_GLA_A
    n_zx = ORIG_SSD_DT - ORIG_SSD_Z
    n_dt = N_IN - ORIG_SSD_DT
    o_ref[0:n_zx, :] = w_ref[n_a:n_a + n_zx, :].astype(BF16)
    o_ref[n_zx:n_zx + n_a, :] = w_ref[0:n_a, :].astype(BF16)
    o_ref[n_zx + n_a:n_zx + n_a + n_dt, :] = w_ref[n_a + n_zx:n_a + n_zx + n_dt, :].astype(BF16)
    o_ref[n_zx + n_a + n_dt:N_TAIL, :] = jnp.zeros((N_TAIL - n_zx - n_a - n_dt, cols), BF16)


def _prep_w_tail(w_in_t, *, cb=256):
    depth, _, d = w_in_t.shape
    return pl.pallas_call(
        _tail_kernel,
        grid=(depth, d // cb),
        in_specs=[pl.BlockSpec((None, N_TAIL, cb), lambda l, i: (l, N_MAIN // N_TAIL, i))],
        out_specs=pl.BlockSpec((None, N_TAIL, cb), lambda l, i: (l, 0, i)),
        out_shape=jax.ShapeDtypeStruct((depth, N_TAIL, d), BF16),
        compiler_params=_cparams("parallel", "parallel"),
        name="w_tail",
    )(w_in_t)


def _pad_lanes(v, start, width=128):
    out = jnp.zeros((v.shape[0], 1, width), F32)
    return out.at[:, 0, start:start + v.shape[1]].set(v.astype(F32))


def kernel(x_prompt, x_sample, state_hgrn, state_rglru, state_rglru_conv, state_gla, state_ssd, state_ssd_conv, rms_in, w_in, hgrn_lower_bounds, hgrn_norm, rglru_conv_w, rglru_conv_b, rglru_w_r, rglru_b_r, rglru_w_i, rglru_b_i, rglru_lambda, gla_w_up, gla_b_up, gla_norm, ssd_conv_w, ssd_conv_b, ssd_dt_bias, ssd_a_log, ssd_d, ssd_norm, w_out, rms_final):
    bsz, seq, _ = x_prompt.shape
    nb = x_sample.shape[0]
    row = lambda v: v.reshape(DEPTH, 1, -1).astype(F32)

    lb_all = _lower_bounds(hgrn_lower_bounds.astype(F32)).reshape(DEPTH, 1, BRANCH)
    w_in = jnp.swapaxes(w_in.astype(F32), 1, 2)
    w_tail = _prep_w_tail(w_in)
    w_out16 = w_out.astype(BF16)
    wr16 = rglru_w_r.astype(BF16)
    wi16 = rglru_w_i.astype(BF16)
    wup16 = jnp.concatenate(
        [gla_w_up, jnp.zeros((DEPTH, 128 - GLA_RANK, GLA_KDIM), gla_w_up.dtype)], axis=1).astype(BF16)
    dtb = _pad_lanes(ssd_dt_bias, DT_LANE)
    a_pad = _pad_lanes(-jnp.exp(ssd_a_log.astype(F32)), DT_LANE)
    d_x = jnp.repeat(ssd_d.astype(F32), SSD_P, axis=-1).reshape(DEPTH, 1, BRANCH)
    rms_in_r, hnw, gnw, snw = row(rms_in), row(hgrn_norm), row(gla_norm), row(ssd_norm)
    rcb, br, bi, lam, bup = (row(rglru_conv_b), row(rglru_b_r), row(rglru_b_i),
                             row(rglru_lambda), row(gla_b_up))
    scb = row(ssd_conv_b)

    xp = x_prompt.reshape(bsz * seq, D_MODEL)
    xs = x_sample.reshape(nb, D_MODEL)
    tm_in = 1024 if (bsz * seq) % 1024 == 0 else CHUNK
    tm_out = 512 if (bsz * seq) % 512 == 0 else CHUNK
    bb = 8 if nb % 8 == 0 else 1
    rf = rms_final.reshape(1, D_MODEL).astype(F32)

    hp = _rmsnorm(xp, rms_in_r[0], BF16, tm=tm_out)
    hs = _rmsnorm(xs, rms_in_r[0], BF16, tm=nb)
    outs_p = [[] for _ in range(6)]
    outs_s = [[] for _ in range(3)]
    ns_hg = ns_gl = ns_ss = None
    for l in range(DEPTH):
        scw = ssd_conv_w[l].astype(F32)
        next_w, next_dt = (rms_in_r[l + 1], BF16) if l + 1 < DEPTH else (rf, F32)
        proj = _inproj(hp, w_in, w_tail, l, tm=tm_in, tn=1024)
        y_hg, s_hg = _hgrn_prompt(proj, lb_all[l], hnw[l], bsz, seq)
        y_rg, s_rg, s_rgc = _rglru_prompt(proj, rglru_conv_w[l].astype(F32), rcb[l], wr16[l], br[l],
                                          wi16[l], bi[l], lam[l], bsz, seq)
        y_gl, s_gl = _gla_prompt(proj, wup16[l], bup[l], gnw[l], bsz, seq)
        y_ss, s_ss, s_ssc = _ssd_prompt(proj, scw[:, :BRANCH], scb[l][:, :BRANCH], scw[:, BRANCH:],
                                        scb[l][:, BRANCH:], dtb[l], a_pad[l], d_x[l], snw[l], bsz, seq)
        xp, hp = _outproj((y_hg, y_rg, y_gl, y_ss), w_out16[l], xp, next_w, next_dt,
                          tm=tm_out if next_dt == BF16 else tm_out // 2)
        for lst, s in zip(outs_p, (s_hg, s_rg.reshape(bsz, BRANCH), s_rgc, s_gl, s_ss, s_ssc)):
            lst.append(s)

        proj_s = _inproj(hs, w_in, w_tail, l, tm=nb, tn=1024)
        (hq, hk, gq, gd, yrg_s, nrh, nrcs, sx, sbc, sdx, sda, nscs) = _sample_pre(
            proj_s, lb_all[l], jnp.swapaxes(state_rglru_conv[l], 0, 1), state_rglru[l],
            rglru_conv_w[l].astype(F32), rcb[l], wr16[l], br[l], wi16[l], bi[l], lam[l],
            wup16[l], bup[l], jnp.swapaxes(state_ssd_conv[l], 0, 1), scw, scb[l], dtb[l], a_pad[l])
        hv = proj_s[:, BLK_HG_I * BRANCH:(BLK_HG_I + 1) * BRANCH]
        gk = proj_s[:, BLK_GL_K * GLA_KDIM:(BLK_GL_K + 1) * GLA_KDIM]
        gv = proj_s[:, BLK_GL_V * BRANCH:(BLK_GL_V + 1) * BRANCH]
        hsh = lambda a: a.reshape(nb, HG_HEADS, -1)
        gsh = lambda a: a.reshape(nb, GLA_HEADS, -1)
        ns_hg, o_hg = _gla_state(None, hsh(hk), hsh(hv), hsh(hq), state_hgrn, ns_hg, l, bb=bb)
        ns_gl, o_gl = _gla_state(gsh(gd), gsh(gk), gsh(gv), gsh(gq), state_gla, ns_gl, l, bb=bb)
        ns_ss, y_ssr = _ssd_state(sda.reshape(nb, SSD_HEADS, SSD_N), sdx.reshape(nb, 8, 128),
                                  sbc[:, :SSD_G * SSD_N].reshape(nb, SSD_G, SSD_N),
                                  sbc[:, SSD_G * SSD_N:].reshape(nb, SSD_G, SSD_N),
                                  state_ssd, ns_ss, l, bb=bb)
        yhg_s, ygl_s, yss_s = _sample_post(proj_s, o_hg.reshape(nb, BRANCH), o_gl.reshape(nb, BRANCH),
                                           y_ssr.reshape(nb, BRANCH), sx, d_x[l], hnw[l], gnw[l], snw[l])
        xs, hs = _outproj((yhg_s, yrg_s, ygl_s, yss_s), w_out16[l], xs, next_w, next_dt, tm=nb)
        for lst, s in zip(outs_s, (nrh, jnp.swapaxes(nrcs, 0, 1), jnp.swapaxes(nscs, 0, 1))):
            lst.append(s)

    y_prompt = hp.reshape(bsz, seq, D_MODEL)
    y_sample = hs.reshape(nb, 1, D_MODEL)
    s_rg, s_rgc, s_ssc = (jnp.stack(l) for l in outs_s)
    return ((y_prompt, y_sample) + tuple(jnp.stack(l) for l in outs_p)
            + (ns_hg, s_rg, s_rgc, ns_gl, ns_ss, s_ssc))
```

```python
import functools
import math

import numpy as np
import jax
import jax.numpy as jnp
from jax import lax
from jax.experimental import pallas as pl
from jax.experimental.pallas import tpu as pltpu

F32 = jnp.float32
BF16 = jnp.bfloat16

D_MODEL = 2048
DEPTH = 4
BRANCH = 1024
D_MIX = 4 * BRANCH
CONV_W = 4
EPS = 1e-6
TINY = 1e-30

HG_HEADS, HG_DK, HG_DV = 8, 128, 128
RG_BLOCKS, RG_BW, RG_C = 8, 128, 8.0
GLA_HEADS, GLA_DK, GLA_DV, GLA_RANK, GLA_TAU = 4, 128, 256, 16, 16.0
GLA_KDIM = GLA_HEADS * GLA_DK
SSD_HEADS, SSD_P, SSD_G, SSD_N = 16, 64, 2, 128
SSD_BC = 2 * SSD_G * SSD_N
SSD_CONV_DIM = BRANCH + SSD_BC

ORIG_GLA_A = 9216
ORIG_SSD_Z = 9232
ORIG_SSD_DT = 11792
N_IN = 11808
N_PROJ = 12288
COL_SMALL = 11776
DT_LANE = 16
BLK_HG_Q, BLK_HG_F, BLK_HG_I, BLK_HG_G = 0, 1, 2, 3
BLK_RG_X, BLK_RG_G = 4, 5
BLK_GL_Q, BLK_GL_K = 12, 13
BLK_GL_V, BLK_GL_G = 7, 8
BLK_SS_Z, BLK_SS_X = 9, 10
BLK_SS_BC = 22
BLK_SMALL = COL_SMALL // 128

CHUNK = 128
SUB = 16
LOG2E = math.log2(math.e)
VMEM_LIMIT = 52 * 1024 * 1024


def _cparams(*sem):
    return pltpu.CompilerParams(dimension_semantics=sem, vmem_limit_bytes=VMEM_LIMIT)


def _sigmoid(x):
    return 0.5 * jnp.tanh(0.5 * x) + 0.5


def _silu(x):
    return x * _sigmoid(x)


def _softplus(x):
    return jnp.maximum(x, 0.0) + jnp.log(1.0 + jnp.exp(-jnp.abs(x)))


def _dot(a, b):
    return jnp.dot(a, b, preferred_element_type=F32)


def _dot_nt(a, b):
    return lax.dot_general(a, b, (((1,), (1,)), ((), ())), preferred_element_type=F32)


def _split3(a):
    a0 = a.astype(BF16)
    r1 = a - a0.astype(F32)
    a1 = r1.astype(BF16)
    a2 = (r1 - a1.astype(F32)).astype(BF16)
    return a0, a1, a2


def _sel_left(sel3, x):
    return _dot(sel3, jnp.concatenate(_split3(x), axis=0))


def _sel_right(x, sel3):
    return _dot(jnp.concatenate(_split3(x), axis=1), sel3)


def _group_norm(y, w, width):
    parts = []
    for g in range(y.shape[1] // width):
        yg = y[:, g * width:(g + 1) * width]
        ms = jnp.mean(yg * yg, axis=-1, keepdims=True)
        parts.append(yg * lax.rsqrt(ms + EPS))
    out = parts[0] if len(parts) == 1 else jnp.concatenate(parts, axis=1)
    return out * w


def _tri_const(c):
    return jnp.asarray(np.tile(np.tril(np.ones((c, c), np.float32)), (1, 3)), dtype=BF16)


def _level_const(c):
    t = np.arange(c)[:, None]
    s = np.arange(c)[None, :]
    lvl = np.zeros((c, c), np.int32)
    lvl[(t // SUB == s // SUB) & (s <= t)] = 1
    h, code = SUB, 2
    while h < c:
        m = (t // (2 * h) == s // (2 * h)) & (t % (2 * h) >= h) & (s % (2 * h) < h)
        lvl[m] = code
        h *= 2
        code += 1
    return jnp.asarray(lvl)


def _lb_kernel(p_ref, o_ref):
    x = p_ref[...]
    m = jnp.max(x, axis=0, keepdims=True)
    e = jnp.exp(x - m)
    p = e / jnp.sum(e, axis=0, keepdims=True)
    acc = jnp.zeros_like(p[0:1])
    rows = [acc]
    for l in range(1, DEPTH):
        acc = acc + p[l:l + 1]
        rows.append(acc)
    o_ref[...] = jnp.concatenate(rows, axis=0)


def _lower_bounds(param):
    return pl.pallas_call(
        _lb_kernel, out_shape=jax.ShapeDtypeStruct(param.shape, F32), name="hgrn_lb")(param)


N_MAIN = ORIG_GLA_A
N_TAIL = N_PROJ - N_MAIN


def _inproj_kernel(h_ref, w_ref, wt_ref, o_ref, wb_scr, *, n_main, rb):
    j = pl.program_id(0)

    @pl.when(pl.program_id(1) == 0)
    def _():
        @pl.when(j < n_main)
        def _():
            def body(i, carry):
                r = pl.multiple_of(i * rb, rb)
                wb_scr[pl.ds(r, rb), :] = w_ref[pl.ds(r, rb), :].astype(BF16)
                return carry
            lax.fori_loop(0, wb_scr.shape[0] // rb, body, 0)

        @pl.when(j >= n_main)
        def _():
            wb_scr[...] = wt_ref[...]

    o_ref[...] = _dot_nt(h_ref[...], wb_scr[...])


def _inproj(h, w_in_t, w_tail_t, layer, *, tm, tn):
    m, d = h.shape
    n_main = N_MAIN // tn
    return pl.pallas_call(
        functools.partial(_inproj_kernel, n_main=n_main, rb=128),
        grid=(N_PROJ // tn, m // tm),
        in_specs=[pl.BlockSpec((tm, d), lambda j, i: (i, 0)),
                  pl.BlockSpec((None, tn, d), lambda j, i: (layer, jnp.minimum(j, n_main - 1), 0)),
                  pl.BlockSpec((None, tn, d), lambda j, i: (layer, jnp.maximum(j - n_main, 0), 0))],
        out_specs=pl.BlockSpec((tm, tn), lambda j, i: (i, j)),
        out_shape=jax.ShapeDtypeStruct((m, N_PROJ), F32),
        scratch_shapes=[pltpu.VMEM((tn, d), BF16)],
        compiler_params=_cparams("arbitrary", "arbitrary"),
        name="inproj",
    )(h, w_in_t, w_tail_t)


def _outproj_kernel(y0_ref, y1_ref, y2_ref, y3_ref, w_ref, x_ref, nw_ref, *rest, rb):
    xo_ref, ho_ref = rest[-2:]
    acc = x_ref[...]
    for g, y_ref in enumerate((y0_ref, y1_ref, y2_ref, y3_ref)):
        acc = acc + _dot(y_ref[...], w_ref[g * BRANCH:(g + 1) * BRANCH, :])
    xo_ref[...] = acc

    def body(i, carry):
        r = pl.multiple_of(i * rb, rb)
        x = xo_ref[pl.ds(r, rb), :]
        ms = jnp.mean(x * x, axis=-1, keepdims=True)
        ho_ref[pl.ds(r, rb), :] = (x * lax.rsqrt(ms + EPS) * nw_ref[...]).astype(ho_ref.dtype)
        return carry
    lax.fori_loop(0, xo_ref.shape[0] // rb, body, 0)


def _shared_rows(m, tm, shared):
    if shared is None:
        return m, 0, (), ()
    total, row0, buf = shared
    if buf is None:
        return total, row0 // tm, (), ()
    return total, row0 // tm, (buf,), (pl.BlockSpec(memory_space=pl.ANY),)


def _outproj(ys, w, x, norm_w, norm_dtype, *, tm, shared=None):
    m, d = x.shape
    yspec = pl.BlockSpec((tm, BRANCH), lambda i: (i, 0))
    xspec = pl.BlockSpec((tm, d), lambda i: (i, 0))
    rows, off, extra, extra_specs = _shared_rows(m, tm, shared)
    return pl.pallas_call(
        functools.partial(_outproj_kernel, rb=min(tm, 64)),
        grid=(m // tm,),
        in_specs=[yspec, yspec, yspec, yspec,
                  pl.BlockSpec((D_MIX, d), lambda i: (0, 0), pipeline_mode=pl.Buffered(1)),
                  xspec, pl.BlockSpec((1, d), lambda i: (0, 0)), *extra_specs],
        out_specs=[xspec, pl.BlockSpec((tm, d), lambda i: (i + off, 0))],
        out_shape=[jax.ShapeDtypeStruct((m, d), F32), jax.ShapeDtypeStruct((rows, d), norm_dtype)],
        input_output_aliases={7: 1} if extra else {},
        compiler_params=_cparams("parallel"),
        name="outproj",
    )(*ys, w, x, norm_w, *extra)


def _rmsnorm_kernel(x_ref, w_ref, *rest):
    o_ref = rest[-1]
    x = x_ref[...]
    ms = jnp.mean(x * x, axis=-1, keepdims=True)
    o_ref[...] = (x * lax.rsqrt(ms + EPS) * w_ref[...]).astype(o_ref.dtype)


def _rmsnorm(x, w, out_dtype, *, tm, shared=None):
    m, d = x.shape
    rows, off, extra, extra_specs = _shared_rows(m, tm, shared)
    return pl.pallas_call(
        _rmsnorm_kernel,
        grid=(m // tm,),
        in_specs=[pl.BlockSpec((tm, d), lambda i: (i, 0)),
                  pl.BlockSpec((1, d), lambda i: (0, 0)), *extra_specs],
        out_specs=pl.BlockSpec((tm, d), lambda i: (i + off, 0)),
        out_shape=jax.ShapeDtypeStruct((rows, d), out_dtype),
        input_output_aliases={2: 0} if extra else {},
        compiler_params=_cparams("parallel"),
        name="rmsnorm",
    )(x, w, *extra)


def _gla_chunk_heads(qs, ks, get_v, log2fs, get_st, tri, lvl, n_heads):
    n = len(qs)
    c, width = qs[0].shape
    bs = [_sel_left(tri, lf) for lf in log2fs]

    def ref_rows(b, rows, span):
        return jnp.concatenate(
            [jnp.broadcast_to(b[r:r + 1, :], (span, width)) for r in rows], axis=0)

    def rows_only(x, lo, hi):
        parts = [jnp.zeros((lo, width), BF16), x[lo:hi, :], jnp.zeros((c - hi, width), BF16)]
        return jnp.concatenate([p for p in parts if p.shape[0]], axis=0)

    s_diag = []
    for h in range(n):
        ed = bs[h] - ref_rows(bs[h], range(SUB // 2, c, SUB), SUB)
        s_diag.append(_dot_nt((qs[h] * jnp.exp2(ed)).astype(BF16), (ks[h] * jnp.exp2(-ed)).astype(BF16)))

    s_off = []
    for h in range(n):
        lq, lk = [], []
        half = SUB
        while half < c:
            ref = ref_rows(bs[h], range(half - 1, c, 2 * half), 2 * half)
            mixed = jnp.concatenate(
                [(ks[h] if (r // half) % 2 == 0 else qs[h])[r:r + half, :] for r in range(0, c, half)],
                axis=0)
            x = (mixed * jnp.exp2(-jnp.abs(bs[h] - ref))).astype(BF16)
            for r0 in range(0, c, 2 * half):
                lk.append(rows_only(x, r0, r0 + half))
                lq.append(rows_only(x, r0 + half, r0 + 2 * half))
            half *= 2
        s_off.append(_dot_nt(jnp.concatenate(lq, axis=1), jnp.concatenate(lk, axis=1)))

    outs, sts = [], {}
    for i in range(n):
        head = i % n_heads
        scores = jnp.where(lvl == 1, s_diag[i], s_off[i])
        b = bs[i]
        b_last = b[c - 1:c, :]
        q_in = (qs[i] * jnp.exp2(b)).astype(BF16)
        k_end = (ks[i] * jnp.exp2(b_last - b)).astype(BF16)
        v = get_v(i)
        st = sts[head] if head in sts else get_st(head)
        outs.append(_dot(scores.astype(BF16), v.astype(BF16)) + _dot_nt(q_in, st.astype(BF16)))
        sts[head] = st * jnp.exp2(b_last) + _dot(v.T.astype(BF16), k_end)
    return outs, [sts[h] for h in range(n_heads)]


def _hgrn_prompt_kernel(q_ref, f_ref, i_ref, g_ref, lb_ref, nw_ref, tri_ref, lvl_ref,
                        y_ref, s_ref, st_scr):
    c = pl.program_id(1)

    @pl.when(c == 0)
    def _():
        st_scr[...] = jnp.zeros_like(st_scr)

    items = [(slice(cc * CHUNK, (cc + 1) * CHUNK), slice(h * HG_DK, (h + 1) * HG_DK))
             for cc in range(q_ref.shape[0] // CHUNK) for h in range(HG_HEADS)]
    qs, ks, log2fs = [], [], []
    for rs, sl in items:
        k = (0.5 - 0.5 * lb_ref[:, sl]) * (1.0 - jnp.tanh(0.5 * f_ref[rs, sl]))
        ks.append(k)
        log2fs.append(jnp.log2(jnp.maximum(1.0 - k, TINY)))
        qs.append(_silu(q_ref[rs, sl]))
    outs, new_sts = _gla_chunk_heads(qs, ks, lambda i: i_ref[items[i][0], items[i][1]], log2fs,
                                     lambda h: st_scr[h], tri_ref[...], lvl_ref[...], HG_HEADS)
    for h in range(HG_HEADS):
        st_scr[h] = new_sts[h]
    for (rs, sl), o in zip(items, outs):
        y = _group_norm(o, nw_ref[:, sl], HG_DV) * _silu(g_ref[rs, sl])
        y_ref[rs, sl] = y.astype(BF16)

    @pl.when(c == pl.num_programs(1) - 1)
    def _():
        for h in range(HG_HEADS):
            s_ref[0, h] = st_scr[h].T


def _gla_rows(seq):
    for n in (4, 2):
        if seq % (n * CHUNK) == 0:
            return n * CHUNK
    return CHUNK


def _hgrn_prompt(proj, lb, nw, bsz, seq):
    rows = _gla_rows(seq)
    nc = seq // rows
    blk = lambda j: pl.BlockSpec((rows, BRANCH), lambda b, c, j=j: (b * nc + c, j))
    row = pl.BlockSpec((1, BRANCH), lambda b, c: (0, 0))
    cc = pl.BlockSpec((CHUNK, CHUNK), lambda b, c: (0, 0))
    c3 = pl.BlockSpec((CHUNK, 3 * CHUNK), lambda b, c: (0, 0))
    return pl.pallas_call(
        _hgrn_prompt_kernel,
        grid=(bsz, nc),
        in_specs=[blk(BLK_HG_Q), blk(BLK_HG_F), blk(BLK_HG_I), blk(BLK_HG_G), row, row, c3, cc],
        out_specs=[pl.BlockSpec((rows, BRANCH), lambda b, c: (b * nc + c, 0)),
                   pl.BlockSpec((1, HG_HEADS, HG_DK, HG_DV), lambda b, c: (b, 0, 0, 0))],
        out_shape=[jax.ShapeDtypeStruct((bsz * seq, BRANCH), BF16),
                   jax.ShapeDtypeStruct((bsz, HG_HEADS, HG_DK, HG_DV), F32)],
        scratch_shapes=[pltpu.VMEM((HG_HEADS, HG_DV, HG_DK), F32)],
        compiler_params=_cparams("parallel", "arbitrary"),
        name="hgrn_prompt",
    )(proj, proj, proj, proj, lb, nw, _tri_const(CHUNK), _level_const(CHUNK))


def _gla_prompt_kernel(q_ref, k_ref, v_ref, g_ref, sm_ref, wup_ref, bup_ref, nw_ref,
                       tri_ref, lvl_ref, y_ref, s_ref, st_scr):
    c = pl.program_id(1)

    @pl.when(c == 0)
    def _():
        st_scr[...] = jnp.zeros_like(st_scr)

    up = _dot(sm_ref[...].astype(BF16), wup_ref[...]) + bup_ref[...]
    log2_a = -_softplus(-up) * (LOG2E / GLA_TAU)
    items = [(slice(cc * CHUNK, (cc + 1) * CHUNK), slice(h * GLA_DK, (h + 1) * GLA_DK),
              slice(h * GLA_DV, (h + 1) * GLA_DV))
             for cc in range(q_ref.shape[0] // CHUNK) for h in range(GLA_HEADS)]
    outs, new_sts = _gla_chunk_heads(
        [q_ref[rs, ks] * (GLA_DK ** -0.5) for rs, ks, _ in items], [k_ref[rs, ks] for rs, ks, _ in items],
        lambda i: v_ref[items[i][0], items[i][2]], [log2_a[rs, ks] for rs, ks, _ in items],
        lambda h: st_scr[h], tri_ref[...], lvl_ref[...], GLA_HEADS)
    for h in range(GLA_HEADS):
        st_scr[h] = new_sts[h]
    for (rs, _, vs), o in zip(items, outs):
        y = _group_norm(o, nw_ref[:, vs], GLA_DV) * _silu(g_ref[rs, vs])
        y_ref[rs, vs] = y.astype(BF16)

    @pl.when(c == pl.num_programs(1) - 1)
    def _():
        for h in range(GLA_HEADS):
            s_ref[0, h] = st_scr[h].T


def _gla_prompt(proj, wup, bup, nw, bsz, seq):
    rows = _gla_rows(seq)
    nc = seq // rows
    blk = lambda w, j: pl.BlockSpec((rows, w), lambda b, c, j=j: (b * nc + c, j))
    const = lambda shape: pl.BlockSpec(shape, lambda b, c: (0,) * len(shape))
    return pl.pallas_call(
        _gla_prompt_kernel,
        grid=(bsz, nc),
        in_specs=[blk(GLA_KDIM, BLK_GL_Q), blk(GLA_KDIM, BLK_GL_K), blk(BRANCH, BLK_GL_V),
                  blk(BRANCH, BLK_GL_G), blk(128, BLK_SMALL),
                  const((128, GLA_KDIM)), const((1, GLA_KDIM)), const((1, BRANCH)),
                  const((CHUNK, 3 * CHUNK)), const((CHUNK, CHUNK))],
        out_specs=[pl.BlockSpec((rows, BRANCH), lambda b, c: (b * nc + c, 0)),
                   pl.BlockSpec((1, GLA_HEADS, GLA_DK, GLA_DV), lambda b, c: (b, 0, 0, 0))],
        out_shape=[jax.ShapeDtypeStruct((bsz * seq, BRANCH), BF16),
                   jax.ShapeDtypeStruct((bsz, GLA_HEADS, GLA_DK, GLA_DV), F32)],
        scratch_shapes=[pltpu.VMEM((GLA_HEADS, GLA_DV, GLA_DK), F32)],
        compiler_params=_cparams("parallel", "arbitrary"),
        name="gla_prompt",
    )(proj, proj, proj, proj, proj, wup, bup, nw, _tri_const(CHUNK), _level_const(CHUNK))


def _chunk_conv(x_ref, carry, w_ref, b_ref, first):
    c = x_ref.shape[0]
    last = CONV_W - 1

    @pl.when(first)
    def _():
        carry[...] = jnp.zeros_like(carry)

    x = x_ref[...]
    y = b_ref[...] + x * w_ref[last:last + 1, :]
    for s in range(1, CONV_W):
        y = y + pltpu.roll(x, s, 0) * w_ref[last - s:last - s + 1, :]
    ext = jnp.concatenate([carry[...], x[0:8, :]], axis=0)
    head = b_ref[...] + ext[8:16, :] * w_ref[last:last + 1, :]
    for s in range(1, CONV_W):
        head = head + ext[8 - s:16 - s, :] * w_ref[last - s:last - s + 1, :]
    tail = x[c - 8:c, :]
    carry[...] = tail
    return jnp.concatenate([head, y[8:, :]], axis=0), tail[8 - last:8, :]


def _rglru_gates(xc, wr_ref, br_ref, wi_ref, bi_ref, lam_ref):
    a_parts, u_parts = [], []
    for n in range(RG_BLOCKS):
        sl = slice(n * RG_BW, (n + 1) * RG_BW)
        xb = xc[:, sl]
        xb16 = xb.astype(BF16)
        r = _sigmoid(_dot(xb16, wr_ref[n]) + br_ref[:, sl])
        i = _sigmoid(_dot(xb16, wi_ref[n]) + bi_ref[:, sl])
        log_a = -RG_C * r * _softplus(-lam_ref[:, sl])
        a = jnp.exp(log_a)
        one_m_a2 = -jnp.tanh(log_a) * (a * a + 1.0)
        a_parts.append(a)
        u_parts.append(jnp.sqrt(jnp.maximum(one_m_a2, 0.0)) * (i * xb))
    return a_parts, u_parts


def _rglru_prompt_kernel(x_ref, g_ref, cw_ref, cb_ref, wr_ref, br_ref, wi_ref, bi_ref, lam_ref,
                         y_ref, h_ref, cs_ref, buf, h_scr):
    c = pl.program_id(1)
    first = c == 0

    @pl.when(first)
    def _():
        h_scr[...] = jnp.zeros_like(h_scr)

    xc, tail = _chunk_conv(x_ref, buf, cw_ref, cb_ref, first)
    cs_ref[0] = tail
    a_parts, u_parts = _rglru_gates(xc, wr_ref, br_ref, wi_ref, bi_ref, lam_ref)
    n_rows = xc.shape[0]
    ng = n_rows // 8
    sub = lax.broadcasted_iota(jnp.int32, (ng, 8, RG_BW), 1)
    for n in range(RG_BLOCKS):
        sl = slice(n * RG_BW, (n + 1) * RG_BW)
        a = a_parts[n].reshape(ng, 8, RG_BW)
        u = u_parts[n].reshape(ng, 8, RG_BW)
        s = 1
        while s < 8:
            keep = sub >= s
            a_sh = jnp.where(keep, pltpu.roll(a, s, 1), 1.0)
            u_sh = jnp.where(keep, pltpu.roll(u, s, 1), 0.0)
            u = a * u_sh + u
            a = a * a_sh
            s *= 2
        h = h_scr[:, sl]
        groups = []
        for j in range(ng):
            hj = a[j] * h + u[j]
            groups.append(hj)
            h = hj[7:8, :]
        h_scr[:, sl] = h
        y_ref[:, sl] = (jnp.concatenate(groups, axis=0) * _silu(g_ref[:, sl])).astype(BF16)
    h_ref[0] = h_scr[...]


def _rglru_prompt(proj, cw, cb, wr, br, wi, bi, lam, bsz, seq):
    nc = seq // CHUNK
    blk = lambda j: pl.BlockSpec((CHUNK, BRANCH), lambda b, c, j=j: (b * nc + c, j))
    const = lambda shape: pl.BlockSpec(shape, lambda b, c: (0,) * len(shape))
    return pl.pallas_call(
        _rglru_prompt_kernel,
        grid=(bsz, nc),
        in_specs=[blk(BLK_RG_X), blk(BLK_RG_G), const((CONV_W, BRANCH)), const((1, BRANCH)),
                  const((RG_BLOCKS, RG_BW, RG_BW)), const((1, BRANCH)),
                  const((RG_BLOCKS, RG_BW, RG_BW)), const((1, BRANCH)), const((1, BRANCH))],
        out_specs=[pl.BlockSpec((CHUNK, BRANCH), lambda b, c: (b * nc + c, 0)),
                   pl.BlockSpec((1, 1, BRANCH), lambda b, c: (b, 0, 0)),
                   pl.BlockSpec((1, CONV_W - 1, BRANCH), lambda b, c: (b, 0, 0))],
        out_shape=[jax.ShapeDtypeStruct((bsz * seq, BRANCH), BF16),
                   jax.ShapeDtypeStruct((bsz, 1, BRANCH), F32),
                   jax.ShapeDtypeStruct((bsz, CONV_W - 1, BRANCH), F32)],
        scratch_shapes=[pltpu.VMEM((8, BRANCH), F32), pltpu.VMEM((1, BRANCH), F32)],
        compiler_params=_cparams("parallel", "arbitrary"),
        name="rglru_prompt",
    )(proj, proj, cw, cb, wr, br, wi, bi, lam)


def _ssd_prompt_kernel(z_ref, x_ref, bc_ref, sm_ref, cwx_ref, cbx_ref, cwb_ref, cbb_ref,
                       dtb_ref, a_ref, d_ref, nw_ref, tri_ref, exp_ref,
                       y_ref, s_ref, cs_ref, st_scr, xbuf, bcbuf):
    c = pl.program_id(1)
    first = c == 0
    n_rows = x_ref.shape[0]
    gw = BRANCH // SSD_G

    @pl.when(first)
    def _():
        st_scr[...] = jnp.zeros_like(st_scr)

    xc, xtail = _chunk_conv(x_ref, xbuf, cwx_ref, cbx_ref, first)
    bcc, bctail = _chunk_conv(bc_ref, bcbuf, cwb_ref, cbb_ref, first)
    cs_ref[0, :, 0:BRANCH] = xtail
    cs_ref[0, :, BRANCH:SSD_CONV_DIM] = bctail
    xs = _silu(xc)
    bcs = _silu(bcc)

    tri = tri_ref[...]
    expand = exp_ref[...]
    dt = _softplus(sm_ref[...] + dtb_ref[...])
    cum = _sel_left(tri, dt * (a_ref[...] * LOG2E))
    cum_t = cum.T
    dt_x = _sel_right(dt, expand)
    cum_x = _sel_right(cum, expand)
    cum_last = cum_x[n_rows - 1:n_rows, :]
    dec_in = jnp.exp2(cum_x)
    xdt = xs * dt_x
    xw = (xdt * jnp.exp2(cum_last - cum_x)).astype(BF16)
    xdt16 = xdt.astype(BF16)

    t_idx = lax.broadcasted_iota(jnp.int32, (n_rows, n_rows), 0)
    s_idx = lax.broadcasted_iota(jnp.int32, (n_rows, n_rows), 1)
    causal = s_idx <= t_idx
    lane = lax.broadcasted_iota(jnp.int32, (n_rows, 2 * SSD_P), 1)

    y_parts = []
    for g in range(SSD_G):
        b_g = bcs[:, g * SSD_N:(g + 1) * SSD_N].astype(BF16)
        c_g = bcs[:, (SSD_G + g) * SSD_N:(SSD_G + g + 1) * SSD_N].astype(BF16)
        cb = _dot_nt(c_g, b_g)
        st_g = st_scr[:, g * gw:(g + 1) * gw]
        y_inter = _dot(c_g, st_g.astype(BF16)) * dec_in[:, g * gw:(g + 1) * gw]
        heads_per_g = SSD_HEADS // SSD_G
        for pair in range(heads_per_g // 2):
            h0 = g * heads_per_g + 2 * pair
            xp = xdt16[:, h0 * SSD_P:(h0 + 2) * SSD_P]
            outs = []
            for h in (h0, h0 + 1):
                col = DT_LANE + h
                seg = cum[:, col:col + 1] - cum_t[col:col + 1, :]
                lmat = jnp.where(causal, jnp.exp2(jnp.where(causal, seg, 0.0)), 0.0)
                outs.append(_dot((cb * lmat).astype(BF16), xp))
            y_parts.append(jnp.where(lane < SSD_P, outs[0], outs[1]))
        y_parts.append(y_inter)
        st_scr[:, g * gw:(g + 1) * gw] = (
            st_g * jnp.exp2(cum_last[:, g * gw:(g + 1) * gw])
            + _dot(bcs[:, g * SSD_N:(g + 1) * SSD_N].T.astype(BF16), xw[:, g * gw:(g + 1) * gw]))
    npair = SSD_HEADS // SSD_G // 2
    y = jnp.concatenate(
        [jnp.concatenate(y_parts[g * (npair + 1):g * (npair + 1) + npair], axis=1)
         + y_parts[g * (npair + 1) + npair] for g in range(SSD_G)], axis=1)
    y = (y + xs * d_ref[...]) * _silu(z_ref[...])
    y_ref[...] = _group_norm(y, nw_ref[...], gw).astype(BF16)

    @pl.when(c == pl.num_programs(1) - 1)
    def _():
        s_ref[0] = st_scr[...].T.reshape(SSD_HEADS, SSD_P, SSD_N)


def _ssd_expand_const(width):
    e = np.zeros((128, SSD_HEADS * width), np.float32)
    for h in range(SSD_HEADS):
        e[DT_LANE + h, h * width:(h + 1) * width] = 1.0
    return jnp.asarray(np.tile(e, (3, 1)), dtype=BF16)


def _ssd_prompt(proj, cwx, cbx, cwb, cbb, dtb, a_pad, d_x, nw, bsz, seq):
    nc = seq // CHUNK
    blk = lambda w, j: pl.BlockSpec((CHUNK, w), lambda b, c, j=j: (b * nc + c, j))
    const = lambda shape: pl.BlockSpec(shape, lambda b, c: (0,) * len(shape))
    return pl.pallas_call(
        _ssd_prompt_kernel,
        grid=(bsz, nc),
        in_specs=[blk(BRANCH, BLK_SS_Z), blk(BRANCH, BLK_SS_X), blk(SSD_BC, BLK_SS_BC),
                  blk(128, BLK_SMALL),
                  const((CONV_W, BRANCH)), const((1, BRANCH)), const((CONV_W, SSD_BC)),
                  const((1, SSD_BC)), const((1, 128)), const((1, 128)), const((1, BRANCH)),
                  const((1, BRANCH)), const((CHUNK, 3 * CHUNK)), const((3 * 128, BRANCH))],
        out_specs=[pl.BlockSpec((CHUNK, BRANCH), lambda b, c: (b * nc + c, 0)),
                   pl.BlockSpec((1, SSD_HEADS, SSD_P, SSD_N), lambda b, c: (b, 0, 0, 0)),
                   pl.BlockSpec((1, CONV_W - 1, SSD_CONV_DIM), lambda b, c: (b, 0, 0))],
        out_shape=[jax.ShapeDtypeStruct((bsz * seq, BRANCH), BF16),
                   jax.ShapeDtypeStruct((bsz, SSD_HEADS, SSD_P, SSD_N), F32),
                   jax.ShapeDtypeStruct((bsz, CONV_W - 1, SSD_CONV_DIM), F32)],
        scratch_shapes=[pltpu.VMEM((SSD_N, BRANCH), F32),
                        pltpu.VMEM((8, BRANCH), F32),
                        pltpu.VMEM((8, SSD_BC), F32)],
        compiler_params=_cparams("parallel", "arbitrary"),
        name="ssd_prompt",
    )(proj, proj, proj, proj, cwx, cbx, cwb, cbb, dtb, a_pad, d_x, nw,
      _tri_const(CHUNK), _ssd_expand_const(SSD_P))


def _step_conv(x, cs_ref, w_ref, b_ref, ncs_ref):
    y = b_ref[...] + x * w_ref[CONV_W - 1:CONV_W, :]
    for j in range(CONV_W - 1):
        y = y + cs_ref[j] * w_ref[j:j + 1, :]
    for j in range(CONV_W - 2):
        ncs_ref[j] = cs_ref[j + 1]
    ncs_ref[CONV_W - 2] = x
    return y


def _sample_pre_kernel(p_ref, lb_ref, rcs_ref, rh_ref, rcw_ref, rcb_ref, wr_ref, br_ref, wi_ref,
                       bi_ref, lam_ref, wup_ref, bup_ref, scs_ref, scw_ref, scb_ref, dtb_ref,
                       a_ref, exp_ref, expw_ref,
                       hq_ref, hk_ref, gq_ref, gd_ref, yrg_ref, nrh_ref, nrcs_ref,
                       sx_ref, sbc_ref, sdx_ref, sda_ref, nscs_ref):
    col = lambda blk, w: slice(blk * w, (blk + 1) * w)
    f = p_ref[:, col(BLK_HG_F, BRANCH)]
    lb = lb_ref[...]
    hq_ref[...] = _silu(p_ref[:, col(BLK_HG_Q, BRANCH)])
    hk_ref[...] = (1.0 - lb) * (1.0 - _sigmoid(f))
    sm = p_ref[:, col(BLK_SMALL, 128)]
    up = _dot(sm.astype(BF16), wup_ref[...]) + bup_ref[...]
    gq_ref[...] = p_ref[:, col(BLK_GL_Q, GLA_KDIM)] * (GLA_DK ** -0.5)
    gd_ref[...] = jnp.exp(-_softplus(-up) * (1.0 / GLA_TAU))
    xc = _step_conv(p_ref[:, col(BLK_RG_X, BRANCH)], rcs_ref, rcw_ref, rcb_ref, nrcs_ref)
    a_parts, u_parts = _rglru_gates(xc, wr_ref, br_ref, wi_ref, bi_ref, lam_ref)
    h = jnp.concatenate(a_parts, axis=1) * rh_ref[...] + jnp.concatenate(u_parts, axis=1)
    nrh_ref[...] = h
    yrg_ref[...] = (h * _silu(p_ref[:, col(BLK_RG_G, BRANCH)])).astype(BF16)
    xbc = jnp.concatenate([p_ref[:, col(BLK_SS_X, BRANCH)], p_ref[:, col(BLK_SS_BC, SSD_BC)]], axis=1)
    xbc = _silu(_step_conv(xbc, scs_ref, scw_ref, scb_ref, nscs_ref))
    xs = xbc[:, 0:BRANCH]
    sx_ref[...] = xs
    sbc_ref[...] = xbc[:, BRANCH:SSD_CONV_DIM]
    dt = _softplus(sm + dtb_ref[...])
    expand = exp_ref[...]
    sdx_ref[...] = xs * _sel_right(dt, expand)
    sda_ref[...] = jnp.exp(_sel_right(dt * a_ref[...], expw_ref[...]))


def _sample_pre(proj, lb, rcs, rh, rcw, rcb, wr, br, wi, bi, lam, wup, bup, scs, scw, scb,
                dtb, a_pad):
    nb = proj.shape[0]
    sd = lambda *shape, dt=F32: jax.ShapeDtypeStruct(shape, dt)
    return pl.pallas_call(
        _sample_pre_kernel,
        out_shape=[sd(nb, BRANCH), sd(nb, BRANCH),
                   sd(nb, GLA_KDIM), sd(nb, GLA_KDIM),
                   sd(nb, BRANCH, dt=BF16), sd(nb, BRANCH), sd(CONV_W - 1, nb, BRANCH),
                   sd(nb, BRANCH), sd(nb, SSD_BC), sd(nb, BRANCH), sd(nb, SSD_HEADS * SSD_N),
                   sd(CONV_W - 1, nb, SSD_CONV_DIM)],
        compiler_params=pltpu.CompilerParams(vmem_limit_bytes=VMEM_LIMIT),
        name="sample_pre",
    )(proj, lb, rcs, rh, rcw, rcb, wr, br, wi, bi, lam, wup, bup, scs, scw, scb, dtb, a_pad,
      _ssd_expand_const(SSD_P), _ssd_expand_const(SSD_N))


def _pad_t(x):
    r = x.shape[0]
    return jnp.concatenate([x, jnp.zeros((128 - r, 128), F32)], axis=0).T


STATE_UNROLL = 4


def _gla_state_kernel(*refs, heads, tied):
    so_ref, o_ref = refs[-2:]
    if tied:
        k_ref, v_ref, q_ref, s_ref = refs[:4]
    else:
        d_ref, k_ref, v_ref, q_ref, s_ref = refs[:5]
    dv = s_ref.shape[-1]

    def body(b, carry):
        kt_ = _pad_t(k_ref[b])
        dt_ = None if tied else _pad_t(d_ref[b])
        vr = v_ref[b]
        qr = q_ref[b]
        for h in range(heads):
            kb = jnp.broadcast_to(kt_[:, h:h + 1], (kt_.shape[0], dv))
            d = jnp.maximum(1.0 - kb, TINY) if tied else dt_[:, h:h + 1]
            s_new = d * s_ref[b, h] + kb * vr[h:h + 1, :]
            so_ref[b, h] = s_new
            q8 = jnp.broadcast_to(qr[h:h + 1, :], (8, qr.shape[1])).astype(BF16)
            o_ref[b, h:h + 1, :] = _dot(q8, s_new.astype(BF16))[0:1, :]
        return carry
    lax.fori_loop(0, s_ref.shape[0], body, 0, unroll=STATE_UNROLL)


def _state_call(kern, name, vec_args, vec_specs, s_all, so_prev, layer, o_shape, o_spec, bb):
    nb = s_all.shape[1]
    st = pl.BlockSpec((None, bb) + s_all.shape[2:], lambda i: (layer, i, 0, 0, 0))
    in_specs = list(vec_specs) + [st]
    args = list(vec_args) + [s_all]
    aliases = {}
    if so_prev is not None:
        in_specs.append(pl.BlockSpec(memory_space=pl.ANY))
        args.append(so_prev)
        aliases = {len(args) - 1: 0}
    return pl.pallas_call(
        kern,
        grid=(nb // bb,),
        in_specs=in_specs,
        out_specs=[st, o_spec],
        out_shape=[jax.ShapeDtypeStruct(s_all.shape, F32), o_shape],
        input_output_aliases=aliases,
        compiler_params=_cparams("parallel"),
        name=name,
    )(*args)


def _gla_state(d, k, v, q, s_all, so_prev, layer, *, bb):
    _, nb, heads, dk, dv = s_all.shape
    vec = lambda w: pl.BlockSpec((bb, heads, w), lambda i: (i, 0, 0))
    tied = d is None
    args = (k, v, q) if tied else (d, k, v, q)
    specs = (vec(dk), vec(dv), vec(dk)) if tied else (vec(dk), vec(dk), vec(dv), vec(dk))
    return _state_call(functools.partial(_gla_state_kernel, heads=heads, tied=tied),
                       "hgrn_state" if tied else "gla_state", args, specs, s_all, so_prev, layer,
                       jax.ShapeDtypeStruct((nb, heads, dv), F32), vec(dv), bb)


def _ssd_state_kernel(da_ref, dx_ref, b_ref, c_ref, s_ref, *rest):
    so_ref, y_ref = rest[-2:]
    hpg = SSD_HEADS // SSD_G

    def body(b, carry):
        ar = da_ref[b]
        xt_ = _pad_t(dx_ref[b])
        br = b_ref[b]
        cr = c_ref[b]
        for h in range(SSD_HEADS):
            g = h // hpg
            rows = slice((h % 2) * SSD_P, (h % 2 + 1) * SSD_P)
            j = h // 2
            s_new = ar[h:h + 1, :] * s_ref[b, h] + xt_[rows, j:j + 1] * br[g:g + 1, :]
            so_ref[b, h] = s_new
            c8 = jnp.broadcast_to(cr[g:g + 1, :], (8, SSD_N)).astype(BF16)
            y_ref[b, h:h + 1, :] = _dot_nt(c8, s_new.astype(BF16))[0:1, :]
        return carry
    lax.fori_loop(0, s_ref.shape[0], body, 0, unroll=STATE_UNROLL)


def _ssd_state(da, dx, bv, cv, s_all, so_prev, layer, *, bb):
    nb = s_all.shape[1]
    vec = lambda r, w: pl.BlockSpec((bb, r, w), lambda i: (i, 0, 0))
    return _state_call(_ssd_state_kernel, "ssd_state", (da, dx, bv, cv),
                       (vec(SSD_HEADS, SSD_N), vec(8, 128), vec(SSD_G, SSD_N), vec(SSD_G, SSD_N)),
                       s_all, so_prev, layer,
                       jax.ShapeDtypeStruct((nb, SSD_HEADS, SSD_P), F32), vec(SSD_HEADS, SSD_P), bb)


def _sample_post_kernel(p_ref, ohg_ref, ogl_ref, yss_ref, sx_ref, d_ref, hnw_ref, gnw_ref,
                        snw_ref, yhg_ref, ygl_ref, yso_ref):
    col = lambda blk: slice(blk * BRANCH, (blk + 1) * BRANCH)
    yhg_ref[...] = (_group_norm(ohg_ref[...], hnw_ref[...], HG_DV)
                    * _silu(p_ref[:, col(BLK_HG_G)])).astype(BF16)
    ygl_ref[...] = (_group_norm(ogl_ref[...], gnw_ref[...], GLA_DV)
                    * _silu(p_ref[:, col(BLK_GL_G)])).astype(BF16)
    y = (yss_ref[...] + sx_ref[...] * d_ref[...]) * _silu(p_ref[:, col(BLK_SS_Z)])
    yso_ref[...] = _group_norm(y, snw_ref[...], BRANCH // SSD_G).astype(BF16)


def _sample_post(proj, ohg, ogl, yss, sx, d_x, hnw, gnw, snw):
    nb = proj.shape[0]
    out = jax.ShapeDtypeStruct((nb, BRANCH), BF16)
    return pl.pallas_call(
        _sample_post_kernel, out_shape=[out, out, out],
        compiler_params=pltpu.CompilerParams(vmem_limit_bytes=VMEM_LIMIT),
        name="sample_post",
    )(proj, ohg, ogl, yss, sx, d_x, hnw, gnw, snw)


def _tail_kernel(w_ref, o_ref):
    cols = w_ref.shape[1]
    n_a = ORIG_SSD_Z - ORIG_GLA_A
    n_zx = ORIG_SSD_DT - ORIG_SSD_Z
    n_dt = N_IN - ORIG_SSD_DT
    o_ref[0:n_zx, :] = w_ref[n_a:n_a + n_zx, :].astype(BF16)
    o_ref[n_zx:n_zx + n_a, :] = w_ref[0:n_a, :].astype(BF16)
    o_ref[n_zx + n_a:n_zx + n_a + n_dt, :] = w_ref[n_a + n_zx:n_a + n_zx + n_dt, :].astype(BF16)
    o_ref[n_zx + n_a + n_dt:N_TAIL, :] = jnp.zeros((N_TAIL - n_zx - n_a - n_dt, cols), BF16)


def _prep_w_tail(w_in_t, *, cb=256):
    depth, _, d = w_in_t.shape
    return pl.pallas_call(
        _tail_kernel,
        grid=(depth, d // cb),
        in_specs=[pl.BlockSpec((None, N_TAIL, cb), lambda l, i: (l, N_MAIN // N_TAIL, i))],
        out_specs=pl.BlockSpec((None, N_TAIL, cb), lambda l, i: (l, 0, i)),
        out_shape=jax.ShapeDtypeStruct((depth, N_TAIL, d), BF16),
        compiler_params=_cparams("parallel", "parallel"),
        name="w_tail",
    )(w_in_t)


def _pad_lanes(v, start, width=128):
    out = jnp.zeros((v.shape[0], 1, width), F32)
    return out.at[:, 0, start:start + v.shape[1]].set(v.astype(F32))


def kernel(x_prompt, x_sample, state_hgrn, state_rglru, state_rglru_conv, state_gla, state_ssd, state_ssd_conv, rms_in, w_in, hgrn_lower_bounds, hgrn_norm, rglru_conv_w, rglru_conv_b, rglru_w_r, rglru_b_r, rglru_w_i, rglru_b_i, rglru_lambda, gla_w_up, gla_b_up, gla_norm, ssd_conv_w, ssd_conv_b, ssd_dt_bias, ssd_a_log, ssd_d, ssd_norm, w_out, rms_final):
    bsz, seq, _ = x_prompt.shape
    nb = x_sample.shape[0]
    row = lambda v: v.reshape(DEPTH, 1, -1).astype(F32)

    lb_all = _lower_bounds(hgrn_lower_bounds.astype(F32)).reshape(DEPTH, 1, BRANCH)
    w_in = jnp.swapaxes(w_in.astype(F32), 1, 2)
    w_tail = _prep_w_tail(w_in)
    w_out16 = w_out.astype(BF16)
    wr16 = rglru_w_r.astype(BF16)
    wi16 = rglru_w_i.astype(BF16)
    wup16 = jnp.concatenate(
        [gla_w_up, jnp.zeros((DEPTH, 128 - GLA_RANK, GLA_KDIM), gla_w_up.dtype)], axis=1).astype(BF16)
    dtb = _pad_lanes(ssd_dt_bias, DT_LANE)
    a_pad = _pad_lanes(-jnp.exp(ssd_a_log.astype(F32)), DT_LANE)
    d_x = jnp.repeat(ssd_d.astype(F32), SSD_P, axis=-1).reshape(DEPTH, 1, BRANCH)
    rms_in_r, hnw, gnw, snw = row(rms_in), row(hgrn_norm), row(gla_norm), row(ssd_norm)
    rcb, br, bi, lam, bup = (row(rglru_conv_b), row(rglru_b_r), row(rglru_b_i),
                             row(rglru_lambda), row(gla_b_up))
    scb = row(ssd_conv_b)

    xp = x_prompt.reshape(bsz * seq, D_MODEL)
    xs = x_sample.reshape(nb, D_MODEL)
    n_p = bsz * seq
    n_all = n_p + nb
    assert n_p % nb == 0
    tm_in = 1040 if n_all % 1040 == 0 else n_all
    tm_out = 512 if n_p % 512 == 0 else CHUNK
    bb = 8 if nb % 8 == 0 else 1
    rf = rms_final.reshape(1, D_MODEL).astype(F32)

    h_all = _rmsnorm(xp, rms_in_r[0], BF16, tm=tm_out, shared=(n_all, 0, None))
    h_all = _rmsnorm(xs, rms_in_r[0], BF16, tm=nb, shared=(n_all, n_p, h_all))
    outs_p = [[] for _ in range(6)]
    outs_s = [[] for _ in range(3)]
    ns_hg = ns_gl = ns_ss = None
    for l in range(DEPTH):
        scw = ssd_conv_w[l].astype(F32)
        last = l + 1 == DEPTH
        next_w, next_dt = (rf, F32) if last else (rms_in_r[l + 1], BF16)
        proj = _inproj(h_all, w_in, w_tail, l, tm=tm_in, tn=1024)
        proj_s = proj[n_p:]
        y_hg, s_hg = _hgrn_prompt(proj, lb_all[l], hnw[l], bsz, seq)
        y_rg, s_rg, s_rgc = _rglru_prompt(proj, rglru_conv_w[l].astype(F32), rcb[l], wr16[l], br[l],
                                          wi16[l], bi[l], lam[l], bsz, seq)
        y_gl, s_gl = _gla_prompt(proj, wup16[l], bup[l], gnw[l], bsz, seq)
        y_ss, s_ss, s_ssc = _ssd_prompt(proj, scw[:, :BRANCH], scb[l][:, :BRANCH], scw[:, BRANCH:],
                                        scb[l][:, BRANCH:], dtb[l], a_pad[l], d_x[l], snw[l], bsz, seq)
        if last:
            xp, hp = _outproj((y_hg, y_rg, y_gl, y_ss), w_out16[l], xp, next_w, next_dt,
                              tm=tm_out // 2)
        else:
            xp, h_all = _outproj((y_hg, y_rg, y_gl, y_ss), w_out16[l], xp, next_w, next_dt,
                                 tm=tm_out, shared=(n_all, 0, None))
        for lst, s in zip(outs_p, (s_hg, s_rg.reshape(bsz, BRANCH), s_rgc, s_gl, s_ss, s_ssc)):
            lst.append(s)

        (hq, hk, gq, gd, yrg_s, nrh, nrcs, sx, sbc, sdx, sda, nscs) = _sample_pre(
            proj_s, lb_all[l], jnp.swapaxes(state_rglru_conv[l], 0, 1), state_rglru[l],
            rglru_conv_w[l].astype(F32), rcb[l], wr16[l], br[l], wi16[l], bi[l], lam[l],
            wup16[l], bup[l], jnp.swapaxes(state_ssd_conv[l], 0, 1), scw, scb[l], dtb[l], a_pad[l])
        hv = proj_s[:, BLK_HG_I * BRANCH:(BLK_HG_I + 1) * BRANCH]
        gk = proj_s[:, BLK_GL_K * GLA_KDIM:(BLK_GL_K + 1) * GLA_KDIM]
        gv = proj_s[:, BLK_GL_V * BRANCH:(BLK_GL_V + 1) * BRANCH]
        hsh = lambda a: a.reshape(nb, HG_HEADS, -1)
        gsh = lambda a: a.reshape(nb, GLA_HEADS, -1)
        ns_hg, o_hg = _gla_state(None, hsh(hk), hsh(hv), hsh(hq), state_hgrn, ns_hg, l, bb=bb)
        ns_gl, o_gl = _gla_state(gsh(gd), gsh(gk), gsh(gv), gsh(gq), state_gla, ns_gl, l, bb=bb)
        ns_ss, y_ssr = _ssd_state(sda.reshape(nb, SSD_HEADS, SSD_N), sdx.reshape(nb, 8, 128),
                                  sbc[:, :SSD_G * SSD_N].reshape(nb, SSD_G, SSD_N),
                                  sbc[:, SSD_G * SSD_N:].reshape(nb, SSD_G, SSD_N),
                                  state_ssd, ns_ss, l, bb=bb)
        yhg_s, ygl_s, yss_s = _sample_post(proj_s, o_hg.reshape(nb, BRANCH), o_gl.reshape(nb, BRANCH),
                                           y_ssr.reshape(nb, BRANCH), sx, d_x[l], hnw[l], gnw[l], snw[l])
        if last:
            xs, hs = _outproj((yhg_s, yrg_s, ygl_s, yss_s), w_out16[l], xs, next_w, next_dt, tm=nb)
        else:
            xs, h_all = _outproj((yhg_s, yrg_s, ygl_s, yss_s), w_out16[l], xs, next_w, next_dt, tm=nb,
                                 shared=(n_all, n_p, h_all))
        for lst, s in zip(outs_s, (nrh, jnp.swapaxes(nrcs, 0, 1), jnp.swapaxes(nscs, 0, 1))):
            lst.append(s)

    y_prompt = hp.reshape(bsz, seq, D_MODEL)
    y_sample = hs.reshape(nb, 1, D_MODEL)
    s_rg, s_rgc, s_ssc = (jnp.stack(l) for l in outs_s)
    return ((y_prompt, y_sample) + tuple(jnp.stack(l) for l in outs_p)
            + (ns_hg, s_rg, s_rgc, ns_gl, ns_ss, s_ssc))
```

```python
import functools
import math

import numpy as np
import jax
import jax.numpy as jnp
from jax import lax
from jax.experimental import pallas as pl
from jax.experimental.pallas import tpu as pltpu

F32 = jnp.float32
BF16 = jnp.bfloat16

D_MODEL = 2048
DEPTH = 4
BRANCH = 1024
D_MIX = 4 * BRANCH
CONV_W = 4
EPS = 1e-6
TINY = 1e-30

HG_HEADS, HG_DK, HG_DV = 8, 128, 128
RG_BLOCKS, RG_BW, RG_C = 8, 128, 8.0
GLA_HEADS, GLA_DK, GLA_DV, GLA_RANK, GLA_TAU = 4, 128, 256, 16, 16.0
GLA_KDIM = GLA_HEADS * GLA_DK
SSD_HEADS, SSD_P, SSD_G, SSD_N = 16, 64, 2, 128
SSD_BC = 2 * SSD_G * SSD_N
SSD_CONV_DIM = BRANCH + SSD_BC

ORIG_GLA_A = 9216
ORIG_SSD_Z = 9232
ORIG_SSD_DT = 11792
N_IN = 11808
N_PROJ = 12288
COL_SMALL = 11776
DT_LANE = 16
BLK_HG_Q, BLK_HG_F, BLK_HG_I, BLK_HG_G = 0, 1, 2, 3
BLK_RG_X, BLK_RG_G = 4, 5
BLK_GL_Q, BLK_GL_K = 12, 13
BLK_GL_V, BLK_GL_G = 7, 8
BLK_SS_Z, BLK_SS_X = 9, 10
BLK_SS_BC = 22
BLK_SMALL = COL_SMALL // 128

CHUNK = 128
SUB = 16
LOG2E = math.log2(math.e)
VMEM_LIMIT = 52 * 1024 * 1024


def _cparams(*sem):
    return pltpu.CompilerParams(dimension_semantics=sem, vmem_limit_bytes=VMEM_LIMIT)


def _sigmoid(x):
    return 0.5 * jnp.tanh(0.5 * x) + 0.5


def _silu(x):
    return x * _sigmoid(x)


def _softplus(x):
    return jnp.maximum(x, 0.0) + jnp.log(1.0 + jnp.exp(-jnp.abs(x)))


def _dot(a, b):
    return jnp.dot(a, b, preferred_element_type=F32)


def _dot_nt(a, b):
    return lax.dot_general(a, b, (((1,), (1,)), ((), ())), preferred_element_type=F32)


def _split3(a):
    a0 = a.astype(BF16)
    r1 = a - a0.astype(F32)
    a1 = r1.astype(BF16)
    a2 = (r1 - a1.astype(F32)).astype(BF16)
    return a0, a1, a2


def _sel_left(sel3, x):
    return _dot(sel3, jnp.concatenate(_split3(x), axis=0))


def _sel_right(x, sel3):
    return _dot(jnp.concatenate(_split3(x), axis=1), sel3)


def _group_norm(y, w, width):
    parts = []
    for g in range(y.shape[1] // width):
        yg = y[:, g * width:(g + 1) * width]
        ms = jnp.mean(yg * yg, axis=-1, keepdims=True)
        parts.append(yg * lax.rsqrt(ms + EPS))
    out = parts[0] if len(parts) == 1 else jnp.concatenate(parts, axis=1)
    return out * w


def _tri_const(c):
    return jnp.asarray(np.tile(np.tril(np.ones((c, c), np.float32)), (1, 3)), dtype=BF16)


def _level_const(c):
    t = np.arange(c)[:, None]
    s = np.arange(c)[None, :]
    lvl = np.zeros((c, c), np.int32)
    lvl[(t // SUB == s // SUB) & (s <= t)] = 1
    h, code = SUB, 2
    while h < c:
        m = (t // (2 * h) == s // (2 * h)) & (t % (2 * h) >= h) & (s % (2 * h) < h)
        lvl[m] = code
        h *= 2
        code += 1
    return jnp.asarray(lvl)


def _lb_kernel(p_ref, o_ref):
    x = p_ref[...]
    m = jnp.max(x, axis=0, keepdims=True)
    e = jnp.exp(x - m)
    p = e / jnp.sum(e, axis=0, keepdims=True)
    acc = jnp.zeros_like(p[0:1])
    rows = [acc]
    for l in range(1, DEPTH):
        acc = acc + p[l:l + 1]
        rows.append(acc)
    o_ref[...] = jnp.concatenate(rows, axis=0)


def _lower_bounds(param):
    return pl.pallas_call(
        _lb_kernel, out_shape=jax.ShapeDtypeStruct(param.shape, F32), name="hgrn_lb")(param)


N_MAIN = ORIG_GLA_A
N_TAIL = N_PROJ - N_MAIN


def _inproj_kernel(h_ref, w_ref, wt_ref, o_ref, wb_scr, *, n_main, rb):
    j = pl.program_id(0)

    @pl.when(pl.program_id(1) == 0)
    def _():
        @pl.when(j < n_main)
        def _():
            def body(i, carry):
                r = pl.multiple_of(i * rb, rb)
                wb_scr[pl.ds(r, rb), :] = w_ref[pl.ds(r, rb), :].astype(BF16)
                return carry
            lax.fori_loop(0, wb_scr.shape[0] // rb, body, 0)

        @pl.when(j >= n_main)
        def _():
            wb_scr[...] = wt_ref[...]

    o_ref[...] = _dot_nt(h_ref[...], wb_scr[...])


def _inproj(h, w_in_t, w_tail_t, layer, *, tm, tn):
    m, d = h.shape
    n_main = N_MAIN // tn
    return pl.pallas_call(
        functools.partial(_inproj_kernel, n_main=n_main, rb=128),
        grid=(N_PROJ // tn, m // tm),
        in_specs=[pl.BlockSpec((tm, d), lambda j, i: (i, 0)),
                  pl.BlockSpec((None, tn, d), lambda j, i: (layer, jnp.minimum(j, n_main - 1), 0)),
                  pl.BlockSpec((None, tn, d), lambda j, i: (layer, jnp.maximum(j - n_main, 0), 0))],
        out_specs=pl.BlockSpec((tm, tn), lambda j, i: (i, j)),
        out_shape=jax.ShapeDtypeStruct((m, N_PROJ), F32),
        scratch_shapes=[pltpu.VMEM((tn, d), BF16)],
        compiler_params=_cparams("arbitrary", "arbitrary"),
        name="inproj",
    )(h, w_in_t, w_tail_t)


def _outproj_kernel(y0_ref, y1_ref, y2_ref, y3_ref, w_ref, x_ref, nw_ref, *rest, rb):
    xo_ref, ho_ref = rest[-2:]
    acc = x_ref[...]
    for g, y_ref in enumerate((y0_ref, y1_ref, y2_ref, y3_ref)):
        acc = acc + _dot(y_ref[...], w_ref[g * BRANCH:(g + 1) * BRANCH, :])
    xo_ref[...] = acc

    def body(i, carry):
        r = pl.multiple_of(i * rb, rb)
        x = xo_ref[pl.ds(r, rb), :]
        ms = jnp.mean(x * x, axis=-1, keepdims=True)
        ho_ref[pl.ds(r, rb), :] = (x * lax.rsqrt(ms + EPS) * nw_ref[...]).astype(ho_ref.dtype)
        return carry
    lax.fori_loop(0, xo_ref.shape[0] // rb, body, 0)


def _shared_rows(m, tm, shared):
    if shared is None:
        return m, 0, (), ()
    total, row0, buf = shared
    if buf is None:
        return total, row0 // tm, (), ()
    return total, row0 // tm, (buf,), (pl.BlockSpec(memory_space=pl.ANY),)


def _outproj(ys, w, x, norm_w, norm_dtype, *, tm, shared=None):
    m, d = x.shape
    yspec = pl.BlockSpec((tm, BRANCH), lambda i: (i, 0))
    xspec = pl.BlockSpec((tm, d), lambda i: (i, 0))
    rows, off, extra, extra_specs = _shared_rows(m, tm, shared)
    return pl.pallas_call(
        functools.partial(_outproj_kernel, rb=min(tm, 64)),
        grid=(m // tm,),
        in_specs=[yspec, yspec, yspec, yspec,
                  pl.BlockSpec((D_MIX, d), lambda i: (0, 0), pipeline_mode=pl.Buffered(1)),
                  xspec, pl.BlockSpec((1, d), lambda i: (0, 0)), *extra_specs],
        out_specs=[xspec, pl.BlockSpec((tm, d), lambda i: (i + off, 0))],
        out_shape=[jax.ShapeDtypeStruct((m, d), F32), jax.ShapeDtypeStruct((rows, d), norm_dtype)],
        input_output_aliases={7: 1} if extra else {},
        compiler_params=_cparams("parallel"),
        name="outproj",
    )(*ys, w, x, norm_w, *extra)


def _rmsnorm_kernel(x_ref, w_ref, *rest):
    o_ref = rest[-1]
    x = x_ref[...]
    ms = jnp.mean(x * x, axis=-1, keepdims=True)
    o_ref[...] = (x * lax.rsqrt(ms + EPS) * w_ref[...]).astype(o_ref.dtype)


def _rmsnorm(x, w, out_dtype, *, tm, shared=None):
    m, d = x.shape
    rows, off, extra, extra_specs = _shared_rows(m, tm, shared)
    return pl.pallas_call(
        _rmsnorm_kernel,
        grid=(m // tm,),
        in_specs=[pl.BlockSpec((tm, d), lambda i: (i, 0)),
                  pl.BlockSpec((1, d), lambda i: (0, 0)), *extra_specs],
        out_specs=pl.BlockSpec((tm, d), lambda i: (i + off, 0)),
        out_shape=jax.ShapeDtypeStruct((rows, d), out_dtype),
        input_output_aliases={2: 0} if extra else {},
        compiler_params=_cparams("parallel"),
        name="rmsnorm",
    )(x, w, *extra)


def _gla_chunk_heads(qs, ks, get_v, log2fs, get_st, tri, lvl, n_heads):
    n = len(qs)
    c, width = qs[0].shape
    bs = [_sel_left(tri, lf) for lf in log2fs]

    def ref_rows(b, rows, span):
        return jnp.concatenate(
            [jnp.broadcast_to(b[r:r + 1, :], (span, width)) for r in rows], axis=0)

    def rows_only(x, lo, hi):
        parts = [jnp.zeros((lo, width), BF16), x[lo:hi, :], jnp.zeros((c - hi, width), BF16)]
        return jnp.concatenate([p for p in parts if p.shape[0]], axis=0)

    s_diag = []
    for h in range(n):
        ed = bs[h] - ref_rows(bs[h], range(SUB // 2, c, SUB), SUB)
        s_diag.append(_dot_nt((qs[h] * jnp.exp2(ed)).astype(BF16), (ks[h] * jnp.exp2(-ed)).astype(BF16)))

    s_off = []
    for h in range(n):
        lq, lk = [], []
        half = SUB
        while half < c:
            ref = ref_rows(bs[h], range(half - 1, c, 2 * half), 2 * half)
            mixed = jnp.concatenate(
                [(ks[h] if (r // half) % 2 == 0 else qs[h])[r:r + half, :] for r in range(0, c, half)],
                axis=0)
            x = (mixed * jnp.exp2(-jnp.abs(bs[h] - ref))).astype(BF16)
            for r0 in range(0, c, 2 * half):
                lk.append(rows_only(x, r0, r0 + half))
                lq.append(rows_only(x, r0 + half, r0 + 2 * half))
            half *= 2
        s_off.append(_dot_nt(jnp.concatenate(lq, axis=1), jnp.concatenate(lk, axis=1)))

    outs, sts = [], {}
    for i in range(n):
        head = i % n_heads
        scores = jnp.where(lvl == 1, s_diag[i], s_off[i])
        b = bs[i]
        b_last = b[c - 1:c, :]
        q_in = (qs[i] * jnp.exp2(b)).astype(BF16)
        k_end = (ks[i] * jnp.exp2(b_last - b)).astype(BF16)
        v = get_v(i)
        st = sts[head] if head in sts else get_st(head)
        outs.append(_dot(scores.astype(BF16), v.astype(BF16)) + _dot_nt(q_in, st.astype(BF16)))
        sts[head] = st * jnp.exp2(b_last) + _dot(v.T.astype(BF16), k_end)
    return outs, [sts[h] for h in range(n_heads)]


def _hgrn_prompt_kernel(q_ref, f_ref, i_ref, g_ref, lb_ref, nw_ref, tri_ref, lvl_ref,
                        y_ref, s_ref, st_scr):
    c = pl.program_id(1)

    @pl.when(c == 0)
    def _():
        st_scr[...] = jnp.zeros_like(st_scr)

    items = [(slice(cc * CHUNK, (cc + 1) * CHUNK), slice(h * HG_DK, (h + 1) * HG_DK))
             for cc in range(q_ref.shape[0] // CHUNK) for h in range(HG_HEADS)]
    qs, ks, log2fs = [], [], []
    for rs, sl in items:
        k = (0.5 - 0.5 * lb_ref[:, sl]) * (1.0 - jnp.tanh(0.5 * f_ref[rs, sl]))
        ks.append(k)
        log2fs.append(jnp.log2(jnp.maximum(1.0 - k, TINY)))
        qs.append(_silu(q_ref[rs, sl]))
    outs, new_sts = _gla_chunk_heads(qs, ks, lambda i: i_ref[items[i][0], items[i][1]], log2fs,
                                     lambda h: st_scr[h], tri_ref[...], lvl_ref[...], HG_HEADS)
    for h in range(HG_HEADS):
        st_scr[h] = new_sts[h]
    for (rs, sl), o in zip(items, outs):
        y = _group_norm(o, nw_ref[:, sl], HG_DV) * _silu(g_ref[rs, sl])
        y_ref[rs, sl] = y.astype(BF16)

    @pl.when(c == pl.num_programs(1) - 1)
    def _():
        for h in range(HG_HEADS):
            s_ref[0, h] = st_scr[h].T


def _gla_rows(seq):
    for n in (4, 2):
        if seq % (n * CHUNK) == 0:
            return n * CHUNK
    return CHUNK


def _hgrn_prompt(proj, lb, nw, bsz, seq):
    rows = _gla_rows(seq)
    nc = seq // rows
    blk = lambda j: pl.BlockSpec((rows, BRANCH), lambda b, c, j=j: (b * nc + c, j))
    row = pl.BlockSpec((1, BRANCH), lambda b, c: (0, 0))
    cc = pl.BlockSpec((CHUNK, CHUNK), lambda b, c: (0, 0))
    c3 = pl.BlockSpec((CHUNK, 3 * CHUNK), lambda b, c: (0, 0))
    return pl.pallas_call(
        _hgrn_prompt_kernel,
        grid=(bsz, nc),
        in_specs=[blk(BLK_HG_Q), blk(BLK_HG_F), blk(BLK_HG_I), blk(BLK_HG_G), row, row, c3, cc],
        out_specs=[pl.BlockSpec((rows, BRANCH), lambda b, c: (b * nc + c, 0)),
                   pl.BlockSpec((1, HG_HEADS, HG_DK, HG_DV), lambda b, c: (b, 0, 0, 0))],
        out_shape=[jax.ShapeDtypeStruct((bsz * seq, BRANCH), BF16),
                   jax.ShapeDtypeStruct((bsz, HG_HEADS, HG_DK, HG_DV), F32)],
        scratch_shapes=[pltpu.VMEM((HG_HEADS, HG_DV, HG_DK), F32)],
        compiler_params=_cparams("parallel", "arbitrary"),
        name="hgrn_prompt",
    )(proj, proj, proj, proj, lb, nw, _tri_const(CHUNK), _level_const(CHUNK))


def _gla_prompt_kernel(q_ref, k_ref, v_ref, g_ref, sm_ref, wup_ref, bup_ref, nw_ref,
                       tri_ref, lvl_ref, y_ref, s_ref, st_scr):
    c = pl.program_id(1)

    @pl.when(c == 0)
    def _():
        st_scr[...] = jnp.zeros_like(st_scr)

    up = _dot(sm_ref[...].astype(BF16), wup_ref[...]) + bup_ref[...]
    log2_a = -_softplus(-up) * (LOG2E / GLA_TAU)
    items = [(slice(cc * CHUNK, (cc + 1) * CHUNK), slice(h * GLA_DK, (h + 1) * GLA_DK),
              slice(h * GLA_DV, (h + 1) * GLA_DV))
             for cc in range(q_ref.shape[0] // CHUNK) for h in range(GLA_HEADS)]
    outs, new_sts = _gla_chunk_heads(
        [q_ref[rs, ks] * (GLA_DK ** -0.5) for rs, ks, _ in items], [k_ref[rs, ks] for rs, ks, _ in items],
        lambda i: v_ref[items[i][0], items[i][2]], [log2_a[rs, ks] for rs, ks, _ in items],
        lambda h: st_scr[h], tri_ref[...], lvl_ref[...], GLA_HEADS)
    for h in range(GLA_HEADS):
        st_scr[h] = new_sts[h]
    for (rs, _, vs), o in zip(items, outs):
        y = _group_norm(o, nw_ref[:, vs], GLA_DV) * _silu(g_ref[rs, vs])
        y_ref[rs, vs] = y.astype(BF16)

    @pl.when(c == pl.num_programs(1) - 1)
    def _():
        for h in range(GLA_HEADS):
            s_ref[0, h] = st_scr[h].T


def _gla_prompt(proj, wup, bup, nw, bsz, seq):
    rows = _gla_rows(seq)
    nc = seq // rows
    blk = lambda w, j: pl.BlockSpec((rows, w), lambda b, c, j=j: (b * nc + c, j))
    const = lambda shape: pl.BlockSpec(shape, lambda b, c: (0,) * len(shape))
    return pl.pallas_call(
        _gla_prompt_kernel,
        grid=(bsz, nc),
        in_specs=[blk(GLA_KDIM, BLK_GL_Q), blk(GLA_KDIM, BLK_GL_K), blk(BRANCH, BLK_GL_V),
                  blk(BRANCH, BLK_GL_G), blk(128, BLK_SMALL),
                  const((128, GLA_KDIM)), const((1, GLA_KDIM)), const((1, BRANCH)),
                  const((CHUNK, 3 * CHUNK)), const((CHUNK, CHUNK))],
        out_specs=[pl.BlockSpec((rows, BRANCH), lambda b, c: (b * nc + c, 0)),
                   pl.BlockSpec((1, GLA_HEADS, GLA_DK, GLA_DV), lambda b, c: (b, 0, 0, 0))],
        out_shape=[jax.ShapeDtypeStruct((bsz * seq, BRANCH), BF16),
                   jax.ShapeDtypeStruct((bsz, GLA_HEADS, GLA_DK, GLA_DV), F32)],
        scratch_shapes=[pltpu.VMEM((GLA_HEADS, GLA_DV, GLA_DK), F32)],
        compiler_params=_cparams("parallel", "arbitrary"),
        name="gla_prompt",
    )(proj, proj, proj, proj, proj, wup, bup, nw, _tri_const(CHUNK), _level_const(CHUNK))


def _chunk_conv(x_ref, carry, w_ref, b_ref, first):
    c = x_ref.shape[0]
    last = CONV_W - 1

    @pl.when(first)
    def _():
        carry[...] = jnp.zeros_like(carry)

    x = x_ref[...]
    y = b_ref[...] + x * w_ref[last:last + 1, :]
    for s in range(1, CONV_W):
        y = y + pltpu.roll(x, s, 0) * w_ref[last - s:last - s + 1, :]
    ext = jnp.concatenate([carry[...], x[0:8, :]], axis=0)
    head = b_ref[...] + ext[8:16, :] * w_ref[last:last + 1, :]
    for s in range(1, CONV_W):
        head = head + ext[8 - s:16 - s, :] * w_ref[last - s:last - s + 1, :]
    tail = x[c - 8:c, :]
    carry[...] = tail
    return jnp.concatenate([head, y[8:, :]], axis=0), tail[8 - last:8, :]


def _rglru_gates(xc, wr_ref, br_ref, wi_ref, bi_ref, lam_ref):
    a_parts, u_parts = [], []
    for n in range(RG_BLOCKS):
        sl = slice(n * RG_BW, (n + 1) * RG_BW)
        xb = xc[:, sl]
        xb16 = xb.astype(BF16)
        r = _sigmoid(_dot(xb16, wr_ref[n]) + br_ref[:, sl])
        i = _sigmoid(_dot(xb16, wi_ref[n]) + bi_ref[:, sl])
        log_a = -RG_C * r * _softplus(-lam_ref[:, sl])
        a = jnp.exp(log_a)
        one_m_a2 = -jnp.tanh(log_a) * (a * a + 1.0)
        a_parts.append(a)
        u_parts.append(jnp.sqrt(jnp.maximum(one_m_a2, 0.0)) * (i * xb))
    return a_parts, u_parts


def _rglru_prompt_kernel(x_ref, g_ref, cw_ref, cb_ref, wr_ref, br_ref, wi_ref, bi_ref, lam_ref,
                         y_ref, h_ref, cs_ref, buf, h_scr):
    c = pl.program_id(1)
    first = c == 0

    @pl.when(first)
    def _():
        h_scr[...] = jnp.zeros_like(h_scr)

    xc, tail = _chunk_conv(x_ref, buf, cw_ref, cb_ref, first)
    cs_ref[0] = tail
    a_parts, u_parts = _rglru_gates(xc, wr_ref, br_ref, wi_ref, bi_ref, lam_ref)
    n_rows = xc.shape[0]
    ng = n_rows // 8
    sub = lax.broadcasted_iota(jnp.int32, (ng, 8, RG_BW), 1)
    for n in range(RG_BLOCKS):
        sl = slice(n * RG_BW, (n + 1) * RG_BW)
        a = a_parts[n].reshape(ng, 8, RG_BW)
        u = u_parts[n].reshape(ng, 8, RG_BW)
        s = 1
        while s < 8:
            keep = sub >= s
            a_sh = jnp.where(keep, pltpu.roll(a, s, 1), 1.0)
            u_sh = jnp.where(keep, pltpu.roll(u, s, 1), 0.0)
            u = a * u_sh + u
            a = a * a_sh
            s *= 2
        h = h_scr[:, sl]
        groups = []
        for j in range(ng):
            hj = a[j] * h + u[j]
            groups.append(hj)
            h = hj[7:8, :]
        h_scr[:, sl] = h
        y_ref[:, sl] = (jnp.concatenate(groups, axis=0) * _silu(g_ref[:, sl])).astype(BF16)
    h_ref[0] = h_scr[...]


def _rglru_prompt(proj, cw, cb, wr, br, wi, bi, lam, bsz, seq):
    rows = _gla_rows(seq)
    nc = seq // rows
    blk = lambda j: pl.BlockSpec((rows, BRANCH), lambda b, c, j=j: (b * nc + c, j))
    const = lambda shape: pl.BlockSpec(shape, lambda b, c: (0,) * len(shape))
    return pl.pallas_call(
        _rglru_prompt_kernel,
        grid=(bsz, nc),
        in_specs=[blk(BLK_RG_X), blk(BLK_RG_G), const((CONV_W, BRANCH)), const((1, BRANCH)),
                  const((RG_BLOCKS, RG_BW, RG_BW)), const((1, BRANCH)),
                  const((RG_BLOCKS, RG_BW, RG_BW)), const((1, BRANCH)), const((1, BRANCH))],
        out_specs=[pl.BlockSpec((rows, BRANCH), lambda b, c: (b * nc + c, 0)),
                   pl.BlockSpec((1, 1, BRANCH), lambda b, c: (b, 0, 0)),
                   pl.BlockSpec((1, CONV_W - 1, BRANCH), lambda b, c: (b, 0, 0))],
        out_shape=[jax.ShapeDtypeStruct((bsz * seq, BRANCH), BF16),
                   jax.ShapeDtypeStruct((bsz, 1, BRANCH), F32),
                   jax.ShapeDtypeStruct((bsz, CONV_W - 1, BRANCH), F32)],
        scratch_shapes=[pltpu.VMEM((8, BRANCH), F32), pltpu.VMEM((1, BRANCH), F32)],
        compiler_params=_cparams("parallel", "arbitrary"),
        name="rglru_prompt",
    )(proj, proj, cw, cb, wr, br, wi, bi, lam)


def _ssd_prompt_kernel(z_ref, x_ref, bc_ref, sm_ref, cwx_ref, cbx_ref, cwb_ref, cbb_ref,
                       dtb_ref, a_ref, d_ref, nw_ref, tri_ref, exp_ref,
                       y_ref, s_ref, cs_ref, st_scr, xbuf, bcbuf):
    c = pl.program_id(1)
    first = c == 0
    n_chunks = x_ref.shape[0] // CHUNK
    gw = BRANCH // SSD_G
    hpg = SSD_HEADS // SSD_G

    @pl.when(first)
    def _():
        st_scr[...] = jnp.zeros_like(st_scr)

    xc, xtail = _chunk_conv(x_ref, xbuf, cwx_ref, cbx_ref, first)
    bcc, bctail = _chunk_conv(bc_ref, bcbuf, cwb_ref, cbb_ref, first)
    cs_ref[0, :, 0:BRANCH] = xtail
    cs_ref[0, :, BRANCH:SSD_CONV_DIM] = bctail
    xs_all = _silu(xc)
    bcs_all = _silu(bcc)
    dt_all = _softplus(sm_ref[...] + dtb_ref[...])

    tri = tri_ref[...]
    expand = exp_ref[...]
    a2 = a_ref[...] * LOG2E
    t_idx = lax.broadcasted_iota(jnp.int32, (CHUNK, CHUNK), 0)
    s_idx = lax.broadcasted_iota(jnp.int32, (CHUNK, CHUNK), 1)
    causal = s_idx <= t_idx
    lane = lax.broadcasted_iota(jnp.int32, (CHUNK, 2 * SSD_P), 1)

    chunks = []
    for k in range(n_chunks):
        rs = slice(k * CHUNK, (k + 1) * CHUNK)
        xs, bcs, dt = xs_all[rs, :], bcs_all[rs, :], dt_all[rs, :]
        cum = _sel_left(tri, dt * a2)
        cum_t = cum.T
        dt_x = _sel_right(dt, expand)
        cum_x = _sel_right(cum, expand)
        cum_last = cum_x[CHUNK - 1:CHUNK, :]
        xdt = xs * dt_x
        xw = (xdt * jnp.exp2(cum_last - cum_x)).astype(BF16)
        xdt16 = xdt.astype(BF16)
        b16 = [bcs[:, g * SSD_N:(g + 1) * SSD_N].astype(BF16) for g in range(SSD_G)]
        c16 = [bcs[:, (SSD_G + g) * SSD_N:(SSD_G + g + 1) * SSD_N].astype(BF16) for g in range(SSD_G)]
        bt16 = [bcs[:, g * SSD_N:(g + 1) * SSD_N].T.astype(BF16) for g in range(SSD_G)]
        pairs = []
        for g in range(SSD_G):
            cb = _dot_nt(c16[g], b16[g])
            for pair in range(hpg // 2):
                h0 = g * hpg + 2 * pair
                xp = xdt16[:, h0 * SSD_P:(h0 + 2) * SSD_P]
                outs = []
                for h in (h0, h0 + 1):
                    col = DT_LANE + h
                    seg = cum[:, col:col + 1] - cum_t[col:col + 1, :]
                    lmat = jnp.where(causal, jnp.exp2(jnp.where(causal, seg, 0.0)), 0.0)
                    outs.append(_dot((cb * lmat).astype(BF16), xp))
                pairs.append(jnp.where(lane < SSD_P, outs[0], outs[1]))
        chunks.append(dict(rs=rs, xs=xs, y_intra=jnp.concatenate(pairs, axis=1),
                           dec_in=jnp.exp2(cum_x), dec_out=jnp.exp2(cum_last), xw=xw, c16=c16, bt16=bt16))

    st = [st_scr[:, g * gw:(g + 1) * gw] for g in range(SSD_G)]
    for ck in chunks:
        y_inter = []
        for g in range(SSD_G):
            gs = slice(g * gw, (g + 1) * gw)
            y_inter.append(_dot(ck["c16"][g], st[g].astype(BF16)) * ck["dec_in"][:, gs])
            st[g] = st[g] * ck["dec_out"][:, gs] + _dot(ck["bt16"][g], ck["xw"][:, gs])
        ck["y"] = ck["y_intra"] + jnp.concatenate(y_inter, axis=1)
    for g in range(SSD_G):
        st_scr[:, g * gw:(g + 1) * gw] = st[g]

    for ck in chunks:
        rs = ck["rs"]
        y = (ck["y"] + ck["xs"] * d_ref[...]) * _silu(z_ref[rs, :])
        y_ref[rs, :] = _group_norm(y, nw_ref[...], gw).astype(BF16)

    @pl.when(c == pl.num_programs(1) - 1)
    def _():
        s_ref[0] = st_scr[...].T.reshape(SSD_HEADS, SSD_P, SSD_N)


def _ssd_expand_const(width):
    e = np.zeros((128, SSD_HEADS * width), np.float32)
    for h in range(SSD_HEADS):
        e[DT_LANE + h, h * width:(h + 1) * width] = 1.0
    return jnp.asarray(np.tile(e, (3, 1)), dtype=BF16)


def _ssd_prompt(proj, cwx, cbx, cwb, cbb, dtb, a_pad, d_x, nw, bsz, seq):
    rows = _gla_rows(seq)
    nc = seq // rows
    blk = lambda w, j: pl.BlockSpec((rows, w), lambda b, c, j=j: (b * nc + c, j))
    const = lambda shape: pl.BlockSpec(shape, lambda b, c: (0,) * len(shape))
    return pl.pallas_call(
        _ssd_prompt_kernel,
        grid=(bsz, nc),
        in_specs=[blk(BRANCH, BLK_SS_Z), blk(BRANCH, BLK_SS_X), blk(SSD_BC, BLK_SS_BC),
                  blk(128, BLK_SMALL),
                  const((CONV_W, BRANCH)), const((1, BRANCH)), const((CONV_W, SSD_BC)),
                  const((1, SSD_BC)), const((1, 128)), const((1, 128)), const((1, BRANCH)),
                  const((1, BRANCH)), const((CHUNK, 3 * CHUNK)), const((3 * 128, BRANCH))],
        out_specs=[pl.BlockSpec((rows, BRANCH), lambda b, c: (b * nc + c, 0)),
                   pl.BlockSpec((1, SSD_HEADS, SSD_P, SSD_N), lambda b, c: (b, 0, 0, 0)),
                   pl.BlockSpec((1, CONV_W - 1, SSD_CONV_DIM), lambda b, c: (b, 0, 0))],
        out_shape=[jax.ShapeDtypeStruct((bsz * seq, BRANCH), BF16),
                   jax.ShapeDtypeStruct((bsz, SSD_HEADS, SSD_P, SSD_N), F32),
                   jax.ShapeDtypeStruct((bsz, CONV_W - 1, SSD_CONV_DIM), F32)],
        scratch_shapes=[pltpu.VMEM((SSD_N, BRANCH), F32),
                        pltpu.VMEM((8, BRANCH), F32),
                        pltpu.VMEM((8, SSD_BC), F32)],
        compiler_params=_cparams("parallel", "arbitrary"),
        name="ssd_prompt",
    )(proj, proj, proj, proj, cwx, cbx, cwb, cbb, dtb, a_pad, d_x, nw,
      _tri_const(CHUNK), _ssd_expand_const(SSD_P))


def _step_conv(x, cs_ref, w_ref, b_ref, ncs_ref):
    y = b_ref[...] + x * w_ref[CONV_W - 1:CONV_W, :]
    for j in range(CONV_W - 1):
        y = y + cs_ref[j] * w_ref[j:j + 1, :]
    for j in range(CONV_W - 2):
        ncs_ref[j] = cs_ref[j + 1]
    ncs_ref[CONV_W - 2] = x
    return y


def _sample_pre_kernel(p_ref, lb_ref, rcs_ref, rh_ref, rcw_ref, rcb_ref, wr_ref, br_ref, wi_ref,
                       bi_ref, lam_ref, wup_ref, bup_ref, scs_ref, scw_ref, scb_ref, dtb_ref,
                       a_ref, exp_ref, expw_ref,
                       hq_ref, hk_ref, gq_ref, gd_ref, yrg_ref, nrh_ref, nrcs_ref,
                       sx_ref, sbc_ref, sdx_ref, sda_ref, nscs_ref):
    col = lambda blk, w: slice(blk * w, (blk + 1) * w)
    f = p_ref[:, col(BLK_HG_F, BRANCH)]
    lb = lb_ref[...]
    hq_ref[...] = _silu(p_ref[:, col(BLK_HG_Q, BRANCH)])
    hk_ref[...] = (1.0 - lb) * (1.0 - _sigmoid(f))
    sm = p_ref[:, col(BLK_SMALL, 128)]
    up = _dot(sm.astype(BF16), wup_ref[...]) + bup_ref[...]
    gq_ref[...] = p_ref[:, col(BLK_GL_Q, GLA_KDIM)] * (GLA_DK ** -0.5)
    gd_ref[...] = jnp.exp(-_softplus(-up) * (1.0 / GLA_TAU))
    xc = _step_conv(p_ref[:, col(BLK_RG_X, BRANCH)], rcs_ref, rcw_ref, rcb_ref, nrcs_ref)
    a_parts, u_parts = _rglru_gates(xc, wr_ref, br_ref, wi_ref, bi_ref, lam_ref)
    h = jnp.concatenate(a_parts, axis=1) * rh_ref[...] + jnp.concatenate(u_parts, axis=1)
    nrh_ref[...] = h
    yrg_ref[...] = (h * _silu(p_ref[:, col(BLK_RG_G, BRANCH)])).astype(BF16)
    xbc = jnp.concatenate([p_ref[:, col(BLK_SS_X, BRANCH)], p_ref[:, col(BLK_SS_BC, SSD_BC)]], axis=1)
    xbc = _silu(_step_conv(xbc, scs_ref, scw_ref, scb_ref, nscs_ref))
    xs = xbc[:, 0:BRANCH]
    sx_ref[...] = xs
    sbc_ref[...] = xbc[:, BRANCH:SSD_CONV_DIM]
    dt = _softplus(sm + dtb_ref[...])
    expand = exp_ref[...]
    sdx_ref[...] = xs * _sel_right(dt, expand)
    sda_ref[...] = jnp.exp(_sel_right(dt * a_ref[...], expw_ref[...]))


def _sample_pre(proj, lb, rcs, rh, rcw, rcb, wr, br, wi, bi, lam, wup, bup, scs, scw, scb,
                dtb, a_pad):
    nb = proj.shape[0]
    sd = lambda *shape, dt=F32: jax.ShapeDtypeStruct(shape, dt)
    return pl.pallas_call(
        _sample_pre_kernel,
        out_shape=[sd(nb, BRANCH), sd(nb, BRANCH),
                   sd(nb, GLA_KDIM), sd(nb, GLA_KDIM),
                   sd(nb, BRANCH, dt=BF16), sd(nb, BRANCH), sd(CONV_W - 1, nb, BRANCH),
                   sd(nb, BRANCH), sd(nb, SSD_BC), sd(nb, BRANCH), sd(nb, SSD_HEADS * SSD_N),
                   sd(CONV_W - 1, nb, SSD_CONV_DIM)],
        compiler_params=pltpu.CompilerParams(vmem_limit_bytes=VMEM_LIMIT),
        name="sample_pre",
    )(proj, lb, rcs, rh, rcw, rcb, wr, br, wi, bi, lam, wup, bup, scs, scw, scb, dtb, a_pad,
      _ssd_expand_const(SSD_P), _ssd_expand_const(SSD_N))


def _pad_t(x):
    r = x.shape[0]
    return jnp.concatenate([x, jnp.zeros((128 - r, 128), F32)], axis=0).T


STATE_UNROLL = 4


def _gla_state_kernel(*refs, heads, tied):
    so_ref, o_ref = refs[-2:]
    if tied:
        k_ref, v_ref, q_ref, s_ref = refs[:4]
    else:
        d_ref, k_ref, v_ref, q_ref, s_ref = refs[:5]
    dv = s_ref.shape[-1]

    def body(b, carry):
        kt_ = _pad_t(k_ref[b])
        dt_ = None if tied else _pad_t(d_ref[b])
        vr = v_ref[b]
        qr = q_ref[b]
        for h in range(heads):
            kb = jnp.broadcast_to(kt_[:, h:h + 1], (kt_.shape[0], dv))
            d = jnp.maximum(1.0 - kb, TINY) if tied else dt_[:, h:h + 1]
            s_new = d * s_ref[b, h] + kb * vr[h:h + 1, :]
            so_ref[b, h] = s_new
            q8 = jnp.broadcast_to(qr[h:h + 1, :], (8, qr.shape[1])).astype(BF16)
            o_ref[b, h:h + 1, :] = _dot(q8, s_new.astype(BF16))[0:1, :]
        return carry
    lax.fori_loop(0, s_ref.shape[0], body, 0, unroll=STATE_UNROLL)


def _state_call(kern, name, vec_args, vec_specs, s_all, so_prev, layer, o_shape, o_spec, bb):
    nb = s_all.shape[1]
    st = pl.BlockSpec((None, bb) + s_all.shape[2:], lambda i: (layer, i, 0, 0, 0))
    in_specs = list(vec_specs) + [st]
    args = list(vec_args) + [s_all]
    aliases = {}
    if so_prev is not None:
        in_specs.append(pl.BlockSpec(memory_space=pl.ANY))
        args.append(so_prev)
        aliases = {len(args) - 1: 0}
    return pl.pallas_call(
        kern,
        grid=(nb // bb,),
        in_specs=in_specs,
        out_specs=[st, o_spec],
        out_shape=[jax.ShapeDtypeStruct(s_all.shape, F32), o_shape],
        input_output_aliases=aliases,
        compiler_params=_cparams("parallel"),
        name=name,
    )(*args)


def _gla_state(d, k, v, q, s_all, so_prev, layer, *, bb):
    _, nb, heads, dk, dv = s_all.shape
    vec = lambda w: pl.BlockSpec((bb, heads, w), lambda i: (i, 0, 0))
    tied = d is None
    args = (k, v, q) if tied else (d, k, v, q)
    specs = (vec(dk), vec(dv), vec(dk)) if tied else (vec(dk), vec(dk), vec(dv), vec(dk))
    return _state_call(functools.partial(_gla_state_kernel, heads=heads, tied=tied),
                       "hgrn_state" if tied else "gla_state", args, specs, s_all, so_prev, layer,
                       jax.ShapeDtypeStruct((nb, heads, dv), F32), vec(dv), bb)


def _ssd_state_kernel(da_ref, dx_ref, b_ref, c_ref, s_ref, *rest):
    so_ref, y_ref = rest[-2:]
    hpg = SSD_HEADS // SSD_G

    def body(b, carry):
        ar = da_ref[b]
        xt_ = _pad_t(dx_ref[b])
        br = b_ref[b]
        cr = c_ref[b]
        for h in range(SSD_HEADS):
            g = h // hpg
            rows = slice((h % 2) * SSD_P, (h % 2 + 1) * SSD_P)
            j = h // 2
            s_new = ar[h:h + 1, :] * s_ref[b, h] + xt_[rows, j:j + 1] * br[g:g + 1, :]
            so_ref[b, h] = s_new
            c8 = jnp.broadcast_to(cr[g:g + 1, :], (8, SSD_N)).astype(BF16)
            y_ref[b, h:h + 1, :] = _dot_nt(c8, s_new.astype(BF16))[0:1, :]
        return carry
    lax.fori_loop(0, s_ref.shape[0], body, 0, unroll=STATE_UNROLL)


def _ssd_state(da, dx, bv, cv, s_all, so_prev, layer, *, bb):
    nb = s_all.shape[1]
    vec = lambda r, w: pl.BlockSpec((bb, r, w), lambda i: (i, 0, 0))
    return _state_call(_ssd_state_kernel, "ssd_state", (da, dx, bv, cv),
                       (vec(SSD_HEADS, SSD_N), vec(8, 128), vec(SSD_G, SSD_N), vec(SSD_G, SSD_N)),
                       s_all, so_prev, layer,
                       jax.ShapeDtypeStruct((nb, SSD_HEADS, SSD_P), F32), vec(SSD_HEADS, SSD_P), bb)


def _sample_post_kernel(p_ref, ohg_ref, ogl_ref, yss_ref, sx_ref, d_ref, hnw_ref, gnw_ref,
                        snw_ref, yhg_ref, ygl_ref, yso_ref):
    col = lambda blk: slice(blk * BRANCH, (blk + 1) * BRANCH)
    yhg_ref[...] = (_group_norm(ohg_ref[...], hnw_ref[...], HG_DV)
                    * _silu(p_ref[:, col(BLK_HG_G)])).astype(BF16)
    ygl_ref[...] = (_group_norm(ogl_ref[...], gnw_ref[...], GLA_DV)
                    * _silu(p_ref[:, col(BLK_GL_G)])).astype(BF16)
    y = (yss_ref[...] + sx_ref[...] * d_ref[...]) * _silu(p_ref[:, col(BLK_SS_Z)])
    yso_ref[...] = _group_norm(y, snw_ref[...], BRANCH // SSD_G).astype(BF16)


def _sample_post(proj, ohg, ogl, yss, sx, d_x, hnw, gnw, snw):
    nb = proj.shape[0]
    out = jax.ShapeDtypeStruct((nb, BRANCH), BF16)
    return pl.pallas_call(
        _sample_post_kernel, out_shape=[out, out, out],
        compiler_params=pltpu.CompilerParams(vmem_limit_bytes=VMEM_LIMIT),
        name="sample_post",
    )(proj, ohg, ogl, yss, sx, d_x, hnw, gnw, snw)


def _tail_kernel(w_ref, o_ref):
    cols = w_ref.shape[1]
    n_a = ORIG_SSD_Z - ORIG_GLA_A
    n_zx = ORIG_SSD_DT - ORIG_SSD_Z
    n_dt = N_IN - ORIG_SSD_DT
    o_ref[0:n_zx, :] = w_ref[n_a:n_a + n_zx, :].astype(BF16)
    o_ref[n_zx:n_zx + n_a, :] = w_ref[0:n_a, :].astype(BF16)
    o_ref[n_zx + n_a:n_zx + n_a + n_dt, :] = w_ref[n_a + n_zx:n_a + n_zx + n_dt, :].astype(BF16)
    o_ref[n_zx + n_a + n_dt:N_TAIL, :] = jnp.zeros((N_TAIL - n_zx - n_a - n_dt, cols), BF16)


def _prep_w_tail(w_in_t, *, cb=256):
    depth, _, d = w_in_t.shape
    return pl.pallas_call(
        _tail_kernel,
        grid=(depth, d // cb),
        in_specs=[pl.BlockSpec((None, N_TAIL, cb), lambda l, i: (l, N_MAIN // N_TAIL, i))],
        out_specs=pl.BlockSpec((None, N_TAIL, cb), lambda l, i: (l, 0, i)),
        out_shape=jax.ShapeDtypeStruct((depth, N_TAIL, d), BF16),
        compiler_params=_cparams("parallel", "parallel"),
        name="w_tail",
    )(w_in_t)


def _pad_lanes(v, start, width=128):
    out = jnp.zeros((v.shape[0], 1, width), F32)
    return out.at[:, 0, start:start + v.shape[1]].set(v.astype(F32))


def kernel(x_prompt, x_sample, state_hgrn, state_rglru, state_rglru_conv, state_gla, state_ssd, state_ssd_conv, rms_in, w_in, hgrn_lower_bounds, hgrn_norm, rglru_conv_w, rglru_conv_b, rglru_w_r, rglru_b_r, rglru_w_i, rglru_b_i, rglru_lambda, gla_w_up, gla_b_up, gla_norm, ssd_conv_w, ssd_conv_b, ssd_dt_bias, ssd_a_log, ssd_d, ssd_norm, w_out, rms_final):
    bsz, seq, _ = x_prompt.shape
    nb = x_sample.shape[0]
    row = lambda v: v.reshape(DEPTH, 1, -1).astype(F32)

    lb_all = _lower_bounds(hgrn_lower_bounds.astype(F32)).reshape(DEPTH, 1, BRANCH)
    w_in = jnp.swapaxes(w_in.astype(F32), 1, 2)
    w_tail = _prep_w_tail(w_in)
    w_out16 = w_out.astype(BF16)
    wr16 = rglru_w_r.astype(BF16)
    wi16 = rglru_w_i.astype(BF16)
    wup16 = jnp.concatenate(
        [gla_w_up, jnp.zeros((DEPTH, 128 - GLA_RANK, GLA_KDIM), gla_w_up.dtype)], axis=1).astype(BF16)
    dtb = _pad_lanes(ssd_dt_bias, DT_LANE)
    a_pad = _pad_lanes(-jnp.exp(ssd_a_log.astype(F32)), DT_LANE)
    d_x = jnp.repeat(ssd_d.astype(F32), SSD_P, axis=-1).reshape(DEPTH, 1, BRANCH)
    rms_in_r, hnw, gnw, snw = row(rms_in), row(hgrn_norm), row(gla_norm), row(ssd_norm)
    rcb, br, bi, lam, bup = (row(rglru_conv_b), row(rglru_b_r), row(rglru_b_i),
                             row(rglru_lambda), row(gla_b_up))
    scb = row(ssd_conv_b)

    xp = x_prompt.reshape(bsz * seq, D_MODEL)
    xs = x_sample.reshape(nb, D_MODEL)
    n_p = bsz * seq
    n_all = n_p + nb
    assert n_p % nb == 0
    tm_in = 1040 if n_all % 1040 == 0 else n_all
    tm_out = 512 if n_p % 512 == 0 else CHUNK
    bb = next(n for n in (16, 8, 1) if nb % n == 0)
    rf = rms_final.reshape(1, D_MODEL).astype(F32)

    h_all = _rmsnorm(xp, rms_in_r[0], BF16, tm=tm_out, shared=(n_all, 0, None))
    h_all = _rmsnorm(xs, rms_in_r[0], BF16, tm=nb, shared=(n_all, n_p, h_all))
    outs_p = [[] for _ in range(6)]
    outs_s = [[] for _ in range(3)]
    ns_hg = ns_gl = ns_ss = None
    for l in range(DEPTH):
        scw = ssd_conv_w[l].astype(F32)
        last = l + 1 == DEPTH
        next_w, next_dt = (rf, F32) if last else (rms_in_r[l + 1], BF16)
        proj = _inproj(h_all, w_in, w_tail, l, tm=tm_in, tn=1024)
        proj_s = proj[n_p:]
        y_hg, s_hg = _hgrn_prompt(proj, lb_all[l], hnw[l], bsz, seq)
        y_rg, s_rg, s_rgc = _rglru_prompt(proj, rglru_conv_w[l].astype(F32), rcb[l], wr16[l], br[l],
                                          wi16[l], bi[l], lam[l], bsz, seq)
        y_gl, s_gl = _gla_prompt(proj, wup16[l], bup[l], gnw[l], bsz, seq)
        y_ss, s_ss, s_ssc = _ssd_prompt(proj, scw[:, :BRANCH], scb[l][:, :BRANCH], scw[:, BRANCH:],
                                        scb[l][:, BRANCH:], dtb[l], a_pad[l], d_x[l], snw[l], bsz, seq)
        if last:
            xp, hp = _outproj((y_hg, y_rg, y_gl, y_ss), w_out16[l], xp, next_w, next_dt,
                              tm=tm_out // 2)
        else:
            xp, h_all = _outproj((y_hg, y_rg, y_gl, y_ss), w_out16[l], xp, next_w, next_dt,
                                 tm=tm_out, shared=(n_all, 0, None))
        for lst, s in zip(outs_p, (s_hg, s_rg.reshape(bsz, BRANCH), s_rgc, s_gl, s_ss, s_ssc)):
            lst.append(s)

        (hq, hk, gq, gd, yrg_s, nrh, nrcs, sx, sbc, sdx, sda, nscs) = _sample_pre(
            proj_s, lb_all[l], jnp.swapaxes(state_rglru_conv[l], 0, 1), state_rglru[l],
            rglru_conv_w[l].astype(F32), rcb[l], wr16[l], br[l], wi16[l], bi[l], lam[l],
            wup16[l], bup[l], jnp.swapaxes(state_ssd_conv[l], 0, 1), scw, scb[l], dtb[l], a_pad[l])
        hv = proj_s[:, BLK_HG_I * BRANCH:(BLK_HG_I + 1) * BRANCH]
        gk = proj_s[:, BLK_GL_K * GLA_KDIM:(BLK_GL_K + 1) * GLA_KDIM]
        gv = proj_s[:, BLK_GL_V * BRANCH:(BLK_GL_V + 1) * BRANCH]
        hsh = lambda a: a.reshape(nb, HG_HEADS, -1)
        gsh = lambda a: a.reshape(nb, GLA_HEADS, -1)
        ns_hg, o_hg = _gla_state(None, hsh(hk), hsh(hv), hsh(hq), state_hgrn, ns_hg, l, bb=bb)
        ns_gl, o_gl = _gla_state(gsh(gd), gsh(gk), gsh(gv), gsh(gq), state_gla, ns_gl, l, bb=bb)
        ns_ss, y_ssr = _ssd_state(sda.reshape(nb, SSD_HEADS, SSD_N), sdx.reshape(nb, 8, 128),
                                  sbc[:, :SSD_G * SSD_N].reshape(nb, SSD_G, SSD_N),
                                  sbc[:, SSD_G * SSD_N:].reshape(nb, SSD_G, SSD_N),
                                  state_ssd, ns_ss, l, bb=bb)
        yhg_s, ygl_s, yss_s = _sample_post(proj_s, o_hg.reshape(nb, BRANCH), o_gl.reshape(nb, BRANCH),
                                           y_ssr.reshape(nb, BRANCH), sx, d_x[l], hnw[l], gnw[l], snw[l])
        if last:
            xs, hs = _outproj((yhg_s, yrg_s, ygl_s, yss_s), w_out16[l], xs, next_w, next_dt, tm=nb)
        else:
            xs, h_all = _outproj((yhg_s, yrg_s, ygl_s, yss_s), w_out16[l], xs, next_w, next_dt, tm=nb,
                                 shared=(n_all, n_p, h_all))
        for lst, s in zip(outs_s, (nrh, jnp.swapaxes(nrcs, 0, 1), jnp.swapaxes(nscs, 0, 1))):
            lst.append(s)

    y_prompt = hp.reshape(bsz, seq, D_MODEL)
    y_sample = hs.reshape(nb, 1, D_MODEL)
    s_rg, s_rgc, s_ssc = (jnp.stack(l) for l in outs_s)
    return ((y_prompt, y_sample) + tuple(jnp.stack(l) for l in outs_p)
            + (ns_hg, s_rg, s_rgc, ns_gl, ns_ss, s_ssc))
```

```python
import functools
import math

import numpy as np
import jax
import jax.numpy as jnp
from jax import lax
from jax.experimental import pallas as pl
from jax.experimental.pallas import tpu as pltpu

F32 = jnp.float32
BF16 = jnp.bfloat16

D_MODEL = 2048
DEPTH = 4
BRANCH = 1024
D_MIX = 4 * BRANCH
CONV_W = 4
EPS = 1e-6
TINY = 1e-30

HG_HEADS, HG_DK, HG_DV = 8, 128, 128
RG_BLOCKS, RG_BW, RG_C = 8, 128, 8.0
GLA_HEADS, GLA_DK, GLA_DV, GLA_RANK, GLA_TAU = 4, 128, 256, 16, 16.0
GLA_KDIM = GLA_HEADS * GLA_DK
SSD_HEADS, SSD_P, SSD_G, SSD_N = 16, 64, 2, 128
SSD_BC = 2 * SSD_G * SSD_N
SSD_CONV_DIM = BRANCH + SSD_BC

ORIG_GLA_A = 9216
ORIG_SSD_Z = 9232
ORIG_SSD_DT = 11792
N_IN = 11808
N_PROJ = 12288
COL_SMALL = 11776
DT_LANE = 16
BLK_HG_Q, BLK_HG_F, BLK_HG_I, BLK_HG_G = 0, 1, 2, 3
BLK_RG_X, BLK_RG_G = 4, 5
BLK_GL_Q, BLK_GL_K = 12, 13
BLK_GL_V, BLK_GL_G = 7, 8
BLK_SS_Z, BLK_SS_X = 9, 10
BLK_SS_BC = 22
BLK_SMALL = COL_SMALL // 128

CHUNK = 128
SUB = 16
LOG2E = math.log2(math.e)
VMEM_LIMIT = 52 * 1024 * 1024


def _cparams(*sem):
    return pltpu.CompilerParams(dimension_semantics=sem, vmem_limit_bytes=VMEM_LIMIT)


def _sigmoid(x):
    return 0.5 * jnp.tanh(0.5 * x) + 0.5


def _silu(x):
    return x * _sigmoid(x)


def _softplus(x):
    return jnp.maximum(x, 0.0) + jnp.log(1.0 + jnp.exp(-jnp.abs(x)))


def _dot(a, b):
    return jnp.dot(a, b, preferred_element_type=F32)


def _dot_nt(a, b):
    return lax.dot_general(a, b, (((1,), (1,)), ((), ())), preferred_element_type=F32)


def _split3(a):
    a0 = a.astype(BF16)
    r1 = a - a0.astype(F32)
    a1 = r1.astype(BF16)
    a2 = (r1 - a1.astype(F32)).astype(BF16)
    return a0, a1, a2


def _sel_left(sel3, x):
    return _dot(sel3, jnp.concatenate(_split3(x), axis=0))


def _sel_right(x, sel3):
    return _dot(jnp.concatenate(_split3(x), axis=1), sel3)


def _group_norm(y, w, width):
    parts = []
    for g in range(y.shape[1] // width):
        yg = y[:, g * width:(g + 1) * width]
        ms = jnp.mean(yg * yg, axis=-1, keepdims=True)
        parts.append(yg * lax.rsqrt(ms + EPS))
    out = parts[0] if len(parts) == 1 else jnp.concatenate(parts, axis=1)
    return out * w


def _tri_const(c):
    return jnp.asarray(np.tile(np.tril(np.ones((c, c), np.float32)), (1, 3)), dtype=BF16)


def _level_const(c):
    t = np.arange(c)[:, None]
    s = np.arange(c)[None, :]
    lvl = np.zeros((c, c), np.int32)
    lvl[(t // SUB == s // SUB) & (s <= t)] = 1
    h, code = SUB, 2
    while h < c:
        m = (t // (2 * h) == s // (2 * h)) & (t % (2 * h) >= h) & (s % (2 * h) < h)
        lvl[m] = code
        h *= 2
        code += 1
    return jnp.asarray(lvl)


def _lb_kernel(p_ref, o_ref):
    x = p_ref[...]
    m = jnp.max(x, axis=0, keepdims=True)
    e = jnp.exp(x - m)
    p = e / jnp.sum(e, axis=0, keepdims=True)
    acc = jnp.zeros_like(p[0:1])
    rows = [acc]
    for l in range(1, DEPTH):
        acc = acc + p[l:l + 1]
        rows.append(acc)
    o_ref[...] = jnp.concatenate(rows, axis=0)


def _lower_bounds(param):
    return pl.pallas_call(
        _lb_kernel, out_shape=jax.ShapeDtypeStruct(param.shape, F32), name="hgrn_lb")(param)


N_MAIN = ORIG_GLA_A
N_TAIL = N_PROJ - N_MAIN


def _inproj_kernel(h_ref, w_ref, wt_ref, o_ref, wb_scr, *, n_main, rb):
    j = pl.program_id(0)

    @pl.when(pl.program_id(1) == 0)
    def _():
        @pl.when(j < n_main)
        def _():
            def body(i, carry):
                r = pl.multiple_of(i * rb, rb)
                wb_scr[pl.ds(r, rb), :] = w_ref[pl.ds(r, rb), :].astype(BF16)
                return carry
            lax.fori_loop(0, wb_scr.shape[0] // rb, body, 0)

        @pl.when(j >= n_main)
        def _():
            wb_scr[...] = wt_ref[...]

    o_ref[...] = _dot_nt(h_ref[...], wb_scr[...])


def _inproj(h, w_in_t, w_tail_t, layer, *, tm, tn):
    m, d = h.shape
    n_main = N_MAIN // tn
    return pl.pallas_call(
        functools.partial(_inproj_kernel, n_main=n_main, rb=128),
        grid=(N_PROJ // tn, m // tm),
        in_specs=[pl.BlockSpec((tm, d), lambda j, i: (i, 0)),
                  pl.BlockSpec((None, tn, d), lambda j, i: (layer, jnp.minimum(j, n_main - 1), 0)),
                  pl.BlockSpec((None, tn, d), lambda j, i: (layer, jnp.maximum(j - n_main, 0), 0))],
        out_specs=pl.BlockSpec((tm, tn), lambda j, i: (i, j)),
        out_shape=jax.ShapeDtypeStruct((m, N_PROJ), F32),
        scratch_shapes=[pltpu.VMEM((tn, d), BF16)],
        compiler_params=_cparams("arbitrary", "arbitrary"),
        name="inproj",
    )(h, w_in_t, w_tail_t)


def _outproj_kernel(y0_ref, y1_ref, y2_ref, y3_ref, w_ref, x_ref, nw_ref, *rest, rb):
    xo_ref, ho_ref = rest[-2:]
    acc = x_ref[...]
    for g, y_ref in enumerate((y0_ref, y1_ref, y2_ref, y3_ref)):
        acc = acc + _dot(y_ref[...], w_ref[g * BRANCH:(g + 1) * BRANCH, :])
    xo_ref[...] = acc

    def body(i, carry):
        r = pl.multiple_of(i * rb, rb)
        x = xo_ref[pl.ds(r, rb), :]
        ms = jnp.mean(x * x, axis=-1, keepdims=True)
        ho_ref[pl.ds(r, rb), :] = (x * lax.rsqrt(ms + EPS) * nw_ref[...]).astype(ho_ref.dtype)
        return carry
    lax.fori_loop(0, xo_ref.shape[0] // rb, body, 0)


def _shared_rows(m, tm, shared):
    if shared is None:
        return m, 0, (), ()
    total, row0, buf = shared
    if buf is None:
        return total, row0 // tm, (), ()
    return total, row0 // tm, (buf,), (pl.BlockSpec(memory_space=pl.ANY),)


def _outproj(ys, w, x, norm_w, norm_dtype, *, tm, shared=None):
    m, d = x.shape
    yspec = pl.BlockSpec((tm, BRANCH), lambda i: (i, 0))
    xspec = pl.BlockSpec((tm, d), lambda i: (i, 0))
    rows, off, extra, extra_specs = _shared_rows(m, tm, shared)
    return pl.pallas_call(
        functools.partial(_outproj_kernel, rb=min(tm, 64)),
        grid=(m // tm,),
        in_specs=[yspec, yspec, yspec, yspec,
                  pl.BlockSpec((D_MIX, d), lambda i: (0, 0), pipeline_mode=pl.Buffered(1)),
                  xspec, pl.BlockSpec((1, d), lambda i: (0, 0)), *extra_specs],
        out_specs=[xspec, pl.BlockSpec((tm, d), lambda i: (i + off, 0))],
        out_shape=[jax.ShapeDtypeStruct((m, d), F32), jax.ShapeDtypeStruct((rows, d), norm_dtype)],
        input_output_aliases={7: 1} if extra else {},
        compiler_params=_cparams("parallel"),
        name="outproj",
    )(*ys, w, x, norm_w, *extra)


def _rmsnorm_kernel(x_ref, w_ref, *rest):
    o_ref = rest[-1]
    x = x_ref[...]
    ms = jnp.mean(x * x, axis=-1, keepdims=True)
    o_ref[...] = (x * lax.rsqrt(ms + EPS) * w_ref[...]).astype(o_ref.dtype)


def _rmsnorm(x, w, out_dtype, *, tm, shared=None):
    m, d = x.shape
    rows, off, extra, extra_specs = _shared_rows(m, tm, shared)
    return pl.pallas_call(
        _rmsnorm_kernel,
        grid=(m // tm,),
        in_specs=[pl.BlockSpec((tm, d), lambda i: (i, 0)),
                  pl.BlockSpec((1, d), lambda i: (0, 0)), *extra_specs],
        out_specs=pl.BlockSpec((tm, d), lambda i: (i + off, 0)),
        out_shape=jax.ShapeDtypeStruct((rows, d), out_dtype),
        input_output_aliases={2: 0} if extra else {},
        compiler_params=_cparams("parallel"),
        name="rmsnorm",
    )(x, w, *extra)


def _gla_chunk_heads(qs, ks, get_v, log2fs, get_st, tri, lvl, n_heads):
    n = len(qs)
    c, width = qs[0].shape
    bs = [_sel_left(tri, lf) for lf in log2fs]

    def ref_rows(b, rows, span):
        return jnp.concatenate(
            [jnp.broadcast_to(b[r:r + 1, :], (span, width)) for r in rows], axis=0)

    def rows_only(x, lo, hi):
        parts = [jnp.zeros((lo, width), BF16), x[lo:hi, :], jnp.zeros((c - hi, width), BF16)]
        return jnp.concatenate([p for p in parts if p.shape[0]], axis=0)

    s_diag = []
    for h in range(n):
        ed = bs[h] - ref_rows(bs[h], range(SUB // 2, c, SUB), SUB)
        s_diag.append(_dot_nt((qs[h] * jnp.exp2(ed)).astype(BF16), (ks[h] * jnp.exp2(-ed)).astype(BF16)))

    s_off = []
    for h in range(n):
        lq, lk = [], []
        half = SUB
        while half < c:
            ref = ref_rows(bs[h], range(half - 1, c, 2 * half), 2 * half)
            mixed = jnp.concatenate(
                [(ks[h] if (r // half) % 2 == 0 else qs[h])[r:r + half, :] for r in range(0, c, half)],
                axis=0)
            x = (mixed * jnp.exp2(-jnp.abs(bs[h] - ref))).astype(BF16)
            for r0 in range(0, c, 2 * half):
                lk.append(rows_only(x, r0, r0 + half))
                lq.append(rows_only(x, r0 + half, r0 + 2 * half))
            half *= 2
        s_off.append(_dot_nt(jnp.concatenate(lq, axis=1), jnp.concatenate(lk, axis=1)))

    outs, sts = [], {}
    for i in range(n):
        head = i % n_heads
        scores = jnp.where(lvl == 1, s_diag[i], s_off[i])
        b = bs[i]
        b_last = b[c - 1:c, :]
        q_in = (qs[i] * jnp.exp2(b)).astype(BF16)
        k_end = (ks[i] * jnp.exp2(b_last - b)).astype(BF16)
        v = get_v(i)
        st = sts[head] if head in sts else get_st(head)
        outs.append(_dot(scores.astype(BF16), v.astype(BF16)) + _dot_nt(q_in, st.astype(BF16)))
        sts[head] = st * jnp.exp2(b_last) + _dot(v.T.astype(BF16), k_end)
    return outs, [sts[h] for h in range(n_heads)]


def _hgrn_prompt_kernel(q_ref, f_ref, i_ref, g_ref, lb_ref, nw_ref, tri_ref, lvl_ref,
                        y_ref, s_ref, st_scr):
    c = pl.program_id(1)

    @pl.when(c == 0)
    def _():
        st_scr[...] = jnp.zeros_like(st_scr)

    items = [(slice(cc * CHUNK, (cc + 1) * CHUNK), slice(h * HG_DK, (h + 1) * HG_DK))
             for cc in range(q_ref.shape[0] // CHUNK) for h in range(HG_HEADS)]
    qs, ks, log2fs = [], [], []
    for rs, sl in items:
        k = (0.5 - 0.5 * lb_ref[:, sl]) * (1.0 - jnp.tanh(0.5 * f_ref[rs, sl]))
        ks.append(k)
        log2fs.append(jnp.log2(jnp.maximum(1.0 - k, TINY)))
        qs.append(_silu(q_ref[rs, sl]))
    outs, new_sts = _gla_chunk_heads(qs, ks, lambda i: i_ref[items[i][0], items[i][1]], log2fs,
                                     lambda h: st_scr[h], tri_ref[...], lvl_ref[...], HG_HEADS)
    for h in range(HG_HEADS):
        st_scr[h] = new_sts[h]
    for (rs, sl), o in zip(items, outs):
        y = _group_norm(o, nw_ref[:, sl], HG_DV) * _silu(g_ref[rs, sl])
        y_ref[rs, sl] = y.astype(BF16)

    @pl.when(c == pl.num_programs(1) - 1)
    def _():
        for h in range(HG_HEADS):
            s_ref[0, h] = st_scr[h].T


def _gla_rows(seq):
    for n in (4, 2):
        if seq % (n * CHUNK) == 0:
            return n * CHUNK
    return CHUNK


def _hgrn_prompt(proj, lb, nw, bsz, seq):
    rows = _gla_rows(seq)
    nc = seq // rows
    blk = lambda j: pl.BlockSpec((rows, BRANCH), lambda b, c, j=j: (b * nc + c, j))
    row = pl.BlockSpec((1, BRANCH), lambda b, c: (0, 0))
    cc = pl.BlockSpec((CHUNK, CHUNK), lambda b, c: (0, 0))
    c3 = pl.BlockSpec((CHUNK, 3 * CHUNK), lambda b, c: (0, 0))
    return pl.pallas_call(
        _hgrn_prompt_kernel,
        grid=(bsz, nc),
        in_specs=[blk(BLK_HG_Q), blk(BLK_HG_F), blk(BLK_HG_I), blk(BLK_HG_G), row, row, c3, cc],
        out_specs=[pl.BlockSpec((rows, BRANCH), lambda b, c: (b * nc + c, 0)),
                   pl.BlockSpec((1, HG_HEADS, HG_DK, HG_DV), lambda b, c: (b, 0, 0, 0))],
        out_shape=[jax.ShapeDtypeStruct((bsz * seq, BRANCH), BF16),
                   jax.ShapeDtypeStruct((bsz, HG_HEADS, HG_DK, HG_DV), F32)],
        scratch_shapes=[pltpu.VMEM((HG_HEADS, HG_DV, HG_DK), F32)],
        compiler_params=_cparams("parallel", "arbitrary"),
        name="hgrn_prompt",
    )(proj, proj, proj, proj, lb, nw, _tri_const(CHUNK), _level_const(CHUNK))


def _gla_prompt_kernel(q_ref, k_ref, v_ref, g_ref, sm_ref, wup_ref, bup_ref, nw_ref,
                       tri_ref, lvl_ref, y_ref, s_ref, st_scr):
    c = pl.program_id(1)

    @pl.when(c == 0)
    def _():
        st_scr[...] = jnp.zeros_like(st_scr)

    up = _dot(sm_ref[...].astype(BF16), wup_ref[...]) + bup_ref[...]
    log2_a = -_softplus(-up) * (LOG2E / GLA_TAU)
    items = [(slice(cc * CHUNK, (cc + 1) * CHUNK), slice(h * GLA_DK, (h + 1) * GLA_DK),
              slice(h * GLA_DV, (h + 1) * GLA_DV))
             for cc in range(q_ref.shape[0] // CHUNK) for h in range(GLA_HEADS)]
    outs, new_sts = _gla_chunk_heads(
        [q_ref[rs, ks] * (GLA_DK ** -0.5) for rs, ks, _ in items], [k_ref[rs, ks] for rs, ks, _ in items],
        lambda i: v_ref[items[i][0], items[i][2]], [log2_a[rs, ks] for rs, ks, _ in items],
        lambda h: st_scr[h], tri_ref[...], lvl_ref[...], GLA_HEADS)
    for h in range(GLA_HEADS):
        st_scr[h] = new_sts[h]
    for (rs, _, vs), o in zip(items, outs):
        y = _group_norm(o, nw_ref[:, vs], GLA_DV) * _silu(g_ref[rs, vs])
        y_ref[rs, vs] = y.astype(BF16)

    @pl.when(c == pl.num_programs(1) - 1)
    def _():
        for h in range(GLA_HEADS):
            s_ref[0, h] = st_scr[h].T


def _gla_prompt(proj, wup, bup, nw, bsz, seq):
    rows = _gla_rows(seq)
    nc = seq // rows
    blk = lambda w, j: pl.BlockSpec((rows, w), lambda b, c, j=j: (b * nc + c, j))
    const = lambda shape: pl.BlockSpec(shape, lambda b, c: (0,) * len(shape))
    return pl.pallas_call(
        _gla_prompt_kernel,
        grid=(bsz, nc),
        in_specs=[blk(GLA_KDIM, BLK_GL_Q), blk(GLA_KDIM, BLK_GL_K), blk(BRANCH, BLK_GL_V),
                  blk(BRANCH, BLK_GL_G), blk(128, BLK_SMALL),
                  const((128, GLA_KDIM)), const((1, GLA_KDIM)), const((1, BRANCH)),
                  const((CHUNK, 3 * CHUNK)), const((CHUNK, CHUNK))],
        out_specs=[pl.BlockSpec((rows, BRANCH), lambda b, c: (b * nc + c, 0)),
                   pl.BlockSpec((1, GLA_HEADS, GLA_DK, GLA_DV), lambda b, c: (b, 0, 0, 0))],
        out_shape=[jax.ShapeDtypeStruct((bsz * seq, BRANCH), BF16),
                   jax.ShapeDtypeStruct((bsz, GLA_HEADS, GLA_DK, GLA_DV), F32)],
        scratch_shapes=[pltpu.VMEM((GLA_HEADS, GLA_DV, GLA_DK), F32)],
        compiler_params=_cparams("parallel", "arbitrary"),
        name="gla_prompt",
    )(proj, proj, proj, proj, proj, wup, bup, nw, _tri_const(CHUNK), _level_const(CHUNK))


def _chunk_conv(x_ref, carry, w_ref, b_ref, first):
    c = x_ref.shape[0]
    last = CONV_W - 1

    @pl.when(first)
    def _():
        carry[...] = jnp.zeros_like(carry)

    x = x_ref[...]
    y = b_ref[...] + x * w_ref[last:last + 1, :]
    for s in range(1, CONV_W):
        y = y + pltpu.roll(x, s, 0) * w_ref[last - s:last - s + 1, :]
    ext = jnp.concatenate([carry[...], x[0:8, :]], axis=0)
    head = b_ref[...] + ext[8:16, :] * w_ref[last:last + 1, :]
    for s in range(1, CONV_W):
        head = head + ext[8 - s:16 - s, :] * w_ref[last - s:last - s + 1, :]
    tail = x[c - 8:c, :]
    carry[...] = tail
    return jnp.concatenate([head, y[8:, :]], axis=0), tail[8 - last:8, :]


def _rglru_gates(xc, wr_ref, br_ref, wi_ref, bi_ref, lam_ref):
    a_parts, u_parts = [], []
    for n in range(RG_BLOCKS):
        sl = slice(n * RG_BW, (n + 1) * RG_BW)
        xb = xc[:, sl]
        xb16 = xb.astype(BF16)
        r = _sigmoid(_dot(xb16, wr_ref[n]) + br_ref[:, sl])
        i = _sigmoid(_dot(xb16, wi_ref[n]) + bi_ref[:, sl])
        log_a = -RG_C * r * _softplus(-lam_ref[:, sl])
        a = jnp.exp(log_a)
        one_m_a2 = -jnp.tanh(log_a) * (a * a + 1.0)
        a_parts.append(a)
        u_parts.append(jnp.sqrt(jnp.maximum(one_m_a2, 0.0)) * (i * xb))
    return a_parts, u_parts


def _rglru_prompt_kernel(x_ref, g_ref, cw_ref, cb_ref, wr_ref, br_ref, wi_ref, bi_ref, lam_ref,
                         y_ref, h_ref, cs_ref, buf, h_scr):
    c = pl.program_id(1)
    first = c == 0

    @pl.when(first)
    def _():
        h_scr[...] = jnp.zeros_like(h_scr)

    xc, tail = _chunk_conv(x_ref, buf, cw_ref, cb_ref, first)
    cs_ref[0] = tail
    a_parts, u_parts = _rglru_gates(xc, wr_ref, br_ref, wi_ref, bi_ref, lam_ref)
    n_rows = xc.shape[0]
    ng = n_rows // 8
    sub = lax.broadcasted_iota(jnp.int32, (ng, 8, RG_BW), 1)
    for n in range(RG_BLOCKS):
        sl = slice(n * RG_BW, (n + 1) * RG_BW)
        a = a_parts[n].reshape(ng, 8, RG_BW)
        u = u_parts[n].reshape(ng, 8, RG_BW)
        s = 1
        while s < 8:
            keep = sub >= s
            a_sh = jnp.where(keep, pltpu.roll(a, s, 1), 1.0)
            u_sh = jnp.where(keep, pltpu.roll(u, s, 1), 0.0)
            u = a * u_sh + u
            a = a * a_sh
            s *= 2
        h = h_scr[:, sl]
        groups = []
        for j in range(ng):
            hj = a[j] * h + u[j]
            groups.append(hj)
            h = hj[7:8, :]
        h_scr[:, sl] = h
        y_ref[:, sl] = (jnp.concatenate(groups, axis=0) * _silu(g_ref[:, sl])).astype(BF16)
    h_ref[0] = h_scr[...]


def _rglru_prompt(proj, cw, cb, wr, br, wi, bi, lam, bsz, seq):
    rows = _gla_rows(seq)
    nc = seq // rows
    blk = lambda j: pl.BlockSpec((rows, BRANCH), lambda b, c, j=j: (b * nc + c, j))
    const = lambda shape: pl.BlockSpec(shape, lambda b, c: (0,) * len(shape))
    return pl.pallas_call(
        _rglru_prompt_kernel,
        grid=(bsz, nc),
        in_specs=[blk(BLK_RG_X), blk(BLK_RG_G), const((CONV_W, BRANCH)), const((1, BRANCH)),
                  const((RG_BLOCKS, RG_BW, RG_BW)), const((1, BRANCH)),
                  const((RG_BLOCKS, RG_BW, RG_BW)), const((1, BRANCH)), const((1, BRANCH))],
        out_specs=[pl.BlockSpec((rows, BRANCH), lambda b, c: (b * nc + c, 0)),
                   pl.BlockSpec((1, 1, BRANCH), lambda b, c: (b, 0, 0)),
                   pl.BlockSpec((1, CONV_W - 1, BRANCH), lambda b, c: (b, 0, 0))],
        out_shape=[jax.ShapeDtypeStruct((bsz * seq, BRANCH), BF16),
                   jax.ShapeDtypeStruct((bsz, 1, BRANCH), F32),
                   jax.ShapeDtypeStruct((bsz, CONV_W - 1, BRANCH), F32)],
        scratch_shapes=[pltpu.VMEM((8, BRANCH), F32), pltpu.VMEM((1, BRANCH), F32)],
        compiler_params=_cparams("parallel", "arbitrary"),
        name="rglru_prompt",
    )(proj, proj, cw, cb, wr, br, wi, bi, lam)


def _ssd_prompt_kernel(z_ref, x_ref, bc_ref, sm_ref, cwx_ref, cbx_ref, cwb_ref, cbb_ref,
                       dtb_ref, a_ref, d_ref, nw_ref, tri_ref, exp_ref,
                       y_ref, s_ref, cs_ref, st_scr, xbuf, bcbuf):
    c = pl.program_id(1)
    first = c == 0
    n_chunks = x_ref.shape[0] // CHUNK
    gw = BRANCH // SSD_G
    hpg = SSD_HEADS // SSD_G

    @pl.when(first)
    def _():
        st_scr[...] = jnp.zeros_like(st_scr)

    xc, xtail = _chunk_conv(x_ref, xbuf, cwx_ref, cbx_ref, first)
    bcc, bctail = _chunk_conv(bc_ref, bcbuf, cwb_ref, cbb_ref, first)
    cs_ref[0, :, 0:BRANCH] = xtail
    cs_ref[0, :, BRANCH:SSD_CONV_DIM] = bctail
    xs_all = _silu(xc)
    bcs_all = _silu(bcc)
    dt_all = _softplus(sm_ref[...] + dtb_ref[...])

    tri = tri_ref[...]
    expand = exp_ref[...]
    a2 = a_ref[...] * LOG2E
    t_idx = lax.broadcasted_iota(jnp.int32, (CHUNK, CHUNK), 0)
    s_idx = lax.broadcasted_iota(jnp.int32, (CHUNK, CHUNK), 1)
    causal = s_idx <= t_idx
    lane = lax.broadcasted_iota(jnp.int32, (CHUNK, 2 * SSD_P), 1)

    chunks = []
    for k in range(n_chunks):
        rs = slice(k * CHUNK, (k + 1) * CHUNK)
        xs, bcs, dt = xs_all[rs, :], bcs_all[rs, :], dt_all[rs, :]
        cum = _sel_left(tri, dt * a2)
        cum_t = cum.T
        dt_x = _sel_right(dt, expand)
        cum_x = _sel_right(cum, expand)
        cum_last = cum_x[CHUNK - 1:CHUNK, :]
        xdt = xs * dt_x
        xw = (xdt * jnp.exp2(cum_last - cum_x)).astype(BF16)
        xdt16 = xdt.astype(BF16)
        b16 = [bcs[:, g * SSD_N:(g + 1) * SSD_N].astype(BF16) for g in range(SSD_G)]
        c16 = [bcs[:, (SSD_G + g) * SSD_N:(SSD_G + g + 1) * SSD_N].astype(BF16) for g in range(SSD_G)]
        bt16 = [bcs[:, g * SSD_N:(g + 1) * SSD_N].T.astype(BF16) for g in range(SSD_G)]
        pairs = []
        for g in range(SSD_G):
            cb = _dot_nt(c16[g], b16[g])
            for pair in range(hpg // 2):
                h0 = g * hpg + 2 * pair
                xp = xdt16[:, h0 * SSD_P:(h0 + 2) * SSD_P]
                outs = []
                for h in (h0, h0 + 1):
                    col = DT_LANE + h
                    seg = cum[:, col:col + 1] - cum_t[col:col + 1, :]
                    lmat = jnp.where(causal, jnp.exp2(jnp.where(causal, seg, 0.0)), 0.0)
                    outs.append(_dot((cb * lmat).astype(BF16), xp))
                pairs.append(jnp.where(lane < SSD_P, outs[0], outs[1]))
        chunks.append(dict(rs=rs, xs=xs, y_intra=jnp.concatenate(pairs, axis=1),
                           dec_in=jnp.exp2(cum_x), dec_out=jnp.exp2(cum_last), xw=xw, c16=c16, bt16=bt16))

    st = [st_scr[:, g * gw:(g + 1) * gw] for g in range(SSD_G)]
    for ck in chunks:
        y_inter = []
        for g in range(SSD_G):
            gs = slice(g * gw, (g + 1) * gw)
            y_inter.append(_dot(ck["c16"][g], st[g].astype(BF16)) * ck["dec_in"][:, gs])
            st[g] = st[g] * ck["dec_out"][:, gs] + _dot(ck["bt16"][g], ck["xw"][:, gs])
        ck["y"] = ck["y_intra"] + jnp.concatenate(y_inter, axis=1)
    for g in range(SSD_G):
        st_scr[:, g * gw:(g + 1) * gw] = st[g]

    for ck in chunks:
        rs = ck["rs"]
        y = (ck["y"] + ck["xs"] * d_ref[...]) * _silu(z_ref[rs, :])
        y_ref[rs, :] = _group_norm(y, nw_ref[...], gw).astype(BF16)

    @pl.when(c == pl.num_programs(1) - 1)
    def _():
        s_ref[0] = st_scr[...].T.reshape(SSD_HEADS, SSD_P, SSD_N)


def _ssd_expand_const(width):
    e = np.zeros((128, SSD_HEADS * width), np.float32)
    for h in range(SSD_HEADS):
        e[DT_LANE + h, h * width:(h + 1) * width] = 1.0
    return jnp.asarray(np.tile(e, (3, 1)), dtype=BF16)


def _ssd_prompt(proj, cwx, cbx, cwb, cbb, dtb, a_pad, d_x, nw, bsz, seq):
    rows = CHUNK
    nc = seq // rows
    blk = lambda w, j: pl.BlockSpec((rows, w), lambda b, c, j=j: (b * nc + c, j))
    const = lambda shape: pl.BlockSpec(shape, lambda b, c: (0,) * len(shape))
    return pl.pallas_call(
        _ssd_prompt_kernel,
        grid=(bsz, nc),
        in_specs=[blk(BRANCH, BLK_SS_Z), blk(BRANCH, BLK_SS_X), blk(SSD_BC, BLK_SS_BC),
                  blk(128, BLK_SMALL),
                  const((CONV_W, BRANCH)), const((1, BRANCH)), const((CONV_W, SSD_BC)),
                  const((1, SSD_BC)), const((1, 128)), const((1, 128)), const((1, BRANCH)),
                  const((1, BRANCH)), const((CHUNK, 3 * CHUNK)), const((3 * 128, BRANCH))],
        out_specs=[pl.BlockSpec((rows, BRANCH), lambda b, c: (b * nc + c, 0)),
                   pl.BlockSpec((1, SSD_HEADS, SSD_P, SSD_N), lambda b, c: (b, 0, 0, 0)),
                   pl.BlockSpec((1, CONV_W - 1, SSD_CONV_DIM), lambda b, c: (b, 0, 0))],
        out_shape=[jax.ShapeDtypeStruct((bsz * seq, BRANCH), BF16),
                   jax.ShapeDtypeStruct((bsz, SSD_HEADS, SSD_P, SSD_N), F32),
                   jax.ShapeDtypeStruct((bsz, CONV_W - 1, SSD_CONV_DIM), F32)],
        scratch_shapes=[pltpu.VMEM((SSD_N, BRANCH), F32),
                        pltpu.VMEM((8, BRANCH), F32),
                        pltpu.VMEM((8, SSD_BC), F32)],
        compiler_params=_cparams("parallel", "arbitrary"),
        name="ssd_prompt",
    )(proj, proj, proj, proj, cwx, cbx, cwb, cbb, dtb, a_pad, d_x, nw,
      _tri_const(CHUNK), _ssd_expand_const(SSD_P))


def _step_conv(x, cs_ref, w_ref, b_ref, ncs_ref):
    y = b_ref[...] + x * w_ref[CONV_W - 1:CONV_W, :]
    for j in range(CONV_W - 1):
        y = y + cs_ref[j] * w_ref[j:j + 1, :]
    for j in range(CONV_W - 2):
        ncs_ref[j] = cs_ref[j + 1]
    ncs_ref[CONV_W - 2] = x
    return y


def _sample_pre_kernel(p_ref, lb_ref, rcs_ref, rh_ref, rcw_ref, rcb_ref, wr_ref, br_ref, wi_ref,
                       bi_ref, lam_ref, wup_ref, bup_ref, scs_ref, scw_ref, scb_ref, dtb_ref,
                       a_ref, exp_ref, expw_ref,
                       hq_ref, hk_ref, gq_ref, gd_ref, yrg_ref, nrh_ref, nrcs_ref,
                       sx_ref, sbc_ref, sdx_ref, sda_ref, nscs_ref):
    col = lambda blk, w: slice(blk * w, (blk + 1) * w)
    f = p_ref[:, col(BLK_HG_F, BRANCH)]
    lb = lb_ref[...]
    hq_ref[...] = _silu(p_ref[:, col(BLK_HG_Q, BRANCH)])
    hk_ref[...] = (1.0 - lb) * (1.0 - _sigmoid(f))
    sm = p_ref[:, col(BLK_SMALL, 128)]
    up = _dot(sm.astype(BF16), wup_ref[...]) + bup_ref[...]
    gq_ref[...] = p_ref[:, col(BLK_GL_Q, GLA_KDIM)] * (GLA_DK ** -0.5)
    gd_ref[...] = jnp.exp(-_softplus(-up) * (1.0 / GLA_TAU))
    xc = _step_conv(p_ref[:, col(BLK_RG_X, BRANCH)], rcs_ref, rcw_ref, rcb_ref, nrcs_ref)
    a_parts, u_parts = _rglru_gates(xc, wr_ref, br_ref, wi_ref, bi_ref, lam_ref)
    h = jnp.concatenate(a_parts, axis=1) * rh_ref[...] + jnp.concatenate(u_parts, axis=1)
    nrh_ref[...] = h
    yrg_ref[...] = (h * _silu(p_ref[:, col(BLK_RG_G, BRANCH)])).astype(BF16)
    xbc = jnp.concatenate([p_ref[:, col(BLK_SS_X, BRANCH)], p_ref[:, col(BLK_SS_BC, SSD_BC)]], axis=1)
    xbc = _silu(_step_conv(xbc, scs_ref, scw_ref, scb_ref, nscs_ref))
    xs = xbc[:, 0:BRANCH]
    sx_ref[...] = xs
    sbc_ref[...] = xbc[:, BRANCH:SSD_CONV_DIM]
    dt = _softplus(sm + dtb_ref[...])
    expand = exp_ref[...]
    sdx_ref[...] = xs * _sel_right(dt, expand)
    sda_ref[...] = jnp.exp(_sel_right(dt * a_ref[...], expw_ref[...]))


def _sample_pre(proj, lb, rcs, rh, rcw, rcb, wr, br, wi, bi, lam, wup, bup, scs, scw, scb,
                dtb, a_pad):
    nb = proj.shape[0]
    sd = lambda *shape, dt=F32: jax.ShapeDtypeStruct(shape, dt)
    return pl.pallas_call(
        _sample_pre_kernel,
        out_shape=[sd(nb, BRANCH), sd(nb, BRANCH),
                   sd(nb, GLA_KDIM), sd(nb, GLA_KDIM),
                   sd(nb, BRANCH, dt=BF16), sd(nb, BRANCH), sd(CONV_W - 1, nb, BRANCH),
                   sd(nb, BRANCH), sd(nb, SSD_BC), sd(nb, BRANCH), sd(nb, SSD_HEADS * SSD_N),
                   sd(CONV_W - 1, nb, SSD_CONV_DIM)],
        compiler_params=pltpu.CompilerParams(vmem_limit_bytes=VMEM_LIMIT),
        name="sample_pre",
    )(proj, lb, rcs, rh, rcw, rcb, wr, br, wi, bi, lam, wup, bup, scs, scw, scb, dtb, a_pad,
      _ssd_expand_const(SSD_P), _ssd_expand_const(SSD_N))


def _pad_t(x):
    r = x.shape[0]
    return jnp.concatenate([x, jnp.zeros((128 - r, 128), F32)], axis=0).T


STATE_UNROLL = 4


def _gla_state_kernel(*refs, heads, tied):
    so_ref, o_ref = refs[-2:]
    if tied:
        k_ref, v_ref, q_ref, s_ref = refs[:4]
    else:
        d_ref, k_ref, v_ref, q_ref, s_ref = refs[:5]
    dv = s_ref.shape[-1]

    def body(b, carry):
        kt_ = _pad_t(k_ref[b])
        dt_ = None if tied else _pad_t(d_ref[b])
        vr = v_ref[b]
        qr = q_ref[b]
        for h in range(heads):
            kb = jnp.broadcast_to(kt_[:, h:h + 1], (kt_.shape[0], dv))
            d = jnp.maximum(1.0 - kb, TINY) if tied else dt_[:, h:h + 1]
            s_new = d * s_ref[b, h] + kb * vr[h:h + 1, :]
            so_ref[b, h] = s_new
            q8 = jnp.broadcast_to(qr[h:h + 1, :], (8, qr.shape[1])).astype(BF16)
            o_ref[b, h:h + 1, :] = _dot(q8, s_new.astype(BF16))[0:1, :]
        return carry
    lax.fori_loop(0, s_ref.shape[0], body, 0, unroll=STATE_UNROLL)


def _state_call(kern, name, vec_args, vec_specs, s_all, so_prev, layer, o_shape, o_spec, bb):
    nb = s_all.shape[1]
    st = pl.BlockSpec((None, bb) + s_all.shape[2:], lambda i: (layer, i, 0, 0, 0))
    in_specs = list(vec_specs) + [st]
    args = list(vec_args) + [s_all]
    aliases = {}
    if so_prev is not None:
        in_specs.append(pl.BlockSpec(memory_space=pl.ANY))
        args.append(so_prev)
        aliases = {len(args) - 1: 0}
    return pl.pallas_call(
        kern,
        grid=(nb // bb,),
        in_specs=in_specs,
        out_specs=[st, o_spec],
        out_shape=[jax.ShapeDtypeStruct(s_all.shape, F32), o_shape],
        input_output_aliases=aliases,
        compiler_params=_cparams("parallel"),
        name=name,
    )(*args)


def _gla_state(d, k, v, q, s_all, so_prev, layer, *, bb):
    _, nb, heads, dk, dv = s_all.shape
    vec = lambda w: pl.BlockSpec((bb, heads, w), lambda i: (i, 0, 0))
    tied = d is None
    args = (k, v, q) if tied else (d, k, v, q)
    specs = (vec(dk), vec(dv), vec(dk)) if tied else (vec(dk), vec(dk), vec(dv), vec(dk))
    return _state_call(functools.partial(_gla_state_kernel, heads=heads, tied=tied),
                       "hgrn_state" if tied else "gla_state", args, specs, s_all, so_prev, layer,
                       jax.ShapeDtypeStruct((nb, heads, dv), F32), vec(dv), bb)


def _ssd_state_kernel(da_ref, dx_ref, b_ref, c_ref, s_ref, *rest):
    so_ref, y_ref = rest[-2:]
    hpg = SSD_HEADS // SSD_G

    def body(b, carry):
        ar = da_ref[b]
        xt_ = _pad_t(dx_ref[b])
        br = b_ref[b]
        cr = c_ref[b]
        for h in range(SSD_HEADS):
            g = h // hpg
            rows = slice((h % 2) * SSD_P, (h % 2 + 1) * SSD_P)
            j = h // 2
            s_new = ar[h:h + 1, :] * s_ref[b, h] + xt_[rows, j:j + 1] * br[g:g + 1, :]
            so_ref[b, h] = s_new
            c8 = jnp.broadcast_to(cr[g:g + 1, :], (8, SSD_N)).astype(BF16)
            y_ref[b, h:h + 1, :] = _dot_nt(c8, s_new.astype(BF16))[0:1, :]
        return carry
    lax.fori_loop(0, s_ref.shape[0], body, 0, unroll=STATE_UNROLL)


def _ssd_state(da, dx, bv, cv, s_all, so_prev, layer, *, bb):
    nb = s_all.shape[1]
    vec = lambda r, w: pl.BlockSpec((bb, r, w), lambda i: (i, 0, 0))
    return _state_call(_ssd_state_kernel, "ssd_state", (da, dx, bv, cv),
                       (vec(SSD_HEADS, SSD_N), vec(8, 128), vec(SSD_G, SSD_N), vec(SSD_G, SSD_N)),
                       s_all, so_prev, layer,
                       jax.ShapeDtypeStruct((nb, SSD_HEADS, SSD_P), F32), vec(SSD_HEADS, SSD_P), bb)


def _sample_post_kernel(p_ref, ohg_ref, ogl_ref, yss_ref, sx_ref, d_ref, hnw_ref, gnw_ref,
                        snw_ref, yhg_ref, ygl_ref, yso_ref):
    col = lambda blk: slice(blk * BRANCH, (blk + 1) * BRANCH)
    yhg_ref[...] = (_group_norm(ohg_ref[...], hnw_ref[...], HG_DV)
                    * _silu(p_ref[:, col(BLK_HG_G)])).astype(BF16)
    ygl_ref[...] = (_group_norm(ogl_ref[...], gnw_ref[...], GLA_DV)
                    * _silu(p_ref[:, col(BLK_GL_G)])).astype(BF16)
    y = (yss_ref[...] + sx_ref[...] * d_ref[...]) * _silu(p_ref[:, col(BLK_SS_Z)])
    yso_ref[...] = _group_norm(y, snw_ref[...], BRANCH // SSD_G).astype(BF16)


def _sample_post(proj, ohg, ogl, yss, sx, d_x, hnw, gnw, snw):
    nb = proj.shape[0]
    out = jax.ShapeDtypeStruct((nb, BRANCH), BF16)
    return pl.pallas_call(
        _sample_post_kernel, out_shape=[out, out, out],
        compiler_params=pltpu.CompilerParams(vmem_limit_bytes=VMEM_LIMIT),
        name="sample_post",
    )(proj, ohg, ogl, yss, sx, d_x, hnw, gnw, snw)


def _tail_kernel(w_ref, o_ref):
    cols = w_ref.shape[1]
    n_a = ORIG_SSD_Z - ORIG_GLA_A
    n_zx = ORIG_SSD_DT - ORIG_SSD_Z
    n_dt = N_IN - ORIG_SSD_DT
    o_ref[0:n_zx, :] = w_ref[n_a:n_a + n_zx, :].astype(BF16)
    o_ref[n_zx:n_zx + n_a, :] = w_ref[0:n_a, :].astype(BF16)
    o_ref[n_zx + n_a:n_zx + n_a + n_dt, :] = w_ref[n_a + n_zx:n_a + n_zx + n_dt, :].astype(BF16)
    o_ref[n_zx + n_a + n_dt:N_TAIL, :] = jnp.zeros((N_TAIL - n_zx - n_a - n_dt, cols), BF16)


def _prep_w_tail(w_in_t, *, cb=256):
    depth, _, d = w_in_t.shape
    return pl.pallas_call(
        _tail_kernel,
        grid=(depth, d // cb),
        in_specs=[pl.BlockSpec((None, N_TAIL, cb), lambda l, i: (l, N_MAIN // N_TAIL, i))],
        out_specs=pl.BlockSpec((None, N_TAIL, cb), lambda l, i: (l, 0, i)),
        out_shape=jax.ShapeDtypeStruct((depth, N_TAIL, d), BF16),
        compiler_params=_cparams("parallel", "parallel"),
        name="w_tail",
    )(w_in_t)


def _pad_lanes(v, start, width=128):
    out = jnp.zeros((v.shape[0], 1, width), F32)
    return out.at[:, 0, start:start + v.shape[1]].set(v.astype(F32))


def kernel(x_prompt, x_sample, state_hgrn, state_rglru, state_rglru_conv, state_gla, state_ssd, state_ssd_conv, rms_in, w_in, hgrn_lower_bounds, hgrn_norm, rglru_conv_w, rglru_conv_b, rglru_w_r, rglru_b_r, rglru_w_i, rglru_b_i, rglru_lambda, gla_w_up, gla_b_up, gla_norm, ssd_conv_w, ssd_conv_b, ssd_dt_bias, ssd_a_log, ssd_d, ssd_norm, w_out, rms_final):
    bsz, seq, _ = x_prompt.shape
    nb = x_sample.shape[0]
    row = lambda v: v.reshape(DEPTH, 1, -1).astype(F32)

    lb_all = _lower_bounds(hgrn_lower_bounds.astype(F32)).reshape(DEPTH, 1, BRANCH)
    w_in = jnp.swapaxes(w_in.astype(F32), 1, 2)
    w_tail = _prep_w_tail(w_in)
    w_out16 = w_out.astype(BF16)
    wr16 = rglru_w_r.astype(BF16)
    wi16 = rglru_w_i.astype(BF16)
    wup16 = jnp.concatenate(
        [gla_w_up, jnp.zeros((DEPTH, 128 - GLA_RANK, GLA_KDIM), gla_w_up.dtype)], axis=1).astype(BF16)
    dtb = _pad_lanes(ssd_dt_bias, DT_LANE)
    a_pad = _pad_lanes(-jnp.exp(ssd_a_log.astype(F32)), DT_LANE)
    d_x = jnp.repeat(ssd_d.astype(F32), SSD_P, axis=-1).reshape(DEPTH, 1, BRANCH)
    rms_in_r, hnw, gnw, snw = row(rms_in), row(hgrn_norm), row(gla_norm), row(ssd_norm)
    rcb, br, bi, lam, bup = (row(rglru_conv_b), row(rglru_b_r), row(rglru_b_i),
                             row(rglru_lambda), row(gla_b_up))
    scb = row(ssd_conv_b)

    xp = x_prompt.reshape(bsz * seq, D_MODEL)
    xs = x_sample.reshape(nb, D_MODEL)
    n_p = bsz * seq
    n_all = n_p + nb
    assert n_p % nb == 0
    tm_in = 1040 if n_all % 1040 == 0 else n_all
    tm_out = 512 if n_p % 512 == 0 else CHUNK
    bb = next(n for n in (16, 8, 1) if nb % n == 0)
    rf = rms_final.reshape(1, D_MODEL).astype(F32)

    h_all = _rmsnorm(xp, rms_in_r[0], BF16, tm=tm_out, shared=(n_all, 0, None))
    h_all = _rmsnorm(xs, rms_in_r[0], BF16, tm=nb, shared=(n_all, n_p, h_all))
    outs_p = [[] for _ in range(6)]
    outs_s = [[] for _ in range(3)]
    ns_hg = ns_gl = ns_ss = None
    for l in range(DEPTH):
        scw = ssd_conv_w[l].astype(F32)
        last = l + 1 == DEPTH
        next_w, next_dt = (rf, F32) if last else (rms_in_r[l + 1], BF16)
        proj = _inproj(h_all, w_in, w_tail, l, tm=tm_in, tn=1024)
        proj_s = proj[n_p:]
        y_hg, s_hg = _hgrn_prompt(proj, lb_all[l], hnw[l], bsz, seq)
        y_rg, s_rg, s_rgc = _rglru_prompt(proj, rglru_conv_w[l].astype(F32), rcb[l], wr16[l], br[l],
                                          wi16[l], bi[l], lam[l], bsz, seq)
        y_gl, s_gl = _gla_prompt(proj, wup16[l], bup[l], gnw[l], bsz, seq)
        y_ss, s_ss, s_ssc = _ssd_prompt(proj, scw[:, :BRANCH], scb[l][:, :BRANCH], scw[:, BRANCH:],
                                        scb[l][:, BRANCH:], dtb[l], a_pad[l], d_x[l], snw[l], bsz, seq)
        if last:
            xp, hp = _outproj((y_hg, y_rg, y_gl, y_ss), w_out16[l], xp, next_w, next_dt,
                              tm=tm_out // 2)
        else:
            xp, h_all = _outproj((y_hg, y_rg, y_gl, y_ss), w_out16[l], xp, next_w, next_dt,
                                 tm=tm_out, shared=(n_all, 0, None))
        for lst, s in zip(outs_p, (s_hg, s_rg.reshape(bsz, BRANCH), s_rgc, s_gl, s_ss, s_ssc)):
            lst.append(s)

        (hq, hk, gq, gd, yrg_s, nrh, nrcs, sx, sbc, sdx, sda, nscs) = _sample_pre(
            proj_s, lb_all[l], jnp.swapaxes(state_rglru_conv[l], 0, 1), state_rglru[l],
            rglru_conv_w[l].astype(F32), rcb[l], wr16[l], br[l], wi16[l], bi[l], lam[l],
            wup16[l], bup[l], jnp.swapaxes(state_ssd_conv[l], 0, 1), scw, scb[l], dtb[l], a_pad[l])
        hv = proj_s[:, BLK_HG_I * BRANCH:(BLK_HG_I + 1) * BRANCH]
        gk = proj_s[:, BLK_GL_K * GLA_KDIM:(BLK_GL_K + 1) * GLA_KDIM]
        gv = proj_s[:, BLK_GL_V * BRANCH:(BLK_GL_V + 1) * BRANCH]
        hsh = lambda a: a.reshape(nb, HG_HEADS, -1)
        gsh = lambda a: a.reshape(nb, GLA_HEADS, -1)
        ns_hg, o_hg = _gla_state(None, hsh(hk), hsh(hv), hsh(hq), state_hgrn, ns_hg, l, bb=bb)
        ns_gl, o_gl = _gla_state(gsh(gd), gsh(gk), gsh(gv), gsh(gq), state_gla, ns_gl, l, bb=bb)
        ns_ss, y_ssr = _ssd_state(sda.reshape(nb, SSD_HEADS, SSD_N), sdx.reshape(nb, 8, 128),
                                  sbc[:, :SSD_G * SSD_N].reshape(nb, SSD_G, SSD_N),
                                  sbc[:, SSD_G * SSD_N:].reshape(nb, SSD_G, SSD_N),
                                  state_ssd, ns_ss, l, bb=bb)
        yhg_s, ygl_s, yss_s = _sample_post(proj_s, o_hg.reshape(nb, BRANCH), o_gl.reshape(nb, BRANCH),
                                           y_ssr.reshape(nb, BRANCH), sx, d_x[l], hnw[l], gnw[l], snw[l])
        if last:
            xs, hs = _outproj((yhg_s, yrg_s, ygl_s, yss_s), w_out16[l], xs, next_w, next_dt, tm=nb)
        else:
            xs, h_all = _outproj((yhg_s, yrg_s, ygl_s, yss_s), w_out16[l], xs, next_w, next_dt, tm=nb,
                                 shared=(n_all, n_p, h_all))
        for lst, s in zip(outs_s, (nrh, jnp.swapaxes(nrcs, 0, 1), jnp.swapaxes(nscs, 0, 1))):
            lst.append(s)

    y_prompt = hp.reshape(bsz, seq, D_MODEL)
    y_sample = hs.reshape(nb, 1, D_MODEL)
    s_rg, s_rgc, s_ssc = (jnp.stack(l) for l in outs_s)
    return ((y_prompt, y_sample) + tuple(jnp.stack(l) for l in outs_p)
            + (ns_hg, s_rg, s_rgc, ns_gl, ns_ss, s_ssc))
```

```python
import functools
import math

import numpy as np
import jax
import jax.numpy as jnp
from jax import lax
from jax.experimental import pallas as pl
from jax.experimental.pallas import tpu as pltpu

F32 = jnp.float32
BF16 = jnp.bfloat16

D_MODEL = 2048
DEPTH = 4
BRANCH = 1024
D_MIX = 4 * BRANCH
CONV_W = 4
EPS = 1e-6
TINY = 1e-30

HG_HEADS, HG_DK, HG_DV = 8, 128, 128
RG_BLOCKS, RG_BW, RG_C = 8, 128, 8.0
GLA_HEADS, GLA_DK, GLA_DV, GLA_RANK, GLA_TAU = 4, 128, 256, 16, 16.0
GLA_KDIM = GLA_HEADS * GLA_DK
SSD_HEADS, SSD_P, SSD_G, SSD_N = 16, 64, 2, 128
SSD_BC = 2 * SSD_G * SSD_N
SSD_CONV_DIM = BRANCH + SSD_BC

ORIG_GLA_A = 9216
ORIG_SSD_Z = 9232
ORIG_SSD_DT = 11792
N_IN = 11808
N_PROJ = 12288
COL_SMALL = 11776
DT_LANE = 16
BLK_HG_Q, BLK_HG_F, BLK_HG_I, BLK_HG_G = 0, 1, 2, 3
BLK_RG_X, BLK_RG_G = 4, 5
BLK_GL_Q, BLK_GL_K = 12, 13
BLK_GL_V, BLK_GL_G = 7, 8
BLK_SS_Z, BLK_SS_X = 9, 10
BLK_SS_BC = 22
BLK_SMALL = COL_SMALL // 128

CHUNK = 128
SUB = 16
LOG2E = math.log2(math.e)
VMEM_LIMIT = 52 * 1024 * 1024


def _cparams(*sem):
    return pltpu.CompilerParams(dimension_semantics=sem, vmem_limit_bytes=VMEM_LIMIT)


def _sigmoid(x):
    return 0.5 * jnp.tanh(0.5 * x) + 0.5


def _silu(x):
    return x * _sigmoid(x)


def _softplus(x):
    return jnp.maximum(x, 0.0) + jnp.log(1.0 + jnp.exp(-jnp.abs(x)))


def _dot(a, b):
    return jnp.dot(a, b, preferred_element_type=F32)


def _dot_nt(a, b):
    return lax.dot_general(a, b, (((1,), (1,)), ((), ())), preferred_element_type=F32)


def _split3(a):
    a0 = a.astype(BF16)
    r1 = a - a0.astype(F32)
    a1 = r1.astype(BF16)
    a2 = (r1 - a1.astype(F32)).astype(BF16)
    return a0, a1, a2


def _sel_left(sel3, x):
    return _dot(sel3, jnp.concatenate(_split3(x), axis=0))


def _sel_right(x, sel3):
    return _dot(jnp.concatenate(_split3(x), axis=1), sel3)


def _group_norm(y, w, width):
    parts = []
    for g in range(y.shape[1] // width):
        yg = y[:, g * width:(g + 1) * width]
        ms = jnp.mean(yg * yg, axis=-1, keepdims=True)
        parts.append(yg * lax.rsqrt(ms + EPS))
    out = parts[0] if len(parts) == 1 else jnp.concatenate(parts, axis=1)
    return out * w


def _tri_const(c):
    return jnp.asarray(np.tile(np.tril(np.ones((c, c), np.float32)), (1, 3)), dtype=BF16)


def _level_const(c):
    t = np.arange(c)[:, None]
    s = np.arange(c)[None, :]
    lvl = np.zeros((c, c), np.int32)
    lvl[(t // SUB == s // SUB) & (s <= t)] = 1
    h, code = SUB, 2
    while h < c:
        m = (t // (2 * h) == s // (2 * h)) & (t % (2 * h) >= h) & (s % (2 * h) < h)
        lvl[m] = code
        h *= 2
        code += 1
    return jnp.asarray(lvl)


def _lb_kernel(p_ref, o_ref):
    x = p_ref[...]
    m = jnp.max(x, axis=0, keepdims=True)
    e = jnp.exp(x - m)
    p = e / jnp.sum(e, axis=0, keepdims=True)
    acc = jnp.zeros_like(p[0:1])
    rows = [acc]
    for l in range(1, DEPTH):
        acc = acc + p[l:l + 1]
        rows.append(acc)
    o_ref[...] = jnp.concatenate(rows, axis=0)


def _lower_bounds(param):
    return pl.pallas_call(
        _lb_kernel, out_shape=jax.ShapeDtypeStruct(param.shape, F32), name="hgrn_lb")(param)


N_MAIN = ORIG_GLA_A
N_TAIL = N_PROJ - N_MAIN


def _inproj_kernel(h_ref, w_ref, wt_ref, o_ref, wb_scr, *, n_main, rb):
    j = pl.program_id(0)

    @pl.when(pl.program_id(1) == 0)
    def _():
        @pl.when(j < n_main)
        def _():
            def body(i, carry):
                r = pl.multiple_of(i * rb, rb)
                wb_scr[pl.ds(r, rb), :] = w_ref[pl.ds(r, rb), :].astype(BF16)
                return carry
            lax.fori_loop(0, wb_scr.shape[0] // rb, body, 0)

        @pl.when(j >= n_main)
        def _():
            wb_scr[...] = wt_ref[...]

    o_ref[...] = _dot_nt(h_ref[...], wb_scr[...])


def _inproj(h, w_in_t, w_tail_t, layer, *, tm, tn):
    m, d = h.shape
    n_main = N_MAIN // tn
    return pl.pallas_call(
        functools.partial(_inproj_kernel, n_main=n_main, rb=128),
        grid=(N_PROJ // tn, m // tm),
        in_specs=[pl.BlockSpec((tm, d), lambda j, i: (i, 0)),
                  pl.BlockSpec((None, tn, d), lambda j, i: (layer, jnp.minimum(j, n_main - 1), 0)),
                  pl.BlockSpec((None, tn, d), lambda j, i: (layer, jnp.maximum(j - n_main, 0), 0))],
        out_specs=pl.BlockSpec((tm, tn), lambda j, i: (i, j)),
        out_shape=jax.ShapeDtypeStruct((m, N_PROJ), F32),
        scratch_shapes=[pltpu.VMEM((tn, d), BF16)],
        compiler_params=_cparams("arbitrary", "arbitrary"),
        name="inproj",
    )(h, w_in_t, w_tail_t)


def _outproj_kernel(y0_ref, y1_ref, y2_ref, y3_ref, w_ref, x_ref, nw_ref, *rest, rb):
    xo_ref, ho_ref = rest[-2:]
    acc = x_ref[...]
    for g, y_ref in enumerate((y0_ref, y1_ref, y2_ref, y3_ref)):
        acc = acc + _dot(y_ref[...], w_ref[g * BRANCH:(g + 1) * BRANCH, :])
    xo_ref[...] = acc

    def body(i, carry):
        r = pl.multiple_of(i * rb, rb)
        x = xo_ref[pl.ds(r, rb), :]
        ms = jnp.mean(x * x, axis=-1, keepdims=True)
        ho_ref[pl.ds(r, rb), :] = (x * lax.rsqrt(ms + EPS) * nw_ref[...]).astype(ho_ref.dtype)
        return carry
    lax.fori_loop(0, xo_ref.shape[0] // rb, body, 0)


def _shared_rows(m, tm, shared):
    if shared is None:
        return m, 0, (), ()
    total, row0, buf = shared
    if buf is None:
        return total, row0 // tm, (), ()
    return total, row0 // tm, (buf,), (pl.BlockSpec(memory_space=pl.ANY),)


def _outproj(ys, w_all, layer, x, norm_w, norm_dtype, *, tm, shared=None):
    m, d = x.shape
    yspec = pl.BlockSpec((tm, BRANCH), lambda i: (i, 0))
    xspec = pl.BlockSpec((tm, d), lambda i: (i, 0))
    rows, off, extra, extra_specs = _shared_rows(m, tm, shared)
    return pl.pallas_call(
        functools.partial(_outproj_kernel, rb=min(tm, 64)),
        grid=(m // tm,),
        in_specs=[yspec, yspec, yspec, yspec,
                  pl.BlockSpec((None, D_MIX, d), lambda i: (layer, 0, 0), pipeline_mode=pl.Buffered(1)),
                  xspec, pl.BlockSpec((1, d), lambda i: (0, 0)), *extra_specs],
        out_specs=[xspec, pl.BlockSpec((tm, d), lambda i: (i + off, 0))],
        out_shape=[jax.ShapeDtypeStruct((m, d), F32), jax.ShapeDtypeStruct((rows, d), norm_dtype)],
        input_output_aliases={7: 1} if extra else {},
        compiler_params=_cparams("parallel"),
        name="outproj",
    )(*ys, w_all, x, norm_w, *extra)


def _rmsnorm_kernel(x_ref, w_ref, *rest):
    o_ref = rest[-1]
    x = x_ref[...]
    ms = jnp.mean(x * x, axis=-1, keepdims=True)
    o_ref[...] = (x * lax.rsqrt(ms + EPS) * w_ref[...]).astype(o_ref.dtype)


def _rmsnorm(x, w, out_dtype, *, tm, shared=None):
    m, d = x.shape
    rows, off, extra, extra_specs = _shared_rows(m, tm, shared)
    return pl.pallas_call(
        _rmsnorm_kernel,
        grid=(m // tm,),
        in_specs=[pl.BlockSpec((tm, d), lambda i: (i, 0)),
                  pl.BlockSpec((1, d), lambda i: (0, 0)), *extra_specs],
        out_specs=pl.BlockSpec((tm, d), lambda i: (i + off, 0)),
        out_shape=jax.ShapeDtypeStruct((rows, d), out_dtype),
        input_output_aliases={2: 0} if extra else {},
        compiler_params=_cparams("parallel"),
        name="rmsnorm",
    )(x, w, *extra)


def _gla_chunk_heads(qs, ks, get_v, log2fs, get_st, tri, lvl, n_heads):
    n = len(qs)
    c, width = qs[0].shape
    bs = [_sel_left(tri, lf) for lf in log2fs]

    def ref_rows(b, rows, span):
        return jnp.concatenate(
            [jnp.broadcast_to(b[r:r + 1, :], (span, width)) for r in rows], axis=0)

    def rows_only(x, lo, hi):
        parts = [jnp.zeros((lo, width), BF16), x[lo:hi, :], jnp.zeros((c - hi, width), BF16)]
        return jnp.concatenate([p for p in parts if p.shape[0]], axis=0)

    s_diag = []
    for h in range(n):
        ed = bs[h] - ref_rows(bs[h], range(SUB // 2, c, SUB), SUB)
        s_diag.append(_dot_nt((qs[h] * jnp.exp2(ed)).astype(BF16), (ks[h] * jnp.exp2(-ed)).astype(BF16)))

    s_off = []
    for h in range(n):
        lq, lk = [], []
        half = SUB
        while half < c:
            ref = ref_rows(bs[h], range(half - 1, c, 2 * half), 2 * half)
            mixed = jnp.concatenate(
                [(ks[h] if (r // half) % 2 == 0 else qs[h])[r:r + half, :] for r in range(0, c, half)],
                axis=0)
            x = (mixed * jnp.exp2(-jnp.abs(bs[h] - ref))).astype(BF16)
            for r0 in range(0, c, 2 * half):
                lk.append(rows_only(x, r0, r0 + half))
                lq.append(rows_only(x, r0 + half, r0 + 2 * half))
            half *= 2
        s_off.append(_dot_nt(jnp.concatenate(lq, axis=1), jnp.concatenate(lk, axis=1)))

    outs, sts = [], {}
    for i in range(n):
        head = i % n_heads
        scores = jnp.where(lvl == 1, s_diag[i], s_off[i])
        b = bs[i]
        b_last = b[c - 1:c, :]
        q_in = (qs[i] * jnp.exp2(b)).astype(BF16)
        k_end = (ks[i] * jnp.exp2(b_last - b)).astype(BF16)
        v = get_v(i)
        st = sts[head] if head in sts else get_st(head)
        outs.append(_dot(scores.astype(BF16), v.astype(BF16)) + _dot_nt(q_in, st.astype(BF16)))
        sts[head] = st * jnp.exp2(b_last) + _dot(v.T.astype(BF16), k_end)
    return outs, [sts[h] for h in range(n_heads)]


def _hgrn_prompt_kernel(q_ref, f_ref, i_ref, g_ref, lb_ref, nw_ref, tri_ref, lvl_ref,
                        y_ref, s_ref, st_scr):
    c = pl.program_id(1)

    @pl.when(c == 0)
    def _():
        st_scr[...] = jnp.zeros_like(st_scr)

    items = [(slice(cc * CHUNK, (cc + 1) * CHUNK), slice(h * HG_DK, (h + 1) * HG_DK))
             for cc in range(q_ref.shape[0] // CHUNK) for h in range(HG_HEADS)]
    qs, ks, log2fs = [], [], []
    for rs, sl in items:
        k = (0.5 - 0.5 * lb_ref[:, sl]) * (1.0 - jnp.tanh(0.5 * f_ref[rs, sl]))
        ks.append(k)
        log2fs.append(jnp.log2(jnp.maximum(1.0 - k, TINY)))
        qs.append(_silu(q_ref[rs, sl]))
    outs, new_sts = _gla_chunk_heads(qs, ks, lambda i: i_ref[items[i][0], items[i][1]], log2fs,
                                     lambda h: st_scr[h], tri_ref[...], lvl_ref[...], HG_HEADS)
    for h in range(HG_HEADS):
        st_scr[h] = new_sts[h]
    for (rs, sl), o in zip(items, outs):
        y = _group_norm(o, nw_ref[:, sl], HG_DV) * _silu(g_ref[rs, sl])
        y_ref[rs, sl] = y.astype(BF16)

    @pl.when(c == pl.num_programs(1) - 1)
    def _():
        for h in range(HG_HEADS):
            s_ref[0, h] = st_scr[h].T


def _gla_rows(seq):
    for n in (4, 2):
        if seq % (n * CHUNK) == 0:
            return n * CHUNK
    return CHUNK


def _hgrn_prompt(proj, lb, nw, bsz, seq):
    rows = _gla_rows(seq)
    nc = seq // rows
    blk = lambda j: pl.BlockSpec((rows, BRANCH), lambda b, c, j=j: (b * nc + c, j))
    row = pl.BlockSpec((1, BRANCH), lambda b, c: (0, 0))
    cc = pl.BlockSpec((CHUNK, CHUNK), lambda b, c: (0, 0))
    c3 = pl.BlockSpec((CHUNK, 3 * CHUNK), lambda b, c: (0, 0))
    return pl.pallas_call(
        _hgrn_prompt_kernel,
        grid=(bsz, nc),
        in_specs=[blk(BLK_HG_Q), blk(BLK_HG_F), blk(BLK_HG_I), blk(BLK_HG_G), row, row, c3, cc],
        out_specs=[pl.BlockSpec((rows, BRANCH), lambda b, c: (b * nc + c, 0)),
                   pl.BlockSpec((1, HG_HEADS, HG_DK, HG_DV), lambda b, c: (b, 0, 0, 0))],
        out_shape=[jax.ShapeDtypeStruct((bsz * seq, BRANCH), BF16),
                   jax.ShapeDtypeStruct((bsz, HG_HEADS, HG_DK, HG_DV), F32)],
        scratch_shapes=[pltpu.VMEM((HG_HEADS, HG_DV, HG_DK), F32)],
        compiler_params=_cparams("parallel", "arbitrary"),
        name="hgrn_prompt",
    )(proj, proj, proj, proj, lb, nw, _tri_const(CHUNK), _level_const(CHUNK))


def _gla_prompt_kernel(q_ref, k_ref, v_ref, g_ref, sm_ref, wup_ref, bup_ref, nw_ref,
                       tri_ref, lvl_ref, y_ref, s_ref, st_scr):
    c = pl.program_id(1)

    @pl.when(c == 0)
    def _():
        st_scr[...] = jnp.zeros_like(st_scr)

    up = _dot(sm_ref[...].astype(BF16), wup_ref[...]) + bup_ref[...]
    log2_a = -_softplus(-up) * (LOG2E / GLA_TAU)
    items = [(slice(cc * CHUNK, (cc + 1) * CHUNK), slice(h * GLA_DK, (h + 1) * GLA_DK),
              slice(h * GLA_DV, (h + 1) * GLA_DV))
             for cc in range(q_ref.shape[0] // CHUNK) for h in range(GLA_HEADS)]
    outs, new_sts = _gla_chunk_heads(
        [q_ref[rs, ks] * (GLA_DK ** -0.5) for rs, ks, _ in items], [k_ref[rs, ks] for rs, ks, _ in items],
        lambda i: v_ref[items[i][0], items[i][2]], [log2_a[rs, ks] for rs, ks, _ in items],
        lambda h: st_scr[h], tri_ref[...], lvl_ref[...], GLA_HEADS)
    for h in range(GLA_HEADS):
        st_scr[h] = new_sts[h]
    for (rs, _, vs), o in zip(items, outs):
        y = _group_norm(o, nw_ref[:, vs], GLA_DV) * _silu(g_ref[rs, vs])
        y_ref[rs, vs] = y.astype(BF16)

    @pl.when(c == pl.num_programs(1) - 1)
    def _():
        for h in range(GLA_HEADS):
            s_ref[0, h] = st_scr[h].T


def _gla_prompt(proj, wup, bup, nw, bsz, seq):
    rows = _gla_rows(seq)
    nc = seq // rows
    blk = lambda w, j: pl.BlockSpec((rows, w), lambda b, c, j=j: (b * nc + c, j))
    const = lambda shape: pl.BlockSpec(shape, lambda b, c: (0,) * len(shape))
    return pl.pallas_call(
        _gla_prompt_kernel,
        grid=(bsz, nc),
        in_specs=[blk(GLA_KDIM, BLK_GL_Q), blk(GLA_KDIM, BLK_GL_K), blk(BRANCH, BLK_GL_V),
                  blk(BRANCH, BLK_GL_G), blk(128, BLK_SMALL),
                  const((128, GLA_KDIM)), const((1, GLA_KDIM)), const((1, BRANCH)),
                  const((CHUNK, 3 * CHUNK)), const((CHUNK, CHUNK))],
        out_specs=[pl.BlockSpec((rows, BRANCH), lambda b, c: (b * nc + c, 0)),
                   pl.BlockSpec((1, GLA_HEADS, GLA_DK, GLA_DV), lambda b, c: (b, 0, 0, 0))],
        out_shape=[jax.ShapeDtypeStruct((bsz * seq, BRANCH), BF16),
                   jax.ShapeDtypeStruct((bsz, GLA_HEADS, GLA_DK, GLA_DV), F32)],
        scratch_shapes=[pltpu.VMEM((GLA_HEADS, GLA_DV, GLA_DK), F32)],
        compiler_params=_cparams("parallel", "arbitrary"),
        name="gla_prompt",
    )(proj, proj, proj, proj, proj, wup, bup, nw, _tri_const(CHUNK), _level_const(CHUNK))


def _chunk_conv(x_ref, carry, w_ref, b_ref, first):
    c = x_ref.shape[0]
    last = CONV_W - 1

    @pl.when(first)
    def _():
        carry[...] = jnp.zeros_like(carry)

    x = x_ref[...]
    y = b_ref[...] + x * w_ref[last:last + 1, :]
    for s in range(1, CONV_W):
        y = y + pltpu.roll(x, s, 0) * w_ref[last - s:last - s + 1, :]
    ext = jnp.concatenate([carry[...], x[0:8, :]], axis=0)
    head = b_ref[...] + ext[8:16, :] * w_ref[last:last + 1, :]
    for s in range(1, CONV_W):
        head = head + ext[8 - s:16 - s, :] * w_ref[last - s:last - s + 1, :]
    tail = x[c - 8:c, :]
    carry[...] = tail
    return jnp.concatenate([head, y[8:, :]], axis=0), tail[8 - last:8, :]


def _rglru_gates(xc, wr_ref, br_ref, wi_ref, bi_ref, lam_ref):
    a_parts, u_parts = [], []
    for n in range(RG_BLOCKS):
        sl = slice(n * RG_BW, (n + 1) * RG_BW)
        xb = xc[:, sl]
        xb16 = xb.astype(BF16)
        r = _sigmoid(_dot(xb16, wr_ref[n]) + br_ref[:, sl])
        i = _sigmoid(_dot(xb16, wi_ref[n]) + bi_ref[:, sl])
        log_a = -RG_C * r * _softplus(-lam_ref[:, sl])
        a = jnp.exp(log_a)
        one_m_a2 = -jnp.tanh(log_a) * (a * a + 1.0)
        a_parts.append(a)
        u_parts.append(jnp.sqrt(jnp.maximum(one_m_a2, 0.0)) * (i * xb))
    return a_parts, u_parts


def _rglru_prompt_kernel(x_ref, g_ref, cw_ref, cb_ref, wr_ref, br_ref, wi_ref, bi_ref, lam_ref,
                         y_ref, h_ref, cs_ref, buf, h_scr):
    c = pl.program_id(1)
    first = c == 0

    @pl.when(first)
    def _():
        h_scr[...] = jnp.zeros_like(h_scr)

    xc, tail = _chunk_conv(x_ref, buf, cw_ref, cb_ref, first)
    cs_ref[0] = tail
    a_parts, u_parts = _rglru_gates(xc, wr_ref, br_ref, wi_ref, bi_ref, lam_ref)
    n_rows = xc.shape[0]
    ng = n_rows // 8
    sub = lax.broadcasted_iota(jnp.int32, (ng, 8, RG_BW), 1)
    for n in range(RG_BLOCKS):
        sl = slice(n * RG_BW, (n + 1) * RG_BW)
        a = a_parts[n].reshape(ng, 8, RG_BW)
        u = u_parts[n].reshape(ng, 8, RG_BW)
        s = 1
        while s < 8:
            keep = sub >= s
            a_sh = jnp.where(keep, pltpu.roll(a, s, 1), 1.0)
            u_sh = jnp.where(keep, pltpu.roll(u, s, 1), 0.0)
            u = a * u_sh + u
            a = a * a_sh
            s *= 2
        h = h_scr[:, sl]
        groups = []
        for j in range(ng):
            hj = a[j] * h + u[j]
            groups.append(hj)
            h = hj[7:8, :]
        h_scr[:, sl] = h
        y_ref[:, sl] = (jnp.concatenate(groups, axis=0) * _silu(g_ref[:, sl])).astype(BF16)
    h_ref[0] = h_scr[...]


def _rglru_prompt(proj, cw, cb, wr, br, wi, bi, lam, bsz, seq):
    rows = _gla_rows(seq)
    nc = seq // rows
    blk = lambda j: pl.BlockSpec((rows, BRANCH), lambda b, c, j=j: (b * nc + c, j))
    const = lambda shape: pl.BlockSpec(shape, lambda b, c: (0,) * len(shape))
    return pl.pallas_call(
        _rglru_prompt_kernel,
        grid=(bsz, nc),
        in_specs=[blk(BLK_RG_X), blk(BLK_RG_G), const((CONV_W, BRANCH)), const((1, BRANCH)),
                  const((RG_BLOCKS, RG_BW, RG_BW)), const((1, BRANCH)),
                  const((RG_BLOCKS, RG_BW, RG_BW)), const((1, BRANCH)), const((1, BRANCH))],
        out_specs=[pl.BlockSpec((rows, BRANCH), lambda b, c: (b * nc + c, 0)),
                   pl.BlockSpec((1, 1, BRANCH), lambda b, c: (b, 0, 0)),
                   pl.BlockSpec((1, CONV_W - 1, BRANCH), lambda b, c: (b, 0, 0))],
        out_shape=[jax.ShapeDtypeStruct((bsz * seq, BRANCH), BF16),
                   jax.ShapeDtypeStruct((bsz, 1, BRANCH), F32),
                   jax.ShapeDtypeStruct((bsz, CONV_W - 1, BRANCH), F32)],
        scratch_shapes=[pltpu.VMEM((8, BRANCH), F32), pltpu.VMEM((1, BRANCH), F32)],
        compiler_params=_cparams("parallel", "arbitrary"),
        name="rglru_prompt",
    )(proj, proj, cw, cb, wr, br, wi, bi, lam)


def _ssd_prompt_kernel(z_ref, x_ref, bc_ref, sm_ref, cwx_ref, cbx_ref, cwb_ref, cbb_ref,
                       dtb_ref, a_ref, d_ref, nw_ref, tri_ref, exp_ref,
                       y_ref, s_ref, cs_ref, st_scr, xbuf, bcbuf):
    c = pl.program_id(1)
    first = c == 0
    n_chunks = x_ref.shape[0] // CHUNK
    gw = BRANCH // SSD_G
    hpg = SSD_HEADS // SSD_G

    @pl.when(first)
    def _():
        st_scr[...] = jnp.zeros_like(st_scr)

    xc, xtail = _chunk_conv(x_ref, xbuf, cwx_ref, cbx_ref, first)
    bcc, bctail = _chunk_conv(bc_ref, bcbuf, cwb_ref, cbb_ref, first)
    cs_ref[0, :, 0:BRANCH] = xtail
    cs_ref[0, :, BRANCH:SSD_CONV_DIM] = bctail
    xs_all = _silu(xc)
    bcs_all = _silu(bcc)
    dt_all = _softplus(sm_ref[...] + dtb_ref[...])

    tri = tri_ref[...]
    expand = exp_ref[...]
    a2 = a_ref[...] * LOG2E
    t_idx = lax.broadcasted_iota(jnp.int32, (CHUNK, CHUNK), 0)
    s_idx = lax.broadcasted_iota(jnp.int32, (CHUNK, CHUNK), 1)
    causal = s_idx <= t_idx
    lane = lax.broadcasted_iota(jnp.int32, (CHUNK, 2 * SSD_P), 1)

    chunks = []
    for k in range(n_chunks):
        rs = slice(k * CHUNK, (k + 1) * CHUNK)
        xs, bcs, dt = xs_all[rs, :], bcs_all[rs, :], dt_all[rs, :]
        cum = _sel_left(tri, dt * a2)
        cum_t = cum.T
        dt_x = _sel_right(dt, expand)
        cum_x = _sel_right(cum, expand)
        cum_last = cum_x[CHUNK - 1:CHUNK, :]
        xdt = xs * dt_x
        xw = (xdt * jnp.exp2(cum_last - cum_x)).astype(BF16)
        xdt16 = xdt.astype(BF16)
        b16 = [bcs[:, g * SSD_N:(g + 1) * SSD_N].astype(BF16) for g in range(SSD_G)]
        c16 = [bcs[:, (SSD_G + g) * SSD_N:(SSD_G + g + 1) * SSD_N].astype(BF16) for g in range(SSD_G)]
        bt16 = [bcs[:, g * SSD_N:(g + 1) * SSD_N].T.astype(BF16) for g in range(SSD_G)]
        pairs = []
        for g in range(SSD_G):
            cb = _dot_nt(c16[g], b16[g])
            for pair in range(hpg // 2):
                h0 = g * hpg + 2 * pair
                xp = xdt16[:, h0 * SSD_P:(h0 + 2) * SSD_P]
                outs = []
                for h in (h0, h0 + 1):
                    col = DT_LANE + h
                    seg = cum[:, col:col + 1] - cum_t[col:col + 1, :]
                    lmat = jnp.where(causal, jnp.exp2(jnp.where(causal, seg, 0.0)), 0.0)
                    outs.append(_dot((cb * lmat).astype(BF16), xp))
                pairs.append(jnp.where(lane < SSD_P, outs[0], outs[1]))
        chunks.append(dict(rs=rs, xs=xs, y_intra=jnp.concatenate(pairs, axis=1),
                           dec_in=jnp.exp2(cum_x), dec_out=jnp.exp2(cum_last), xw=xw, c16=c16, bt16=bt16))

    st = [st_scr[:, g * gw:(g + 1) * gw] for g in range(SSD_G)]
    for ck in chunks:
        y_inter = []
        for g in range(SSD_G):
            gs = slice(g * gw, (g + 1) * gw)
            y_inter.append(_dot(ck["c16"][g], st[g].astype(BF16)) * ck["dec_in"][:, gs])
            st[g] = st[g] * ck["dec_out"][:, gs] + _dot(ck["bt16"][g], ck["xw"][:, gs])
        ck["y"] = ck["y_intra"] + jnp.concatenate(y_inter, axis=1)
    for g in range(SSD_G):
        st_scr[:, g * gw:(g + 1) * gw] = st[g]

    for ck in chunks:
        rs = ck["rs"]
        y = (ck["y"] + ck["xs"] * d_ref[...]) * _silu(z_ref[rs, :])
        y_ref[rs, :] = _group_norm(y, nw_ref[...], gw).astype(BF16)

    @pl.when(c == pl.num_programs(1) - 1)
    def _():
        s_ref[0] = st_scr[...].T.reshape(SSD_HEADS, SSD_P, SSD_N)


def _ssd_expand_const(width):
    e = np.zeros((128, SSD_HEADS * width), np.float32)
    for h in range(SSD_HEADS):
        e[DT_LANE + h, h * width:(h + 1) * width] = 1.0
    return jnp.asarray(np.tile(e, (3, 1)), dtype=BF16)


def _ssd_prompt(proj, cwx, cbx, cwb, cbb, dtb, a_pad, d_x, nw, bsz, seq):
    rows = CHUNK
    nc = seq // rows
    blk = lambda w, j: pl.BlockSpec((rows, w), lambda b, c, j=j: (b * nc + c, j))
    const = lambda shape: pl.BlockSpec(shape, lambda b, c: (0,) * len(shape))
    return pl.pallas_call(
        _ssd_prompt_kernel,
        grid=(bsz, nc),
        in_specs=[blk(BRANCH, BLK_SS_Z), blk(BRANCH, BLK_SS_X), blk(SSD_BC, BLK_SS_BC),
                  blk(128, BLK_SMALL),
                  const((CONV_W, BRANCH)), const((1, BRANCH)), const((CONV_W, SSD_BC)),
                  const((1, SSD_BC)), const((1, 128)), const((1, 128)), const((1, BRANCH)),
                  const((1, BRANCH)), const((CHUNK, 3 * CHUNK)), const((3 * 128, BRANCH))],
        out_specs=[pl.BlockSpec((rows, BRANCH), lambda b, c: (b * nc + c, 0)),
                   pl.BlockSpec((1, SSD_HEADS, SSD_P, SSD_N), lambda b, c: (b, 0, 0, 0)),
                   pl.BlockSpec((1, CONV_W - 1, SSD_CONV_DIM), lambda b, c: (b, 0, 0))],
        out_shape=[jax.ShapeDtypeStruct((bsz * seq, BRANCH), BF16),
                   jax.ShapeDtypeStruct((bsz, SSD_HEADS, SSD_P, SSD_N), F32),
                   jax.ShapeDtypeStruct((bsz, CONV_W - 1, SSD_CONV_DIM), F32)],
        scratch_shapes=[pltpu.VMEM((SSD_N, BRANCH), F32),
                        pltpu.VMEM((8, BRANCH), F32),
                        pltpu.VMEM((8, SSD_BC), F32)],
        compiler_params=_cparams("parallel", "arbitrary"),
        name="ssd_prompt",
    )(proj, proj, proj, proj, cwx, cbx, cwb, cbb, dtb, a_pad, d_x, nw,
      _tri_const(CHUNK), _ssd_expand_const(SSD_P))


def _step_conv(x, cs_ref, w_ref, b_ref, ncs_ref):
    y = b_ref[...] + x * w_ref[CONV_W - 1:CONV_W, :]
    for j in range(CONV_W - 1):
        y = y + cs_ref[j] * w_ref[j:j + 1, :]
    for j in range(CONV_W - 2):
        ncs_ref[j] = cs_ref[j + 1]
    ncs_ref[CONV_W - 2] = x
    return y


def _sample_pre_kernel(p_ref, lb_ref, rcs_ref, rh_ref, rcw_ref, rcb_ref, wr_ref, br_ref, wi_ref,
                       bi_ref, lam_ref, wup_ref, bup_ref, scs_ref, scw_ref, scb_ref, dtb_ref,
                       a_ref, exp_ref, expw_ref,
                       hq_ref, hk_ref, gq_ref, gd_ref, yrg_ref, nrh_ref, nrcs_ref,
                       sx_ref, sbc_ref, sdx_ref, sda_ref, nscs_ref):
    col = lambda blk, w: slice(blk * w, (blk + 1) * w)
    f = p_ref[:, col(BLK_HG_F, BRANCH)]
    lb = lb_ref[...]
    hq_ref[...] = _silu(p_ref[:, col(BLK_HG_Q, BRANCH)])
    hk_ref[...] = (1.0 - lb) * (1.0 - _sigmoid(f))
    sm = p_ref[:, col(BLK_SMALL, 128)]
    up = _dot(sm.astype(BF16), wup_ref[...]) + bup_ref[...]
    gq_ref[...] = p_ref[:, col(BLK_GL_Q, GLA_KDIM)] * (GLA_DK ** -0.5)
    gd_ref[...] = jnp.exp(-_softplus(-up) * (1.0 / GLA_TAU))
    xc = _step_conv(p_ref[:, col(BLK_RG_X, BRANCH)], rcs_ref, rcw_ref, rcb_ref, nrcs_ref)
    a_parts, u_parts = _rglru_gates(xc, wr_ref, br_ref, wi_ref, bi_ref, lam_ref)
    h = jnp.concatenate(a_parts, axis=1) * rh_ref[...] + jnp.concatenate(u_parts, axis=1)
    nrh_ref[...] = h
    yrg_ref[...] = (h * _silu(p_ref[:, col(BLK_RG_G, BRANCH)])).astype(BF16)
    xbc = jnp.concatenate([p_ref[:, col(BLK_SS_X, BRANCH)], p_ref[:, col(BLK_SS_BC, SSD_BC)]], axis=1)
    xbc = _silu(_step_conv(xbc, scs_ref, scw_ref, scb_ref, nscs_ref))
    xs = xbc[:, 0:BRANCH]
    sx_ref[...] = xs
    sbc_ref[...] = xbc[:, BRANCH:SSD_CONV_DIM]
    dt = _softplus(sm + dtb_ref[...])
    expand = exp_ref[...]
    sdx_ref[...] = xs * _sel_right(dt, expand)
    sda_ref[...] = jnp.exp(_sel_right(dt * a_ref[...], expw_ref[...]))


def _whole(a):
    shape = a.shape
    return pl.BlockSpec(shape, lambda i: (0,) * len(shape))


def _sample_rows(proj, nb):
    return pl.BlockSpec((nb, proj.shape[1]), lambda i: (proj.shape[0] // nb - 1, 0))


def _sample_pre(proj, nb, lb, rcs, rh, rcw, rcb, wr, br, wi, bi, lam, wup, bup, scs, scw, scb,
                dtb, a_pad):
    sd = lambda *shape, dt=F32: jax.ShapeDtypeStruct(shape, dt)
    rest = (lb, rcs, rh, rcw, rcb, wr, br, wi, bi, lam, wup, bup, scs, scw, scb, dtb, a_pad,
            _ssd_expand_const(SSD_P), _ssd_expand_const(SSD_N))
    outs = [sd(nb, BRANCH), sd(nb, BRANCH),
            sd(nb, GLA_KDIM), sd(nb, GLA_KDIM),
            sd(nb, BRANCH, dt=BF16), sd(nb, BRANCH), sd(CONV_W - 1, nb, BRANCH),
            sd(nb, BRANCH), sd(nb, SSD_BC), sd(nb, BRANCH), sd(nb, SSD_HEADS * SSD_N),
            sd(CONV_W - 1, nb, SSD_CONV_DIM)]
    return pl.pallas_call(
        _sample_pre_kernel,
        grid=(1,),
        in_specs=[_sample_rows(proj, nb)] + [_whole(a) for a in rest],
        out_specs=[_whole(o) for o in outs],
        out_shape=outs,
        compiler_params=_cparams("arbitrary"),
        name="sample_pre",
    )(proj, *rest)


def _pad_t(x):
    r = x.shape[0]
    return jnp.concatenate([x, jnp.zeros((128 - r, 128), F32)], axis=0).T


STATE_UNROLL = 4


def _gla_state_kernel(*refs, heads, tied):
    so_ref, o_ref = refs[-2:]
    if tied:
        k_ref, v_ref, q_ref, s_ref = refs[:4]
    else:
        d_ref, k_ref, v_ref, q_ref, s_ref = refs[:5]
    dv = s_ref.shape[-1]

    def body(b, carry):
        kt_ = _pad_t(k_ref[b])
        dt_ = None if tied else _pad_t(d_ref[b])
        vr = v_ref[b]
        qr = q_ref[b]
        for h in range(heads):
            kb = jnp.broadcast_to(kt_[:, h:h + 1], (kt_.shape[0], dv))
            d = jnp.maximum(1.0 - kb, TINY) if tied else dt_[:, h:h + 1]
            s_new = d * s_ref[b, h] + kb * vr[h:h + 1, :]
            so_ref[b, h] = s_new
            q8 = jnp.broadcast_to(qr[h:h + 1, :], (8, qr.shape[1])).astype(BF16)
            o_ref[b, h:h + 1, :] = _dot(q8, s_new.astype(BF16))[0:1, :]
        return carry
    lax.fori_loop(0, s_ref.shape[0], body, 0, unroll=STATE_UNROLL)


def _state_call(kern, name, vec_args, vec_specs, s_all, so_prev, layer, o_shape, o_spec, bb):
    nb = s_all.shape[1]
    st = pl.BlockSpec((None, bb) + s_all.shape[2:], lambda i: (layer, i, 0, 0, 0))
    in_specs = list(vec_specs) + [st]
    args = list(vec_args) + [s_all]
    aliases = {}
    if so_prev is not None:
        in_specs.append(pl.BlockSpec(memory_space=pl.ANY))
        args.append(so_prev)
        aliases = {len(args) - 1: 0}
    return pl.pallas_call(
        kern,
        grid=(nb // bb,),
        in_specs=in_specs,
        out_specs=[st, o_spec],
        out_shape=[jax.ShapeDtypeStruct(s_all.shape, F32), o_shape],
        input_output_aliases=aliases,
        compiler_params=_cparams("parallel"),
        name=name,
    )(*args)


def _gla_state(d, k, v, q, s_all, so_prev, layer, *, bb):
    _, nb, heads, dk, dv = s_all.shape
    vec = lambda w: pl.BlockSpec((bb, heads, w), lambda i: (i, 0, 0))
    tied = d is None
    args = (k, v, q) if tied else (d, k, v, q)
    specs = (vec(dk), vec(dv), vec(dk)) if tied else (vec(dk), vec(dk), vec(dv), vec(dk))
    return _state_call(functools.partial(_gla_state_kernel, heads=heads, tied=tied),
                       "hgrn_state" if tied else "gla_state", args, specs, s_all, so_prev, layer,
                       jax.ShapeDtypeStruct((nb, heads, dv), F32), vec(dv), bb)


def _ssd_state_kernel(da_ref, dx_ref, b_ref, c_ref, s_ref, *rest):
    so_ref, y_ref = rest[-2:]
    hpg = SSD_HEADS // SSD_G

    def body(b, carry):
        ar = da_ref[b]
        xt_ = _pad_t(dx_ref[b])
        br = b_ref[b]
        cr = c_ref[b]
        for h in range(SSD_HEADS):
            g = h // hpg
            rows = slice((h % 2) * SSD_P, (h % 2 + 1) * SSD_P)
            j = h // 2
            s_new = ar[h:h + 1, :] * s_ref[b, h] + xt_[rows, j:j + 1] * br[g:g + 1, :]
            so_ref[b, h] = s_new
            c8 = jnp.broadcast_to(cr[g:g + 1, :], (8, SSD_N)).astype(BF16)
            y_ref[b, h:h + 1, :] = _dot_nt(c8, s_new.astype(BF16))[0:1, :]
        return carry
    lax.fori_loop(0, s_ref.shape[0], body, 0, unroll=STATE_UNROLL)


def _ssd_state(da, dx, bv, cv, s_all, so_prev, layer, *, bb):
    nb = s_all.shape[1]
    vec = lambda r, w: pl.BlockSpec((bb, r, w), lambda i: (i, 0, 0))
    return _state_call(_ssd_state_kernel, "ssd_state", (da, dx, bv, cv),
                       (vec(SSD_HEADS, SSD_N), vec(8, 128), vec(SSD_G, SSD_N), vec(SSD_G, SSD_N)),
                       s_all, so_prev, layer,
                       jax.ShapeDtypeStruct((nb, SSD_HEADS, SSD_P), F32), vec(SSD_HEADS, SSD_P), bb)


def _sample_post_kernel(p_ref, ohg_ref, ogl_ref, yss_ref, sx_ref, d_ref, hnw_ref, gnw_ref,
                        snw_ref, yhg_ref, ygl_ref, yso_ref):
    col = lambda blk: slice(blk * BRANCH, (blk + 1) * BRANCH)
    yhg_ref[...] = (_group_norm(ohg_ref[...], hnw_ref[...], HG_DV)
                    * _silu(p_ref[:, col(BLK_HG_G)])).astype(BF16)
    ygl_ref[...] = (_group_norm(ogl_ref[...], gnw_ref[...], GLA_DV)
                    * _silu(p_ref[:, col(BLK_GL_G)])).astype(BF16)
    y = (yss_ref[...] + sx_ref[...] * d_ref[...]) * _silu(p_ref[:, col(BLK_SS_Z)])
    yso_ref[...] = _group_norm(y, snw_ref[...], BRANCH // SSD_G).astype(BF16)


def _sample_post(proj, nb, ohg, ogl, yss, sx, d_x, hnw, gnw, snw):
    out = jax.ShapeDtypeStruct((nb, BRANCH), BF16)
    rest = (ohg, ogl, yss, sx, d_x, hnw, gnw, snw)
    return pl.pallas_call(
        _sample_post_kernel,
        grid=(1,),
        in_specs=[_sample_rows(proj, nb)] + [_whole(a) for a in rest],
        out_specs=[_whole(out)] * 3,
        out_shape=[out, out, out],
        compiler_params=_cparams("arbitrary"),
        name="sample_post",
    )(proj, *rest)


def _tail_kernel(w_ref, o_ref):
    cols = w_ref.shape[1]
    n_a = ORIG_SSD_Z - ORIG_GLA_A
    n_zx = ORIG_SSD_DT - ORIG_SSD_Z
    n_dt = N_IN - ORIG_SSD_DT
    o_ref[0:n_zx, :] = w_ref[n_a:n_a + n_zx, :].astype(BF16)
    o_ref[n_zx:n_zx + n_a, :] = w_ref[0:n_a, :].astype(BF16)
    o_ref[n_zx + n_a:n_zx + n_a + n_dt, :] = w_ref[n_a + n_zx:n_a + n_zx + n_dt, :].astype(BF16)
    o_ref[n_zx + n_a + n_dt:N_TAIL, :] = jnp.zeros((N_TAIL - n_zx - n_a - n_dt, cols), BF16)


def _prep_w_tail(w_in_t, *, cb=256):
    depth, _, d = w_in_t.shape
    return pl.pallas_call(
        _tail_kernel,
        grid=(depth, d // cb),
        in_specs=[pl.BlockSpec((None, N_TAIL, cb), lambda l, i: (l, N_MAIN // N_TAIL, i))],
        out_specs=pl.BlockSpec((None, N_TAIL, cb), lambda l, i: (l, 0, i)),
        out_shape=jax.ShapeDtypeStruct((depth, N_TAIL, d), BF16),
        compiler_params=_cparams("parallel", "parallel"),
        name="w_tail",
    )(w_in_t)


def _pad_lanes(v, start, width=128):
    out = jnp.zeros((v.shape[0], 1, width), F32)
    return out.at[:, 0, start:start + v.shape[1]].set(v.astype(F32))


def kernel(x_prompt, x_sample, state_hgrn, state_rglru, state_rglru_conv, state_gla, state_ssd, state_ssd_conv, rms_in, w_in, hgrn_lower_bounds, hgrn_norm, rglru_conv_w, rglru_conv_b, rglru_w_r, rglru_b_r, rglru_w_i, rglru_b_i, rglru_lambda, gla_w_up, gla_b_up, gla_norm, ssd_conv_w, ssd_conv_b, ssd_dt_bias, ssd_a_log, ssd_d, ssd_norm, w_out, rms_final):
    bsz, seq, _ = x_prompt.shape
    nb = x_sample.shape[0]
    row = lambda v: v.reshape(DEPTH, 1, -1).astype(F32)

    lb_all = _lower_bounds(hgrn_lower_bounds.astype(F32)).reshape(DEPTH, 1, BRANCH)
    w_in = jnp.swapaxes(w_in.astype(F32), 1, 2)
    w_tail = _prep_w_tail(w_in)
    w_out16 = w_out.astype(BF16)
    wr16 = rglru_w_r.astype(BF16)
    wi16 = rglru_w_i.astype(BF16)
    wup16 = jnp.concatenate(
        [gla_w_up, jnp.zeros((DEPTH, 128 - GLA_RANK, GLA_KDIM), gla_w_up.dtype)], axis=1).astype(BF16)
    dtb = _pad_lanes(ssd_dt_bias, DT_LANE)
    a_pad = _pad_lanes(-jnp.exp(ssd_a_log.astype(F32)), DT_LANE)
    d_x = jnp.repeat(ssd_d.astype(F32), SSD_P, axis=-1).reshape(DEPTH, 1, BRANCH)
    rms_in_r, hnw, gnw, snw = row(rms_in), row(hgrn_norm), row(gla_norm), row(ssd_norm)
    rcb, br, bi, lam, bup = (row(rglru_conv_b), row(rglru_b_r), row(rglru_b_i),
                             row(rglru_lambda), row(gla_b_up))
    scb = row(ssd_conv_b)

    xp = x_prompt.reshape(bsz * seq, D_MODEL)
    xs = x_sample.reshape(nb, D_MODEL)
    n_p = bsz * seq
    n_all = n_p + nb
    assert n_p % nb == 0
    tm_in = 1040 if n_all % 1040 == 0 else n_all
    tm_out = 512 if n_p % 512 == 0 else CHUNK
    bb = next(n for n in (16, 8, 1) if nb % n == 0)
    rf = rms_final.reshape(1, D_MODEL).astype(F32)

    h_all = _rmsnorm(xp, rms_in_r[0], BF16, tm=tm_out, shared=(n_all, 0, None))
    h_all = _rmsnorm(xs, rms_in_r[0], BF16, tm=nb, shared=(n_all, n_p, h_all))
    outs_p = [[] for _ in range(6)]
    outs_s = [[] for _ in range(3)]
    ns_hg = ns_gl = ns_ss = None
    for l in range(DEPTH):
        scw = ssd_conv_w[l].astype(F32)
        last = l + 1 == DEPTH
        next_w, next_dt = (rf, F32) if last else (rms_in_r[l + 1], BF16)
        proj = _inproj(h_all, w_in, w_tail, l, tm=tm_in, tn=1024)
        y_hg, s_hg = _hgrn_prompt(proj, lb_all[l], hnw[l], bsz, seq)
        y_rg, s_rg, s_rgc = _rglru_prompt(proj, rglru_conv_w[l].astype(F32), rcb[l], wr16[l], br[l],
                                          wi16[l], bi[l], lam[l], bsz, seq)
        y_gl, s_gl = _gla_prompt(proj, wup16[l], bup[l], gnw[l], bsz, seq)
        y_ss, s_ss, s_ssc = _ssd_prompt(proj, scw[:, :BRANCH], scb[l][:, :BRANCH], scw[:, BRANCH:],
                                        scb[l][:, BRANCH:], dtb[l], a_pad[l], d_x[l], snw[l], bsz, seq)
        if last:
            xp, hp = _outproj((y_hg, y_rg, y_gl, y_ss), w_out16, l, xp, next_w, next_dt,
                              tm=tm_out // 2)
        else:
            xp, h_all = _outproj((y_hg, y_rg, y_gl, y_ss), w_out16, l, xp, next_w, next_dt,
                                 tm=tm_out, shared=(n_all, 0, None))
        for lst, s in zip(outs_p, (s_hg, s_rg.reshape(bsz, BRANCH), s_rgc, s_gl, s_ss, s_ssc)):
            lst.append(s)

        (hq, hk, gq, gd, yrg_s, nrh, nrcs, sx, sbc, sdx, sda, nscs) = _sample_pre(
            proj, nb, lb_all[l], jnp.swapaxes(state_rglru_conv[l], 0, 1), state_rglru[l],
            rglru_conv_w[l].astype(F32), rcb[l], wr16[l], br[l], wi16[l], bi[l], lam[l],
            wup16[l], bup[l], jnp.swapaxes(state_ssd_conv[l], 0, 1), scw, scb[l], dtb[l], a_pad[l])
        hv = proj[n_p:, BLK_HG_I * BRANCH:(BLK_HG_I + 1) * BRANCH]
        gk = proj[n_p:, BLK_GL_K * GLA_KDIM:(BLK_GL_K + 1) * GLA_KDIM]
        gv = proj[n_p:, BLK_GL_V * BRANCH:(BLK_GL_V + 1) * BRANCH]
        hsh = lambda a: a.reshape(nb, HG_HEADS, -1)
        gsh = lambda a: a.reshape(nb, GLA_HEADS, -1)
        ns_hg, o_hg = _gla_state(None, hsh(hk), hsh(hv), hsh(hq), state_hgrn, ns_hg, l, bb=bb)
        ns_gl, o_gl = _gla_state(gsh(gd), gsh(gk), gsh(gv), gsh(gq), state_gla, ns_gl, l, bb=bb)
        ns_ss, y_ssr = _ssd_state(sda.reshape(nb, SSD_HEADS, SSD_N), sdx.reshape(nb, 8, 128),
                                  sbc[:, :SSD_G * SSD_N].reshape(nb, SSD_G, SSD_N),
                                  sbc[:, SSD_G * SSD_N:].reshape(nb, SSD_G, SSD_N),
                                  state_ssd, ns_ss, l, bb=bb)
        yhg_s, ygl_s, yss_s = _sample_post(proj, nb, o_hg.reshape(nb, BRANCH), o_gl.reshape(nb, BRANCH),
                                           y_ssr.reshape(nb, BRANCH), sx, d_x[l], hnw[l], gnw[l], snw[l])
        if last:
            xs, hs = _outproj((yhg_s, yrg_s, ygl_s, yss_s), w_out16, l, xs, next_w, next_dt, tm=nb)
        else:
            xs, h_all = _outproj((yhg_s, yrg_s, ygl_s, yss_s), w_out16, l, xs, next_w, next_dt, tm=nb,
                                 shared=(n_all, n_p, h_all))
        for lst, s in zip(outs_s, (nrh, jnp.swapaxes(nrcs, 0, 1), jnp.swapaxes(nscs, 0, 1))):
            lst.append(s)

    y_prompt = hp.reshape(bsz, seq, D_MODEL)
    y_sample = hs.reshape(nb, 1, D_MODEL)
    s_rg, s_rgc, s_ssc = (jnp.stack(l) for l in outs_s)
    return ((y_prompt, y_sample) + tuple(jnp.stack(l) for l in outs_p)
            + (ns_hg, s_rg, s_rgc, ns_gl, ns_ss, s_ssc))
```

```python
import functools
import math

import numpy as np
import jax
import jax.numpy as jnp
from jax import lax
from jax.experimental import pallas as pl
from jax.experimental.pallas import tpu as pltpu

F32 = jnp.float32
BF16 = jnp.bfloat16

D_MODEL = 2048
DEPTH = 4
BRANCH = 1024
D_MIX = 4 * BRANCH
CONV_W = 4
EPS = 1e-6
TINY = 1e-30

HG_HEADS, HG_DK, HG_DV = 8, 128, 128
RG_BLOCKS, RG_BW, RG_C = 8, 128, 8.0
GLA_HEADS, GLA_DK, GLA_DV, GLA_RANK, GLA_TAU = 4, 128, 256, 16, 16.0
GLA_KDIM = GLA_HEADS * GLA_DK
SSD_HEADS, SSD_P, SSD_G, SSD_N = 16, 64, 2, 128
SSD_BC = 2 * SSD_G * SSD_N
SSD_CONV_DIM = BRANCH + SSD_BC

ORIG_GLA_A = 9216
ORIG_SSD_Z = 9232
ORIG_SSD_DT = 11792
N_IN = 11808
N_PROJ = 12288
COL_SMALL = 11776
DT_LANE = 16
BLK_HG_Q, BLK_HG_F, BLK_HG_I, BLK_HG_G = 0, 1, 2, 3
BLK_RG_X, BLK_RG_G = 4, 5
BLK_GL_Q, BLK_GL_K = 12, 13
BLK_GL_V, BLK_GL_G = 7, 8
BLK_SS_Z, BLK_SS_X = 9, 10
BLK_SS_BC = 22
BLK_SMALL = COL_SMALL // 128

CHUNK = 128
SUB = 16
LOG2E = math.log2(math.e)
VMEM_LIMIT = 52 * 1024 * 1024


def _cparams(*sem):
    return pltpu.CompilerParams(dimension_semantics=sem, vmem_limit_bytes=VMEM_LIMIT)


def _sigmoid(x):
    return 0.5 * jnp.tanh(0.5 * x) + 0.5


def _silu(x):
    return x * _sigmoid(x)


def _softplus(x):
    return jnp.maximum(x, 0.0) + jnp.log(1.0 + jnp.exp(-jnp.abs(x)))


def _dot(a, b):
    return jnp.dot(a, b, preferred_element_type=F32)


def _dot_nt(a, b):
    return lax.dot_general(a, b, (((1,), (1,)), ((), ())), preferred_element_type=F32)


def _split3(a):
    a0 = a.astype(BF16)
    r1 = a - a0.astype(F32)
    a1 = r1.astype(BF16)
    a2 = (r1 - a1.astype(F32)).astype(BF16)
    return a0, a1, a2


def _sel_left(sel3, x):
    return _dot(sel3, jnp.concatenate(_split3(x), axis=0))


def _sel_right(x, sel3):
    return _dot(jnp.concatenate(_split3(x), axis=1), sel3)


def _group_norm(y, w, width):
    parts = []
    for g in range(y.shape[1] // width):
        yg = y[:, g * width:(g + 1) * width]
        ms = jnp.mean(yg * yg, axis=-1, keepdims=True)
        parts.append(yg * lax.rsqrt(ms + EPS))
    out = parts[0] if len(parts) == 1 else jnp.concatenate(parts, axis=1)
    return out * w


def _tri_const(c):
    return jnp.asarray(np.tile(np.tril(np.ones((c, c), np.float32)), (1, 3)), dtype=BF16)


def _level_const(c):
    t = np.arange(c)[:, None]
    s = np.arange(c)[None, :]
    lvl = np.zeros((c, c), np.int32)
    lvl[(t // SUB == s // SUB) & (s <= t)] = 1
    h, code = SUB, 2
    while h < c:
        m = (t // (2 * h) == s // (2 * h)) & (t % (2 * h) >= h) & (s % (2 * h) < h)
        lvl[m] = code
        h *= 2
        code += 1
    return jnp.asarray(lvl)


def _lb_kernel(p_ref, o_ref):
    x = p_ref[...]
    m = jnp.max(x, axis=0, keepdims=True)
    e = jnp.exp(x - m)
    p = e / jnp.sum(e, axis=0, keepdims=True)
    acc = jnp.zeros_like(p[0:1])
    rows = [acc]
    for l in range(1, DEPTH):
        acc = acc + p[l:l + 1]
        rows.append(acc)
    o_ref[...] = jnp.concatenate(rows, axis=0)


def _lower_bounds(param):
    return pl.pallas_call(
        _lb_kernel, out_shape=jax.ShapeDtypeStruct(param.shape, F32), name="hgrn_lb")(param)


N_MAIN = ORIG_GLA_A
N_TAIL = N_PROJ - N_MAIN


def _inproj_kernel(h_ref, w_ref, wt_ref, o_ref, wb_scr, *, n_main, rb):
    j = pl.program_id(0)

    @pl.when(pl.program_id(1) == 0)
    def _():
        @pl.when(j < n_main)
        def _():
            def body(i, carry):
                r = pl.multiple_of(i * rb, rb)
                wb_scr[pl.ds(r, rb), :] = w_ref[pl.ds(r, rb), :].astype(BF16)
                return carry
            lax.fori_loop(0, wb_scr.shape[0] // rb, body, 0)

        @pl.when(j >= n_main)
        def _():
            wb_scr[...] = wt_ref[...]

    o_ref[...] = _dot_nt(h_ref[...], wb_scr[...])


def _inproj(h, w_in_t, w_tail_t, layer, *, tm, tn):
    m, d = h.shape
    n_main = N_MAIN // tn
    return pl.pallas_call(
        functools.partial(_inproj_kernel, n_main=n_main, rb=128),
        grid=(N_PROJ // tn, m // tm),
        in_specs=[pl.BlockSpec((tm, d), lambda j, i: (i, 0)),
                  pl.BlockSpec((None, tn, d), lambda j, i: (layer, jnp.minimum(j, n_main - 1), 0)),
                  pl.BlockSpec((None, tn, d), lambda j, i: (layer, jnp.maximum(j - n_main, 0), 0))],
        out_specs=pl.BlockSpec((tm, tn), lambda j, i: (i, j)),
        out_shape=jax.ShapeDtypeStruct((m, N_PROJ), F32),
        scratch_shapes=[pltpu.VMEM((tn, d), BF16)],
        compiler_params=_cparams("arbitrary", "arbitrary"),
        name="inproj",
    )(h, w_in_t, w_tail_t)


def _outproj_kernel(y0_ref, y1_ref, y2_ref, y3_ref, w_ref, x_ref, nw_ref, *rest, rb):
    xo_ref, ho_ref = rest[-2:]
    y_all = jnp.concatenate([y0_ref[...], y1_ref[...], y2_ref[...], y3_ref[...]], axis=1)
    xo_ref[...] = x_ref[...] + _dot(y_all, w_ref[...])

    def body(i, carry):
        r = pl.multiple_of(i * rb, rb)
        x = xo_ref[pl.ds(r, rb), :]
        ms = jnp.mean(x * x, axis=-1, keepdims=True)
        ho_ref[pl.ds(r, rb), :] = (x * lax.rsqrt(ms + EPS) * nw_ref[...]).astype(ho_ref.dtype)
        return carry
    lax.fori_loop(0, xo_ref.shape[0] // rb, body, 0)


def _shared_rows(m, tm, shared):
    if shared is None:
        return m, 0, (), ()
    total, row0, buf = shared
    if buf is None:
        return total, row0 // tm, (), ()
    return total, row0 // tm, (buf,), (pl.BlockSpec(memory_space=pl.ANY),)


def _outproj(ys, w_all, layer, x, norm_w, norm_dtype, *, tm, shared=None):
    m, d = x.shape
    yspec = pl.BlockSpec((tm, BRANCH), lambda i: (i, 0))
    xspec = pl.BlockSpec((tm, d), lambda i: (i, 0))
    rows, off, extra, extra_specs = _shared_rows(m, tm, shared)
    return pl.pallas_call(
        functools.partial(_outproj_kernel, rb=min(tm, 64)),
        grid=(m // tm,),
        in_specs=[yspec, yspec, yspec, yspec,
                  pl.BlockSpec((None, D_MIX, d), lambda i: (layer, 0, 0), pipeline_mode=pl.Buffered(1)),
                  xspec, pl.BlockSpec((1, d), lambda i: (0, 0)), *extra_specs],
        out_specs=[xspec, pl.BlockSpec((tm, d), lambda i: (i + off, 0))],
        out_shape=[jax.ShapeDtypeStruct((m, d), F32), jax.ShapeDtypeStruct((rows, d), norm_dtype)],
        input_output_aliases={7: 1} if extra else {},
        compiler_params=_cparams("parallel"),
        name="outproj",
    )(*ys, w_all, x, norm_w, *extra)


def _rmsnorm_kernel(x_ref, w_ref, *rest):
    o_ref = rest[-1]
    x = x_ref[...]
    ms = jnp.mean(x * x, axis=-1, keepdims=True)
    o_ref[...] = (x * lax.rsqrt(ms + EPS) * w_ref[...]).astype(o_ref.dtype)


def _rmsnorm(x, w, out_dtype, *, tm, shared=None):
    m, d = x.shape
    rows, off, extra, extra_specs = _shared_rows(m, tm, shared)
    return pl.pallas_call(
        _rmsnorm_kernel,
        grid=(m // tm,),
        in_specs=[pl.BlockSpec((tm, d), lambda i: (i, 0)),
                  pl.BlockSpec((1, d), lambda i: (0, 0)), *extra_specs],
        out_specs=pl.BlockSpec((tm, d), lambda i: (i + off, 0)),
        out_shape=jax.ShapeDtypeStruct((rows, d), out_dtype),
        input_output_aliases={2: 0} if extra else {},
        compiler_params=_cparams("parallel"),
        name="rmsnorm",
    )(x, w, *extra)


def _gla_chunk_heads(qs, ks, get_v, log2fs, get_st, tri, lvl, n_heads):
    n = len(qs)
    c, width = qs[0].shape
    bs = [_sel_left(tri, lf) for lf in log2fs]

    def ref_rows(b, rows, span):
        return jnp.concatenate(
            [jnp.broadcast_to(b[r:r + 1, :], (span, width)) for r in rows], axis=0)

    def rows_only(x, lo, hi):
        parts = [jnp.zeros((lo, width), BF16), x[lo:hi, :], jnp.zeros((c - hi, width), BF16)]
        return jnp.concatenate([p for p in parts if p.shape[0]], axis=0)

    s_diag = []
    for h in range(n):
        ed = bs[h] - ref_rows(bs[h], range(SUB // 2, c, SUB), SUB)
        s_diag.append(_dot_nt((qs[h] * jnp.exp2(ed)).astype(BF16), (ks[h] * jnp.exp2(-ed)).astype(BF16)))

    s_off = []
    for h in range(n):
        lq, lk = [], []
        half = SUB
        while half < c:
            ref = ref_rows(bs[h], range(half - 1, c, 2 * half), 2 * half)
            mixed = jnp.concatenate(
                [(ks[h] if (r // half) % 2 == 0 else qs[h])[r:r + half, :] for r in range(0, c, half)],
                axis=0)
            x = (mixed * jnp.exp2(-jnp.abs(bs[h] - ref))).astype(BF16)
            for r0 in range(0, c, 2 * half):
                lk.append(rows_only(x, r0, r0 + half))
                lq.append(rows_only(x, r0 + half, r0 + 2 * half))
            half *= 2
        s_off.append(_dot_nt(jnp.concatenate(lq, axis=1), jnp.concatenate(lk, axis=1)))

    outs, sts = [], {}
    for i in range(n):
        head = i % n_heads
        scores = jnp.where(lvl == 1, s_diag[i], s_off[i])
        b = bs[i]
        b_last = b[c - 1:c, :]
        q_in = (qs[i] * jnp.exp2(b)).astype(BF16)
        k_end = (ks[i] * jnp.exp2(b_last - b)).astype(BF16)
        v = get_v(i)
        st = sts[head] if head in sts else get_st(head)
        outs.append(_dot(scores.astype(BF16), v.astype(BF16)) + _dot_nt(q_in, st.astype(BF16)))
        sts[head] = st * jnp.exp2(b_last) + _dot(v.T.astype(BF16), k_end)
    return outs, [sts[h] for h in range(n_heads)]


def _hgrn_prompt_kernel(q_ref, f_ref, i_ref, g_ref, lb_ref, nw_ref, tri_ref, lvl_ref,
                        y_ref, s_ref, st_scr):
    c = pl.program_id(1)

    @pl.when(c == 0)
    def _():
        st_scr[...] = jnp.zeros_like(st_scr)

    items = [(slice(cc * CHUNK, (cc + 1) * CHUNK), slice(h * HG_DK, (h + 1) * HG_DK))
             for cc in range(q_ref.shape[0] // CHUNK) for h in range(HG_HEADS)]
    qs, ks, log2fs = [], [], []
    for rs, sl in items:
        k = (0.5 - 0.5 * lb_ref[:, sl]) * (1.0 - jnp.tanh(0.5 * f_ref[rs, sl]))
        ks.append(k)
        log2fs.append(jnp.log2(jnp.maximum(1.0 - k, TINY)))
        qs.append(_silu(q_ref[rs, sl]))
    outs, new_sts = _gla_chunk_heads(qs, ks, lambda i: i_ref[items[i][0], items[i][1]], log2fs,
                                     lambda h: st_scr[h], tri_ref[...], lvl_ref[...], HG_HEADS)
    for h in range(HG_HEADS):
        st_scr[h] = new_sts[h]
    for (rs, sl), o in zip(items, outs):
        y = _group_norm(o, nw_ref[:, sl], HG_DV) * _silu(g_ref[rs, sl])
        y_ref[rs, sl] = y.astype(BF16)

    @pl.when(c == pl.num_programs(1) - 1)
    def _():
        for h in range(HG_HEADS):
            s_ref[0, h] = st_scr[h].T


def _gla_rows(seq):
    for n in (4, 2):
        if seq % (n * CHUNK) == 0:
            return n * CHUNK
    return CHUNK


def _hgrn_prompt(proj, lb, nw, bsz, seq):
    rows = _gla_rows(seq)
    nc = seq // rows
    blk = lambda j: pl.BlockSpec((rows, BRANCH), lambda b, c, j=j: (b * nc + c, j))
    row = pl.BlockSpec((1, BRANCH), lambda b, c: (0, 0))
    cc = pl.BlockSpec((CHUNK, CHUNK), lambda b, c: (0, 0))
    c3 = pl.BlockSpec((CHUNK, 3 * CHUNK), lambda b, c: (0, 0))
    return pl.pallas_call(
        _hgrn_prompt_kernel,
        grid=(bsz, nc),
        in_specs=[blk(BLK_HG_Q), blk(BLK_HG_F), blk(BLK_HG_I), blk(BLK_HG_G), row, row, c3, cc],
        out_specs=[pl.BlockSpec((rows, BRANCH), lambda b, c: (b * nc + c, 0)),
                   pl.BlockSpec((1, HG_HEADS, HG_DK, HG_DV), lambda b, c: (b, 0, 0, 0))],
        out_shape=[jax.ShapeDtypeStruct((bsz * seq, BRANCH), BF16),
                   jax.ShapeDtypeStruct((bsz, HG_HEADS, HG_DK, HG_DV), F32)],
        scratch_shapes=[pltpu.VMEM((HG_HEADS, HG_DV, HG_DK), F32)],
        compiler_params=_cparams("parallel", "arbitrary"),
        name="hgrn_prompt",
    )(proj, proj, proj, proj, lb, nw, _tri_const(CHUNK), _level_const(CHUNK))


def _gla_prompt_kernel(q_ref, k_ref, v_ref, g_ref, sm_ref, wup_ref, bup_ref, nw_ref,
                       tri_ref, lvl_ref, y_ref, s_ref, st_scr):
    c = pl.program_id(1)

    @pl.when(c == 0)
    def _():
        st_scr[...] = jnp.zeros_like(st_scr)

    up = _dot(sm_ref[...].astype(BF16), wup_ref[...]) + bup_ref[...]
    log2_a = -_softplus(-up) * (LOG2E / GLA_TAU)
    items = [(slice(cc * CHUNK, (cc + 1) * CHUNK), slice(h * GLA_DK, (h + 1) * GLA_DK),
              slice(h * GLA_DV, (h + 1) * GLA_DV))
             for cc in range(q_ref.shape[0] // CHUNK) for h in range(GLA_HEADS)]
    outs, new_sts = _gla_chunk_heads(
        [q_ref[rs, ks] * (GLA_DK ** -0.5) for rs, ks, _ in items], [k_ref[rs, ks] for rs, ks, _ in items],
        lambda i: v_ref[items[i][0], items[i][2]], [log2_a[rs, ks] for rs, ks, _ in items],
        lambda h: st_scr[h], tri_ref[...], lvl_ref[...], GLA_HEADS)
    for h in range(GLA_HEADS):
        st_scr[h] = new_sts[h]
    for (rs, _, vs), o in zip(items, outs):
        y = _group_norm(o, nw_ref[:, vs], GLA_DV) * _silu(g_ref[rs, vs])
        y_ref[rs, vs] = y.astype(BF16)

    @pl.when(c == pl.num_programs(1) - 1)
    def _():
        for h in range(GLA_HEADS):
            s_ref[0, h] = st_scr[h].T


def _gla_prompt(proj, wup, bup, nw, bsz, seq):
    rows = _gla_rows(seq)
    nc = seq // rows
    blk = lambda w, j: pl.BlockSpec((rows, w), lambda b, c, j=j: (b * nc + c, j))
    const = lambda shape: pl.BlockSpec(shape, lambda b, c: (0,) * len(shape))
    return pl.pallas_call(
        _gla_prompt_kernel,
        grid=(bsz, nc),
        in_specs=[blk(GLA_KDIM, BLK_GL_Q), blk(GLA_KDIM, BLK_GL_K), blk(BRANCH, BLK_GL_V),
                  blk(BRANCH, BLK_GL_G), blk(128, BLK_SMALL),
                  const((128, GLA_KDIM)), const((1, GLA_KDIM)), const((1, BRANCH)),
                  const((CHUNK, 3 * CHUNK)), const((CHUNK, CHUNK))],
        out_specs=[pl.BlockSpec((rows, BRANCH), lambda b, c: (b * nc + c, 0)),
                   pl.BlockSpec((1, GLA_HEADS, GLA_DK, GLA_DV), lambda b, c: (b, 0, 0, 0))],
        out_shape=[jax.ShapeDtypeStruct((bsz * seq, BRANCH), BF16),
                   jax.ShapeDtypeStruct((bsz, GLA_HEADS, GLA_DK, GLA_DV), F32)],
        scratch_shapes=[pltpu.VMEM((GLA_HEADS, GLA_DV, GLA_DK), F32)],
        compiler_params=_cparams("parallel", "arbitrary"),
        name="gla_prompt",
    )(proj, proj, proj, proj, proj, wup, bup, nw, _tri_const(CHUNK), _level_const(CHUNK))


def _chunk_conv(x_ref, carry, w_ref, b_ref, first):
    c = x_ref.shape[0]
    last = CONV_W - 1

    @pl.when(first)
    def _():
        carry[...] = jnp.zeros_like(carry)

    x = x_ref[...]
    y = b_ref[...] + x * w_ref[last:last + 1, :]
    for s in range(1, CONV_W):
        y = y + pltpu.roll(x, s, 0) * w_ref[last - s:last - s + 1, :]
    ext = jnp.concatenate([carry[...], x[0:8, :]], axis=0)
    head = b_ref[...] + ext[8:16, :] * w_ref[last:last + 1, :]
    for s in range(1, CONV_W):
        head = head + ext[8 - s:16 - s, :] * w_ref[last - s:last - s + 1, :]
    tail = x[c - 8:c, :]
    carry[...] = tail
    return jnp.concatenate([head, y[8:, :]], axis=0), tail[8 - last:8, :]


def _rglru_gates(xc, wr_ref, br_ref, wi_ref, bi_ref, lam_ref):
    a_parts, u_parts = [], []
    for n in range(RG_BLOCKS):
        sl = slice(n * RG_BW, (n + 1) * RG_BW)
        xb = xc[:, sl]
        xb16 = xb.astype(BF16)
        r = _sigmoid(_dot(xb16, wr_ref[n]) + br_ref[:, sl])
        i = _sigmoid(_dot(xb16, wi_ref[n]) + bi_ref[:, sl])
        log_a = -RG_C * r * _softplus(-lam_ref[:, sl])
        a = jnp.exp(log_a)
        one_m_a2 = -jnp.tanh(log_a) * (a * a + 1.0)
        a_parts.append(a)
        u_parts.append(jnp.sqrt(jnp.maximum(one_m_a2, 0.0)) * (i * xb))
    return a_parts, u_parts


def _rglru_prompt_kernel(x_ref, g_ref, cw_ref, cb_ref, wr_ref, br_ref, wi_ref, bi_ref, lam_ref,
                         y_ref, h_ref, cs_ref, buf, h_scr):
    c = pl.program_id(1)
    first = c == 0

    @pl.when(first)
    def _():
        h_scr[...] = jnp.zeros_like(h_scr)

    xc, tail = _chunk_conv(x_ref, buf, cw_ref, cb_ref, first)
    cs_ref[0] = tail
    a_parts, u_parts = _rglru_gates(xc, wr_ref, br_ref, wi_ref, bi_ref, lam_ref)
    n_rows = xc.shape[0]
    ng = n_rows // 8
    sub = lax.broadcasted_iota(jnp.int32, (ng, 8, RG_BW), 1)
    for n in range(RG_BLOCKS):
        sl = slice(n * RG_BW, (n + 1) * RG_BW)
        a = a_parts[n].reshape(ng, 8, RG_BW)
        u = u_parts[n].reshape(ng, 8, RG_BW)
        s = 1
        while s < 8:
            keep = sub >= s
            a_sh = jnp.where(keep, pltpu.roll(a, s, 1), 1.0)
            u_sh = jnp.where(keep, pltpu.roll(u, s, 1), 0.0)
            u = a * u_sh + u
            a = a * a_sh
            s *= 2
        h = h_scr[:, sl]
        groups = []
        for j in range(ng):
            hj = a[j] * h + u[j]
            groups.append(hj)
            h = hj[7:8, :]
        h_scr[:, sl] = h
        y_ref[:, sl] = (jnp.concatenate(groups, axis=0) * _silu(g_ref[:, sl])).astype(BF16)
    h_ref[0] = h_scr[...]


def _rglru_prompt(proj, cw, cb, wr, br, wi, bi, lam, bsz, seq):
    rows = _gla_rows(seq)
    nc = seq // rows
    blk = lambda j: pl.BlockSpec((rows, BRANCH), lambda b, c, j=j: (b * nc + c, j))
    const = lambda shape: pl.BlockSpec(shape, lambda b, c: (0,) * len(shape))
    return pl.pallas_call(
        _rglru_prompt_kernel,
        grid=(bsz, nc),
        in_specs=[blk(BLK_RG_X), blk(BLK_RG_G), const((CONV_W, BRANCH)), const((1, BRANCH)),
                  const((RG_BLOCKS, RG_BW, RG_BW)), const((1, BRANCH)),
                  const((RG_BLOCKS, RG_BW, RG_BW)), const((1, BRANCH)), const((1, BRANCH))],
        out_specs=[pl.BlockSpec((rows, BRANCH), lambda b, c: (b * nc + c, 0)),
                   pl.BlockSpec((1, 1, BRANCH), lambda b, c: (b, 0, 0)),
                   pl.BlockSpec((1, CONV_W - 1, BRANCH), lambda b, c: (b, 0, 0))],
        out_shape=[jax.ShapeDtypeStruct((bsz * seq, BRANCH), BF16),
                   jax.ShapeDtypeStruct((bsz, 1, BRANCH), F32),
                   jax.ShapeDtypeStruct((bsz, CONV_W - 1, BRANCH), F32)],
        scratch_shapes=[pltpu.VMEM((8, BRANCH), F32), pltpu.VMEM((1, BRANCH), F32)],
        compiler_params=_cparams("parallel", "arbitrary"),
        name="rglru_prompt",
    )(proj, proj, cw, cb, wr, br, wi, bi, lam)


def _ssd_prompt_kernel(z_ref, x_ref, bc_ref, sm_ref, cwx_ref, cbx_ref, cwb_ref, cbb_ref,
                       dtb_ref, a_ref, d_ref, nw_ref, tri_ref, exp_ref,
                       y_ref, s_ref, cs_ref, st_scr, xbuf, bcbuf):
    c = pl.program_id(1)
    first = c == 0
    n_chunks = x_ref.shape[0] // CHUNK
    gw = BRANCH // SSD_G
    hpg = SSD_HEADS // SSD_G

    @pl.when(first)
    def _():
        st_scr[...] = jnp.zeros_like(st_scr)

    xc, xtail = _chunk_conv(x_ref, xbuf, cwx_ref, cbx_ref, first)
    bcc, bctail = _chunk_conv(bc_ref, bcbuf, cwb_ref, cbb_ref, first)
    cs_ref[0, :, 0:BRANCH] = xtail
    cs_ref[0, :, BRANCH:SSD_CONV_DIM] = bctail
    xs_all = _silu(xc)
    bcs_all = _silu(bcc)
    dt_all = _softplus(sm_ref[...] + dtb_ref[...])

    tri = tri_ref[...]
    expand = exp_ref[...]
    a2 = a_ref[...] * LOG2E
    t_idx = lax.broadcasted_iota(jnp.int32, (CHUNK, CHUNK), 0)
    s_idx = lax.broadcasted_iota(jnp.int32, (CHUNK, CHUNK), 1)
    causal = s_idx <= t_idx
    lane = lax.broadcasted_iota(jnp.int32, (CHUNK, 2 * SSD_P), 1)

    chunks = []
    for k in range(n_chunks):
        rs = slice(k * CHUNK, (k + 1) * CHUNK)
        xs, bcs, dt = xs_all[rs, :], bcs_all[rs, :], dt_all[rs, :]
        cum = _sel_left(tri, dt * a2)
        cum_t = cum.T
        dt_x = _sel_right(dt, expand)
        cum_x = _sel_right(cum, expand)
        cum_last = cum_x[CHUNK - 1:CHUNK, :]
        xdt = xs * dt_x
        xw = (xdt * jnp.exp2(cum_last - cum_x)).astype(BF16)
        xdt16 = xdt.astype(BF16)
        b16 = [bcs[:, g * SSD_N:(g + 1) * SSD_N].astype(BF16) for g in range(SSD_G)]
        c16 = [bcs[:, (SSD_G + g) * SSD_N:(SSD_G + g + 1) * SSD_N].astype(BF16) for g in range(SSD_G)]
        bt16 = [bcs[:, g * SSD_N:(g + 1) * SSD_N].T.astype(BF16) for g in range(SSD_G)]
        pairs = []
        for g in range(SSD_G):
            cb = _dot_nt(c16[g], b16[g])
            for pair in range(hpg // 2):
                h0 = g * hpg + 2 * pair
                xp = xdt16[:, h0 * SSD_P:(h0 + 2) * SSD_P]
                outs = []
                for h in (h0, h0 + 1):
                    col = DT_LANE + h
                    seg = cum[:, col:col + 1] - cum_t[col:col + 1, :]
                    lmat = jnp.where(causal, jnp.exp2(jnp.where(causal, seg, 0.0)), 0.0)
                    outs.append(_dot((cb * lmat).astype(BF16), xp))
                pairs.append(jnp.where(lane < SSD_P, outs[0], outs[1]))
        chunks.append(dict(rs=rs, xs=xs, y_intra=jnp.concatenate(pairs, axis=1),
                           dec_in=jnp.exp2(cum_x), dec_out=jnp.exp2(cum_last), xw=xw, c16=c16, bt16=bt16))

    st = [st_scr[:, g * gw:(g + 1) * gw] for g in range(SSD_G)]
    for ck in chunks:
        y_inter = []
        for g in range(SSD_G):
            gs = slice(g * gw, (g + 1) * gw)
            y_inter.append(_dot(ck["c16"][g], st[g].astype(BF16)) * ck["dec_in"][:, gs])
            st[g] = st[g] * ck["dec_out"][:, gs] + _dot(ck["bt16"][g], ck["xw"][:, gs])
        ck["y"] = ck["y_intra"] + jnp.concatenate(y_inter, axis=1)
    for g in range(SSD_G):
        st_scr[:, g * gw:(g + 1) * gw] = st[g]

    for ck in chunks:
        rs = ck["rs"]
        y = (ck["y"] + ck["xs"] * d_ref[...]) * _silu(z_ref[rs, :])
        y_ref[rs, :] = _group_norm(y, nw_ref[...], gw).astype(BF16)

    @pl.when(c == pl.num_programs(1) - 1)
    def _():
        s_ref[0] = st_scr[...].T.reshape(SSD_HEADS, SSD_P, SSD_N)


def _ssd_expand_const(width):
    e = np.zeros((128, SSD_HEADS * width), np.float32)
    for h in range(SSD_HEADS):
        e[DT_LANE + h, h * width:(h + 1) * width] = 1.0
    return jnp.asarray(np.tile(e, (3, 1)), dtype=BF16)


def _ssd_prompt(proj, cwx, cbx, cwb, cbb, dtb, a_pad, d_x, nw, bsz, seq):
    rows = CHUNK
    nc = seq // rows
    blk = lambda w, j: pl.BlockSpec((rows, w), lambda b, c, j=j: (b * nc + c, j))
    const = lambda shape: pl.BlockSpec(shape, lambda b, c: (0,) * len(shape))
    return pl.pallas_call(
        _ssd_prompt_kernel,
        grid=(bsz, nc),
        in_specs=[blk(BRANCH, BLK_SS_Z), blk(BRANCH, BLK_SS_X), blk(SSD_BC, BLK_SS_BC),
                  blk(128, BLK_SMALL),
                  const((CONV_W, BRANCH)), const((1, BRANCH)), const((CONV_W, SSD_BC)),
                  const((1, SSD_BC)), const((1, 128)), const((1, 128)), const((1, BRANCH)),
                  const((1, BRANCH)), const((CHUNK, 3 * CHUNK)), const((3 * 128, BRANCH))],
        out_specs=[pl.BlockSpec((rows, BRANCH), lambda b, c: (b * nc + c, 0)),
                   pl.BlockSpec((1, SSD_HEADS, SSD_P, SSD_N), lambda b, c: (b, 0, 0, 0)),
                   pl.BlockSpec((1, CONV_W - 1, SSD_CONV_DIM), lambda b, c: (b, 0, 0))],
        out_shape=[jax.ShapeDtypeStruct((bsz * seq, BRANCH), BF16),
                   jax.ShapeDtypeStruct((bsz, SSD_HEADS, SSD_P, SSD_N), F32),
                   jax.ShapeDtypeStruct((bsz, CONV_W - 1, SSD_CONV_DIM), F32)],
        scratch_shapes=[pltpu.VMEM((SSD_N, BRANCH), F32),
                        pltpu.VMEM((8, BRANCH), F32),
                        pltpu.VMEM((8, SSD_BC), F32)],
        compiler_params=_cparams("parallel", "arbitrary"),
        name="ssd_prompt",
    )(proj, proj, proj, proj, cwx, cbx, cwb, cbb, dtb, a_pad, d_x, nw,
      _tri_const(CHUNK), _ssd_expand_const(SSD_P))


def _step_conv(x, cs_ref, w_ref, b_ref, ncs_ref):
    y = b_ref[...] + x * w_ref[CONV_W - 1:CONV_W, :]
    for j in range(CONV_W - 1):
        y = y + cs_ref[j] * w_ref[j:j + 1, :]
    for j in range(CONV_W - 2):
        ncs_ref[j] = cs_ref[j + 1]
    ncs_ref[CONV_W - 2] = x
    return y


def _sample_pre_kernel(p_ref, lb_ref, rcs_ref, rh_ref, rcw_ref, rcb_ref, wr_ref, br_ref, wi_ref,
                       bi_ref, lam_ref, wup_ref, bup_ref, scs_ref, scw_ref, scb_ref, dtb_ref,
                       a_ref, exp_ref, expw_ref,
                       hq_ref, hk_ref, gq_ref, gd_ref, yrg_ref, nrh_ref, nrcs_ref,
                       sx_ref, sbc_ref, sdx_ref, sda_ref, nscs_ref):
    col = lambda blk, w: slice(blk * w, (blk + 1) * w)
    f = p_ref[:, col(BLK_HG_F, BRANCH)]
    lb = lb_ref[...]
    hq_ref[...] = _silu(p_ref[:, col(BLK_HG_Q, BRANCH)])
    hk_ref[...] = (1.0 - lb) * (1.0 - _sigmoid(f))
    sm = p_ref[:, col(BLK_SMALL, 128)]
    up = _dot(sm.astype(BF16), wup_ref[...]) + bup_ref[...]
    gq_ref[...] = p_ref[:, col(BLK_GL_Q, GLA_KDIM)] * (GLA_DK ** -0.5)
    gd_ref[...] = jnp.exp(-_softplus(-up) * (1.0 / GLA_TAU))
    xc = _step_conv(p_ref[:, col(BLK_RG_X, BRANCH)], rcs_ref, rcw_ref, rcb_ref, nrcs_ref)
    a_parts, u_parts = _rglru_gates(xc, wr_ref, br_ref, wi_ref, bi_ref, lam_ref)
    h = jnp.concatenate(a_parts, axis=1) * rh_ref[...] + jnp.concatenate(u_parts, axis=1)
    nrh_ref[...] = h
    yrg_ref[...] = (h * _silu(p_ref[:, col(BLK_RG_G, BRANCH)])).astype(BF16)
    xbc = jnp.concatenate([p_ref[:, col(BLK_SS_X, BRANCH)], p_ref[:, col(BLK_SS_BC, SSD_BC)]], axis=1)
    xbc = _silu(_step_conv(xbc, scs_ref, scw_ref, scb_ref, nscs_ref))
    xs = xbc[:, 0:BRANCH]
    sx_ref[...] = xs
    sbc_ref[...] = xbc[:, BRANCH:SSD_CONV_DIM]
    dt = _softplus(sm + dtb_ref[...])
    expand = exp_ref[...]
    sdx_ref[...] = xs * _sel_right(dt, expand)
    sda_ref[...] = jnp.exp(_sel_right(dt * a_ref[...], expw_ref[...]))


def _whole(a):
    shape = a.shape
    return pl.BlockSpec(shape, lambda i: (0,) * len(shape))


def _sample_rows(proj, nb):
    return pl.BlockSpec((nb, proj.shape[1]), lambda i: (proj.shape[0] // nb - 1, 0))


def _sample_pre(proj, nb, lb, rcs, rh, rcw, rcb, wr, br, wi, bi, lam, wup, bup, scs, scw, scb,
                dtb, a_pad):
    sd = lambda *shape, dt=F32: jax.ShapeDtypeStruct(shape, dt)
    rest = (lb, rcs, rh, rcw, rcb, wr, br, wi, bi, lam, wup, bup, scs, scw, scb, dtb, a_pad,
            _ssd_expand_const(SSD_P), _ssd_expand_const(SSD_N))
    outs = [sd(nb, BRANCH), sd(nb, BRANCH),
            sd(nb, GLA_KDIM), sd(nb, GLA_KDIM),
            sd(nb, BRANCH, dt=BF16), sd(nb, BRANCH), sd(CONV_W - 1, nb, BRANCH),
            sd(nb, BRANCH), sd(nb, SSD_BC), sd(nb, BRANCH), sd(nb, SSD_HEADS * SSD_N),
            sd(CONV_W - 1, nb, SSD_CONV_DIM)]
    return pl.pallas_call(
        _sample_pre_kernel,
        grid=(1,),
        in_specs=[_sample_rows(proj, nb)] + [_whole(a) for a in rest],
        out_specs=[_whole(o) for o in outs],
        out_shape=outs,
        compiler_params=_cparams("arbitrary"),
        name="sample_pre",
    )(proj, *rest)


def _pad_t(x):
    r = x.shape[0]
    return jnp.concatenate([x, jnp.zeros((128 - r, 128), F32)], axis=0).T


STATE_UNROLL = 4


def _gla_state_kernel(*refs, heads, tied):
    so_ref, o_ref = refs[-2:]
    if tied:
        k_ref, v_ref, q_ref, s_ref = refs[:4]
    else:
        d_ref, k_ref, v_ref, q_ref, s_ref = refs[:5]
    dv = s_ref.shape[-1]

    def body(b, carry):
        kt_ = _pad_t(k_ref[b])
        dt_ = None if tied else _pad_t(d_ref[b])
        vr = v_ref[b]
        qr = q_ref[b]
        for h in range(heads):
            kb = jnp.broadcast_to(kt_[:, h:h + 1], (kt_.shape[0], dv))
            d = jnp.maximum(1.0 - kb, TINY) if tied else dt_[:, h:h + 1]
            s_new = d * s_ref[b, h] + kb * vr[h:h + 1, :]
            so_ref[b, h] = s_new
            q8 = jnp.broadcast_to(qr[h:h + 1, :], (8, qr.shape[1])).astype(BF16)
            o_ref[b, h:h + 1, :] = _dot(q8, s_new.astype(BF16))[0:1, :]
        return carry
    lax.fori_loop(0, s_ref.shape[0], body, 0, unroll=STATE_UNROLL)


def _state_call(kern, name, vec_args, vec_specs, s_all, so_prev, layer, o_shape, o_spec, bb):
    nb = s_all.shape[1]
    st = pl.BlockSpec((None, bb) + s_all.shape[2:], lambda i: (layer, i, 0, 0, 0))
    in_specs = list(vec_specs) + [st]
    args = list(vec_args) + [s_all]
    aliases = {}
    if so_prev is not None:
        in_specs.append(pl.BlockSpec(memory_space=pl.ANY))
        args.append(so_prev)
        aliases = {len(args) - 1: 0}
    return pl.pallas_call(
        kern,
        grid=(nb // bb,),
        in_specs=in_specs,
        out_specs=[st, o_spec],
        out_shape=[jax.ShapeDtypeStruct(s_all.shape, F32), o_shape],
        input_output_aliases=aliases,
        compiler_params=_cparams("parallel"),
        name=name,
    )(*args)


def _gla_state(d, k, v, q, s_all, so_prev, layer, *, bb):
    _, nb, heads, dk, dv = s_all.shape
    vec = lambda w: pl.BlockSpec((bb, heads, w), lambda i: (i, 0, 0))
    tied = d is None
    args = (k, v, q) if tied else (d, k, v, q)
    specs = (vec(dk), vec(dv), vec(dk)) if tied else (vec(dk), vec(dk), vec(dv), vec(dk))
    return _state_call(functools.partial(_gla_state_kernel, heads=heads, tied=tied),
                       "hgrn_state" if tied else "gla_state", args, specs, s_all, so_prev, layer,
                       jax.ShapeDtypeStruct((nb, heads, dv), F32), vec(dv), bb)


def _ssd_state_kernel(da_ref, dx_ref, b_ref, c_ref, s_ref, *rest):
    so_ref, y_ref = rest[-2:]
    hpg = SSD_HEADS // SSD_G

    def body(b, carry):
        ar = da_ref[b]
        xt_ = _pad_t(dx_ref[b])
        br = b_ref[b]
        cr = c_ref[b]
        for h in range(SSD_HEADS):
            g = h // hpg
            rows = slice((h % 2) * SSD_P, (h % 2 + 1) * SSD_P)
            j = h // 2
            s_new = ar[h:h + 1, :] * s_ref[b, h] + xt_[rows, j:j + 1] * br[g:g + 1, :]
            so_ref[b, h] = s_new
            c8 = jnp.broadcast_to(cr[g:g + 1, :], (8, SSD_N)).astype(BF16)
            y_ref[b, h:h + 1, :] = _dot_nt(c8, s_new.astype(BF16))[0:1, :]
        return carry
    lax.fori_loop(0, s_ref.shape[0], body, 0, unroll=STATE_UNROLL)


def _ssd_state(da, dx, bv, cv, s_all, so_prev, layer, *, bb):
    nb = s_all.shape[1]
    vec = lambda r, w: pl.BlockSpec((bb, r, w), lambda i: (i, 0, 0))
    return _state_call(_ssd_state_kernel, "ssd_state", (da, dx, bv, cv),
                       (vec(SSD_HEADS, SSD_N), vec(8, 128), vec(SSD_G, SSD_N), vec(SSD_G, SSD_N)),
                       s_all, so_prev, layer,
                       jax.ShapeDtypeStruct((nb, SSD_HEADS, SSD_P), F32), vec(SSD_HEADS, SSD_P), bb)


def _sample_post_kernel(p_ref, ohg_ref, ogl_ref, yss_ref, sx_ref, d_ref, hnw_ref, gnw_ref,
                        snw_ref, yhg_ref, ygl_ref, yso_ref):
    col = lambda blk: slice(blk * BRANCH, (blk + 1) * BRANCH)
    yhg_ref[...] = (_group_norm(ohg_ref[...], hnw_ref[...], HG_DV)
                    * _silu(p_ref[:, col(BLK_HG_G)])).astype(BF16)
    ygl_ref[...] = (_group_norm(ogl_ref[...], gnw_ref[...], GLA_DV)
                    * _silu(p_ref[:, col(BLK_GL_G)])).astype(BF16)
    y = (yss_ref[...] + sx_ref[...] * d_ref[...]) * _silu(p_ref[:, col(BLK_SS_Z)])
    yso_ref[...] = _group_norm(y, snw_ref[...], BRANCH // SSD_G).astype(BF16)


def _sample_post(proj, nb, ohg, ogl, yss, sx, d_x, hnw, gnw, snw):
    out = jax.ShapeDtypeStruct((nb, BRANCH), BF16)
    rest = (ohg, ogl, yss, sx, d_x, hnw, gnw, snw)
    return pl.pallas_call(
        _sample_post_kernel,
        grid=(1,),
        in_specs=[_sample_rows(proj, nb)] + [_whole(a) for a in rest],
        out_specs=[_whole(out)] * 3,
        out_shape=[out, out, out],
        compiler_params=_cparams("arbitrary"),
        name="sample_post",
    )(proj, *rest)


def _tail_kernel(w_ref, o_ref):
    cols = w_ref.shape[1]
    n_a = ORIG_SSD_Z - ORIG_GLA_A
    n_zx = ORIG_SSD_DT - ORIG_SSD_Z
    n_dt = N_IN - ORIG_SSD_DT
    o_ref[0:n_zx, :] = w_ref[n_a:n_a + n_zx, :].astype(BF16)
    o_ref[n_zx:n_zx + n_a, :] = w_ref[0:n_a, :].astype(BF16)
    o_ref[n_zx + n_a:n_zx + n_a + n_dt, :] = w_ref[n_a + n_zx:n_a + n_zx + n_dt, :].astype(BF16)
    o_ref[n_zx + n_a + n_dt:N_TAIL, :] = jnp.zeros((N_TAIL - n_zx - n_a - n_dt, cols), BF16)


def _prep_w_tail(w_in_t, *, cb=256):
    depth, _, d = w_in_t.shape
    return pl.pallas_call(
        _tail_kernel,
        grid=(depth, d // cb),
        in_specs=[pl.BlockSpec((None, N_TAIL, cb), lambda l, i: (l, N_MAIN // N_TAIL, i))],
        out_specs=pl.BlockSpec((None, N_TAIL, cb), lambda l, i: (l, 0, i)),
        out_shape=jax.ShapeDtypeStruct((depth, N_TAIL, d), BF16),
        compiler_params=_cparams("parallel", "parallel"),
        name="w_tail",
    )(w_in_t)


def _pad_lanes(v, start, width=128):
    out = jnp.zeros((v.shape[0], 1, width), F32)
    return out.at[:, 0, start:start + v.shape[1]].set(v.astype(F32))


def kernel(x_prompt, x_sample, state_hgrn, state_rglru, state_rglru_conv, state_gla, state_ssd, state_ssd_conv, rms_in, w_in, hgrn_lower_bounds, hgrn_norm, rglru_conv_w, rglru_conv_b, rglru_w_r, rglru_b_r, rglru_w_i, rglru_b_i, rglru_lambda, gla_w_up, gla_b_up, gla_norm, ssd_conv_w, ssd_conv_b, ssd_dt_bias, ssd_a_log, ssd_d, ssd_norm, w_out, rms_final):
    bsz, seq, _ = x_prompt.shape
    nb = x_sample.shape[0]
    row = lambda v: v.reshape(DEPTH, 1, -1).astype(F32)

    lb_all = _lower_bounds(hgrn_lower_bounds.astype(F32)).reshape(DEPTH, 1, BRANCH)
    w_in = jnp.swapaxes(w_in.astype(F32), 1, 2)
    w_tail = _prep_w_tail(w_in)
    w_out16 = w_out.astype(BF16)
    wr16 = rglru_w_r.astype(BF16)
    wi16 = rglru_w_i.astype(BF16)
    wup16 = jnp.concatenate(
        [gla_w_up, jnp.zeros((DEPTH, 128 - GLA_RANK, GLA_KDIM), gla_w_up.dtype)], axis=1).astype(BF16)
    dtb = _pad_lanes(ssd_dt_bias, DT_LANE)
    a_pad = _pad_lanes(-jnp.exp(ssd_a_log.astype(F32)), DT_LANE)
    d_x = jnp.repeat(ssd_d.astype(F32), SSD_P, axis=-1).reshape(DEPTH, 1, BRANCH)
    rms_in_r, hnw, gnw, snw = row(rms_in), row(hgrn_norm), row(gla_norm), row(ssd_norm)
    rcb, br, bi, lam, bup = (row(rglru_conv_b), row(rglru_b_r), row(rglru_b_i),
                             row(rglru_lambda), row(gla_b_up))
    scb = row(ssd_conv_b)

    xp = x_prompt.reshape(bsz * seq, D_MODEL)
    xs = x_sample.reshape(nb, D_MODEL)
    n_p = bsz * seq
    n_all = n_p + nb
    assert n_p % nb == 0
    tm_in = 1040 if n_all % 1040 == 0 else n_all
    tm_out = 512 if n_p % 512 == 0 else CHUNK
    bb = next(n for n in (16, 8, 1) if nb % n == 0)
    rf = rms_final.reshape(1, D_MODEL).astype(F32)

    h_all = _rmsnorm(xp, rms_in_r[0], BF16, tm=tm_out, shared=(n_all, 0, None))
    h_all = _rmsnorm(xs, rms_in_r[0], BF16, tm=nb, shared=(n_all, n_p, h_all))
    outs_p = [[] for _ in range(6)]
    outs_s = [[] for _ in range(3)]
    ns_hg = ns_gl = ns_ss = None
    for l in range(DEPTH):
        scw = ssd_conv_w[l].astype(F32)
        last = l + 1 == DEPTH
        next_w, next_dt = (rf, F32) if last else (rms_in_r[l + 1], BF16)
        proj = _inproj(h_all, w_in, w_tail, l, tm=tm_in, tn=1024)
        y_hg, s_hg = _hgrn_prompt(proj, lb_all[l], hnw[l], bsz, seq)
        y_rg, s_rg, s_rgc = _rglru_prompt(proj, rglru_conv_w[l].astype(F32), rcb[l], wr16[l], br[l],
                                          wi16[l], bi[l], lam[l], bsz, seq)
        y_gl, s_gl = _gla_prompt(proj, wup16[l], bup[l], gnw[l], bsz, seq)
        y_ss, s_ss, s_ssc = _ssd_prompt(proj, scw[:, :BRANCH], scb[l][:, :BRANCH], scw[:, BRANCH:],
                                        scb[l][:, BRANCH:], dtb[l], a_pad[l], d_x[l], snw[l], bsz, seq)
        if last:
            xp, hp = _outproj((y_hg, y_rg, y_gl, y_ss), w_out16, l, xp, next_w, next_dt,
                              tm=tm_out // 2)
        else:
            xp, h_all = _outproj((y_hg, y_rg, y_gl, y_ss), w_out16, l, xp, next_w, next_dt,
                                 tm=tm_out, shared=(n_all, 0, None))
        for lst, s in zip(outs_p, (s_hg, s_rg.reshape(bsz, BRANCH), s_rgc, s_gl, s_ss, s_ssc)):
            lst.append(s)

        (hq, hk, gq, gd, yrg_s, nrh, nrcs, sx, sbc, sdx, sda, nscs) = _sample_pre(
            proj, nb, lb_all[l], jnp.swapaxes(state_rglru_conv[l], 0, 1), state_rglru[l],
            rglru_conv_w[l].astype(F32), rcb[l], wr16[l], br[l], wi16[l], bi[l], lam[l],
            wup16[l], bup[l], jnp.swapaxes(state_ssd_conv[l], 0, 1), scw, scb[l], dtb[l], a_pad[l])
        hv = proj[n_p:, BLK_HG_I * BRANCH:(BLK_HG_I + 1) * BRANCH]
        gk = proj[n_p:, BLK_GL_K * GLA_KDIM:(BLK_GL_K + 1) * GLA_KDIM]
        gv = proj[n_p:, BLK_GL_V * BRANCH:(BLK_GL_V + 1) * BRANCH]
        hsh = lambda a: a.reshape(nb, HG_HEADS, -1)
        gsh = lambda a: a.reshape(nb, GLA_HEADS, -1)
        ns_hg, o_hg = _gla_state(None, hsh(hk), hsh(hv), hsh(hq), state_hgrn, ns_hg, l, bb=bb)
        ns_gl, o_gl = _gla_state(gsh(gd), gsh(gk), gsh(gv), gsh(gq), state_gla, ns_gl, l, bb=bb)
        ns_ss, y_ssr = _ssd_state(sda.reshape(nb, SSD_HEADS, SSD_N), sdx.reshape(nb, 8, 128),
                                  sbc[:, :SSD_G * SSD_N].reshape(nb, SSD_G, SSD_N),
                                  sbc[:, SSD_G * SSD_N:].reshape(nb, SSD_G, SSD_N),
                                  state_ssd, ns_ss, l, bb=bb)
        yhg_s, ygl_s, yss_s = _sample_post(proj, nb, o_hg.reshape(nb, BRANCH), o_gl.reshape(nb, BRANCH),
                                           y_ssr.reshape(nb, BRANCH), sx, d_x[l], hnw[l], gnw[l], snw[l])
        if last:
            xs, hs = _outproj((yhg_s, yrg_s, ygl_s, yss_s), w_out16, l, xs, next_w, next_dt, tm=nb)
        else:
            xs, h_all = _outproj((yhg_s, yrg_s, ygl_s, yss_s), w_out16, l, xs, next_w, next_dt, tm=nb,
                                 shared=(n_all, n_p, h_all))
        for lst, s in zip(outs_s, (nrh, jnp.swapaxes(nrcs, 0, 1), jnp.swapaxes(nscs, 0, 1))):
            lst.append(s)

    y_prompt = hp.reshape(bsz, seq, D_MODEL)
    y_sample = hs.reshape(nb, 1, D_MODEL)
    s_rg, s_rgc, s_ssc = (jnp.stack(l) for l in outs_s)
    return ((y_prompt, y_sample) + tuple(jnp.stack(l) for l in outs_p)
            + (ns_hg, s_rg, s_rgc, ns_gl, ns_ss, s_ssc))
```
